```python
import jax, jax.numpy as jnp
from jax import lax
import numpy as np

D_MODEL = 1024
BATCH = 8
SEQ = 4096
DEPTH = 4

CTX_LEN = 256
GRID_W = 64
N_MIXERS = 3
EPS = 1e-6
D_FF = 4 * D_MODEL
N_MOD = 6

GLA_HEADS = 4
GLA_DK = D_MODEL // 2 // GLA_HEADS
GLA_DV = D_MODEL // GLA_HEADS
GLA_GATE_RANK = 16
GLA_GATE_TAU = 16.0
GLA_CHUNK = 64

RNN_WIDTH = D_MODEL
RNN_BLOCKS = 8
RNN_BLOCK_DIM = RNN_WIDTH // RNN_BLOCKS
CONV_WIDTH = 4
CONV_LEFT = 2
LRU_C = 8.0

HEAD_DIM = 128
Q_HEADS = D_MODEL // HEAD_DIM
KV_HEADS = 2
GROUP = Q_HEADS // KV_HEADS
Q_BLOCK = 128
ROPE_THETA = 10000.0

kernel_name = "hybrid_gla_rglru_gqa_dit_prefix"


def rmsnorm(x, g):
    xf = x.astype(jnp.float32)
    y = xf * lax.rsqrt(jnp.mean(xf * xf, axis=-1, keepdims=True) + EPS)
    return (y * g.astype(jnp.float32)).astype(x.dtype)


def adaln(x, g, shift, scale):
    return rmsnorm(x, g) * (1 + scale) + shift


def flip(a):
    return jnp.flip(a, axis=1)


def gla_chunk_scan(q, k, v, log_g, s0, with_output):
    B_, T, H, _ = q.shape
    n = T // GLA_CHUNK

    def chunks(a):
        return jnp.moveaxis(a.astype(jnp.float32).reshape(B_, n, GLA_CHUNK, H, a.shape[-1]), 1, 0)

    in_chunk_mask = jnp.tril(jnp.ones((GLA_CHUNK, GLA_CHUNK), dtype=bool))[None, :, :, None, None]

    def step(s, inp):
        qc, kc, vc, gc = inp
        b = jnp.cumsum(gc, axis=1)
        b_end = b[:, -1]
        s_new = s * jnp.exp(b_end)[..., None] + jnp.einsum(
            'bshk,bshv->bhkv', kc * jnp.exp(b_end[:, None] - b), vc)
        if not with_output:
            return s_new, None
        rel = jnp.where(in_chunk_mask, b[:, :, None] - b[:, None, :], -jnp.inf)
        scores = jnp.einsum('bthk,btshk,bshk->bhts', qc, jnp.exp(rel), kc)
        o = (jnp.einsum('bhts,bshv->bthv', scores, vc)
             + jnp.einsum('bthk,bhkv->bthv', qc * jnp.exp(b), s))
        return s_new, o

    s_fin, o = lax.scan(step, s0, (chunks(q), chunks(k), chunks(v), chunks(log_g)))
    if with_output:
        o = jnp.moveaxis(o, 0, 1).reshape(B_, T, H, -1).astype(v.dtype)
    return s_fin, o


def gla_mixer(h_c, h_l, w_in, w_up_f, b_f, w_up_b, b_b, norm_g, w_o, need_ctx):
    dq = GLA_HEADS * GLA_DK
    dv = GLA_HEADS * GLA_DV
    cuts = [dq, 2 * dq, 2 * dq + dv, 2 * dq + 2 * dv, 2 * dq + 2 * dv + GLA_GATE_RANK]

    def project(h):
        B_, T, _ = h.shape
        q, k, v, r, gf, gb = jnp.split(h @ w_in, cuts, axis=-1)
        heads = lambda a, d: a.reshape(B_, T, GLA_HEADS, d)
        log_f = jax.nn.log_sigmoid((gf @ w_up_f + b_f).astype(jnp.float32)) / GLA_GATE_TAU
        log_b = jax.nn.log_sigmoid((gb @ w_up_b + b_b).astype(jnp.float32)) / GLA_GATE_TAU
        return (heads(q, GLA_DK) * GLA_DK ** -0.5, heads(k, GLA_DK), heads(v, GLA_DV), r,
                heads(log_f, GLA_DK), heads(log_b, GLA_DK))

    def readout(o, r):
        B_, T = o.shape[:2]
        o = rmsnorm(o, norm_g).reshape(B_, T, GLA_HEADS * GLA_DV)
        return (o * jax.nn.silu(r)) @ w_o

    qc, kc, vc, rc, gfc, gbc = project(h_c)
    ql, kl, vl, rl, gfl, gbl = project(h_l)
    s0 = jnp.zeros((h_l.shape[0], GLA_HEADS, GLA_DK, GLA_DV), jnp.float32)
    s_cf, o_cf = gla_chunk_scan(qc, kc, vc, gfc, s0, need_ctx)
    s_cb, o_cb = gla_chunk_scan(flip(qc), flip(kc), flip(vc), flip(gbc), s0, need_ctx)
    _, o_lf = gla_chunk_scan(ql, kl, vl, gfl, s_cf, True)
    _, o_lb = gla_chunk_scan(flip(ql), flip(kl), flip(vl), flip(gbl), s_cb, True)
    out_l = readout(o_lf + flip(o_lb), rl)
    out_c = readout(o_cf + flip(o_cb), rc) if need_ctx else None
    return out_c, out_l


def depthwise_conv(x, w, b):
    T = x.shape[1]
    xp = jnp.pad(x, ((0, 0), (CONV_LEFT, CONV_WIDTH - 1 - CONV_LEFT), (0, 0)))
    y = b
    for j in range(CONV_WIDTH):
        y = y + xp[:, j:j + T] * w[j]
    return y


def block_diag_linear(x, w, b):
    B_, T, _ = x.shape
    y = jnp.einsum('btnd,nde->btne', x.reshape(B_, T, RNN_BLOCKS, RNN_BLOCK_DIM), w)
    return y.reshape(B_, T, RNN_WIDTH) + b


def _linear_combine(e1, e2):
    a1, b1 = e1
    a2, b2 = e2
    return a1 * a2, a2 * b1 + b2


def rg_lru(x, w_a, b_a, w_x, b_x, lam, h0):
    xf = x.astype(jnp.float32)
    r = jax.nn.sigmoid(block_diag_linear(xf, w_a, b_a))
    i = jax.nn.sigmoid(block_diag_linear(xf, w_x, b_x))
    log_a = -LRU_C * r * jax.nn.softplus(-lam.astype(jnp.float32))
    a = jnp.exp(log_a)
    u = jnp.sqrt(-jnp.expm1(2.0 * log_a)) * (i * xf)
    a_cum, h = lax.associative_scan(_linear_combine, (a, u), axis=1)
    h = h + a_cum * h0[:, None, :]
    return h, h[:, -1]


def lru_mixer(h_c, h_l, w_in, conv_w, conv_b, wa_f, ba_f, wx_f, bx_f, lam_f,
              wa_b, ba_b, wx_b, bx_b, lam_b, w_o, need_ctx):
    def branches(h):
        gate, xb = jnp.split(h @ w_in, 2, axis=-1)
        return gate, depthwise_conv(xb, conv_w, conv_b)

    def readout(hs, gate):
        return (hs * jax.nn.gelu(gate.astype(jnp.float32))).astype(gate.dtype) @ w_o

    gc, xc = branches(h_c)
    gl, xl = branches(h_l)
    h0 = jnp.zeros((h_l.shape[0], RNN_WIDTH), jnp.float32)
    hcf, scf = rg_lru(xc, wa_f, ba_f, wx_f, bx_f, lam_f, h0)
    hcb, scb = rg_lru(flip(xc), wa_b, ba_b, wx_b, bx_b, lam_b, h0)
    hlf, _ = rg_lru(xl, wa_f, ba_f, wx_f, bx_f, lam_f, scf)
    hlb, _ = rg_lru(flip(xl), wa_b, ba_b, wx_b, bx_b, lam_b, scb)
    out_l = readout(hlf + flip(hlb), gl)
    out_c = readout(hcf + flip(hcb), gc) if need_ctx else None
    return out_c, out_l


def rope_2d_tables(T):
    n_rows = T // GRID_W
    row = jnp.repeat(jnp.arange(n_rows), GRID_W)
    col = jnp.tile(jnp.arange(GRID_W), n_rows)
    n_pairs_axis = HEAD_DIM // 4
    inv_freq = ROPE_THETA ** (-jnp.arange(n_pairs_axis, dtype=jnp.float32) / n_pairs_axis)
    ang = jnp.concatenate([row[:, None] * inv_freq, col[:, None] * inv_freq], axis=-1)
    return jnp.cos(ang)[None, :, None, :], jnp.sin(ang)[None, :, None, :]


def apply_rope(x, cos, sin):
    xf = x.astype(jnp.float32)
    x1, x2 = jnp.split(xf, 2, axis=-1)
    return jnp.concatenate([x1 * cos - x2 * sin, x1 * sin + x2 * cos], axis=-1).astype(x.dtype)


def gqa_attend(q, k, v):
    s = jnp.einsum('bqhgd,bkhd->bhgqk', q, k).astype(jnp.float32) * HEAD_DIM ** -0.5
    p = jax.nn.softmax(s, axis=-1).astype(v.dtype)
    return jnp.einsum('bhgqk,bkhd->bqhgd', p, v)


def attn_mixer(h_c, h_l, w_in, q_g, k_g, w_o, need_ctx):
    def project(h):
        B_, T, _ = h.shape
        q, k, v = jnp.split(h @ w_in, [Q_HEADS * HEAD_DIM, (Q_HEADS + KV_HEADS) * HEAD_DIM], axis=-1)
        q = rmsnorm(q.reshape(B_, T, Q_HEADS, HEAD_DIM), q_g)
        k = rmsnorm(k.reshape(B_, T, KV_HEADS, HEAD_DIM), k_g)
        return q, k, v.reshape(B_, T, KV_HEADS, HEAD_DIM)

    qc, kc, vc = project(h_c)
    ql, kl, vl = project(h_l)
    B_, T = h_l.shape[:2]
    cos, sin = rope_2d_tables(T)
    ql = apply_rope(ql, cos, sin)
    kl = apply_rope(kl, cos, sin)
    k_all = jnp.concatenate([kc, kl], axis=1)
    v_all = jnp.concatenate([vc, vl], axis=1)
    qb = jnp.swapaxes(ql.reshape(B_, T // Q_BLOCK, Q_BLOCK, KV_HEADS, GROUP, HEAD_DIM), 0, 1)
    ob = lax.map(lambda qq: gqa_attend(qq, k_all, v_all), qb)
    out_l = jnp.swapaxes(ob, 0, 1).reshape(B_, T, Q_HEADS * HEAD_DIM) @ w_o
    out_c = None
    if need_ctx:
        Tc = h_c.shape[1]
        oc = gqa_attend(qc.reshape(B_, Tc, KV_HEADS, GROUP, HEAD_DIM), kc, vc)
        out_c = oc.reshape(B_, Tc, Q_HEADS * HEAD_DIM) @ w_o
    return out_c, out_l


def sq_relu_mlp(h, w1, w2):
    return jnp.square(jax.nn.relu(h @ w1)) @ w2


def setup_inputs(seed: int = 0) -> dict:
    key = jax.random.key(seed)
    ks = iter(jax.random.split(key, 64))

    def nrm(shape, scale):
        return jax.random.normal(next(ks), shape, jnp.float32) * scale

    def gain(shape):
        return 1.0 + nrm(shape, 0.02)

    def lru_lambda(n):
        a0 = jax.random.uniform(next(ks), (n, RNN_WIDTH), jnp.float32, minval=0.9, maxval=0.999)
        s = a0 ** (1.0 / LRU_C)
        return jnp.log(s) - jnp.log1p(-s)

    n_a, n_b, n_c = (len(range(kind, DEPTH, N_MIXERS)) for kind in range(N_MIXERS))
    D = D_MODEL
    gla_in = 2 * GLA_HEADS * GLA_DK + 2 * GLA_HEADS * GLA_DV + 2 * GLA_GATE_RANK
    gla_kw = GLA_HEADS * GLA_DK
    bd = RNN_BLOCK_DIM ** -0.5
    return {
        "x": nrm((BATCH, SEQ, D), 1.0),
        "c": nrm((BATCH, D), 1.0),
        "ctx": nrm((BATCH, CTX_LEN, D), 1.0),
        "c_ctx": nrm((D,), 1.0),
        "norm_mix_g": gain((DEPTH, D)),
        "norm_mlp_g": gain((DEPTH, D)),
        "w_mod": nrm((DEPTH, D, N_MOD * D), 0.5 * D ** -0.5),
        "b_mod": nrm((DEPTH, N_MOD * D), 0.02),
        "w_mlp1": nrm((DEPTH, D, D_FF), D ** -0.5),
        "w_mlp2": nrm((DEPTH, D_FF, D), D_FF ** -0.5),
        "gla_w_in": nrm((n_a, D, gla_in), D ** -0.5),
        "gla_w_up_f": nrm((n_a, GLA_GATE_RANK, gla_kw), GLA_GATE_RANK ** -0.5),
        "gla_b_f": nrm((n_a, gla_kw), 0.1),
        "gla_w_up_b": nrm((n_a, GLA_GATE_RANK, gla_kw), GLA_GATE_RANK ** -0.5),
        "gla_b_b": nrm((n_a, gla_kw), 0.1),
        "gla_norm_g": gain((n_a, GLA_DV)),
        "gla_w_o": nrm((n_a, GLA_HEADS * GLA_DV, D), (GLA_HEADS * GLA_DV) ** -0.5),
        "lru_w_in": nrm((n_b, D, 2 * RNN_WIDTH), D ** -0.5),
        "lru_conv_w": nrm((n_b, CONV_WIDTH, RNN_WIDTH), 0.5),
        "lru_conv_b": nrm((n_b, RNN_WIDTH), 0.02),
        "lru_wa_f": nrm((n_b, RNN_BLOCKS, RNN_BLOCK_DIM, RNN_BLOCK_DIM), bd),
        "lru_ba_f": nrm((n_b, RNN_WIDTH), 0.1),
        "lru_wx_f": nrm((n_b, RNN_BLOCKS, RNN_BLOCK_DIM, RNN_BLOCK_DIM), bd),
        "lru_bx_f": nrm((n_b, RNN_WIDTH), 0.1),
        "lru_lam_f": lru_lambda(n_b),
        "lru_wa_b": nrm((n_b, RNN_BLOCKS, RNN_BLOCK_DIM, RNN_BLOCK_DIM), bd),
        "lru_ba_b": nrm((n_b, RNN_WIDTH), 0.1),
        "lru_wx_b": nrm((n_b, RNN_BLOCKS, RNN_BLOCK_DIM, RNN_BLOCK_DIM), bd),
        "lru_bx_b": nrm((n_b, RNN_WIDTH), 0.1),
        "lru_lam_b": lru_lambda(n_b),
        "lru_w_o": nrm((n_b, RNN_WIDTH, D), RNN_WIDTH ** -0.5),
        "attn_w_in": nrm((n_c, D, (Q_HEADS + 2 * KV_HEADS) * HEAD_DIM), D ** -0.5),
        "attn_q_g": gain((n_c, HEAD_DIM)),
        "attn_k_g": gain((n_c, HEAD_DIM)),
        "attn_w_o": nrm((n_c, Q_HEADS * HEAD_DIM, D), (Q_HEADS * HEAD_DIM) ** -0.5),
        "final_g": gain((D,)),
    }


def reference(x, c, ctx, c_ctx, norm_mix_g, norm_mlp_g, w_mod, b_mod, w_mlp1, w_mlp2,
              gla_w_in, gla_w_up_f, gla_b_f, gla_w_up_b, gla_b_b, gla_norm_g, gla_w_o,
              lru_w_in, lru_conv_w, lru_conv_b, lru_wa_f, lru_ba_f, lru_wx_f, lru_bx_f, lru_lam_f,
              lru_wa_b, lru_ba_b, lru_wx_b, lru_bx_b, lru_lam_b, lru_w_o,
              attn_w_in, attn_q_g, attn_k_g, attn_w_o, final_g):
    silu_c = jax.nn.silu(c)
    silu_cc = jax.nn.silu(c_ctx)
    for i in range(DEPTH):
        need_ctx = i < DEPTH - 1
        m_l = (silu_c @ w_mod[i] + b_mod[i])[:, None, :]
        m_c = (silu_cc @ w_mod[i] + b_mod[i])[None, None, :]
        sh1, sc1, g1, sh2, sc2, g2 = jnp.split(m_l, N_MOD, axis=-1)
        csh1, csc1, cg1, csh2, csc2, cg2 = jnp.split(m_c, N_MOD, axis=-1)
        h_l = adaln(x, norm_mix_g[i], sh1, sc1)
        h_c = adaln(ctx, norm_mix_g[i], csh1, csc1)
        kind = i % N_MIXERS
        j = i // N_MIXERS
        if kind == 0:
            o_c, o_l = gla_mixer(h_c, h_l, gla_w_in[j], gla_w_up_f[j], gla_b_f[j], gla_w_up_b[j],
                                 gla_b_b[j], gla_norm_g[j], gla_w_o[j], need_ctx)
        elif kind == 1:
            o_c, o_l = lru_mixer(h_c, h_l, lru_w_in[j], lru_conv_w[j], lru_conv_b[j],
                                 lru_wa_f[j], lru_ba_f[j], lru_wx_f[j], lru_bx_f[j], lru_lam_f[j],
                                 lru_wa_b[j], lru_ba_b[j], lru_wx_b[j], lru_bx_b[j], lru_lam_b[j],
                                 lru_w_o[j], need_ctx)
        else:
            o_c, o_l = attn_mixer(h_c, h_l, attn_w_in[j], attn_q_g[j], attn_k_g[j], attn_w_o[j], need_ctx)
        x = x + g1 * o_l
        x = x + g2 * sq_relu_mlp(adaln(x, norm_mlp_g[i], sh2, sc2), w_mlp1[i], w_mlp2[i])
        if need_ctx:
            ctx = ctx + cg1 * o_c
            ctx = ctx + cg2 * sq_relu_mlp(adaln(ctx, norm_mlp_g[i], csh2, csc2), w_mlp1[i], w_mlp2[i])
    return rmsnorm(x, final_g)
```

```python
import functools

import numpy as np
import jax
import jax.numpy as jnp
from jax import lax
from jax.experimental import pallas as pl
from jax.experimental.pallas import tpu as pltpu

F32 = jnp.float32
BF16 = jnp.bfloat16

EPS = 1e-6
N_MOD = 6
N_MIXERS = 3
MOD_ROWS = 16

GLA_HEADS = 4
GLA_DK = 128
GLA_DV = 256
GLA_GATE_RANK = 16
GLA_GATE_TAU = 16.0
GLA_GATE_PAD = 128

RNN_BLOCKS = 8
RNN_BLOCK_DIM = 128
CONV_WIDTH = 4
CONV_LEFT = 2
LRU_C = 8.0

HEAD_DIM = 128
Q_HEADS = 8
KV_HEADS = 2
GROUP = Q_HEADS // KV_HEADS
GRID_W = 64
ROPE_THETA = 10000.0

VMEM_LIMIT_BYTES = 56 * 1024 * 1024


def _cparams(*sem):
    return pltpu.CompilerParams(dimension_semantics=sem, vmem_limit_bytes=VMEM_LIMIT_BYTES)


def _dot(a, b):
    return jnp.dot(a, b, preferred_element_type=F32)


def _dot_nt(a, b):
    return lax.dot_general(a, b, (((1,), (1,)), ((), ())), preferred_element_type=F32)


def _dot_tn(a, b):
    return lax.dot_general(a, b, (((0,), (0,)), ((), ())), preferred_element_type=F32)


def _rms(x, g):
    return x * lax.rsqrt(jnp.mean(x * x, axis=-1, keepdims=True) + EPS) * g


def _adaln(x, g, shift, scale):
    return _rms(x, g) * (1.0 + scale) + shift


def _mod_kernel(s_ref, w_ref, b_ref, o_ref):
    s = s_ref[...]
    s = s * jax.nn.sigmoid(s)
    o_ref[...] = _dot(s.astype(BF16), w_ref[...].astype(BF16)) + b_ref[...]


def _modulation(c, c_ctx, w_mod, b_mod):
    depth, d, nd = w_mod.shape
    bsz = c.shape[0]
    rows = jnp.concatenate([c_ctx[None, :], c, jnp.zeros((MOD_ROWS - 1 - bsz, d), F32)], axis=0)
    tn = min(nd, 1536)
    out = pl.pallas_call(
        _mod_kernel,
        grid=(depth, nd // tn),
        in_specs=[pl.BlockSpec((MOD_ROWS, d), lambda l, j: (0, 0)),
                  pl.BlockSpec((None, d, tn), lambda l, j: (l, 0, j)),
                  pl.BlockSpec((None, 1, tn), lambda l, j: (l, 0, j))],
        out_specs=pl.BlockSpec((None, MOD_ROWS, tn), lambda l, j: (l, 0, j)),
        out_shape=jax.ShapeDtypeStruct((depth, MOD_ROWS, nd), F32),
        compiler_params=_cparams("arbitrary", "arbitrary"),
        name="modulation",
    )(rows, w_mod, b_mod.reshape(depth, 1, nd))
    return out.reshape(depth, MOD_ROWS, N_MOD, d)


def _mod_spec(layer, d, mod_index):
    return pl.BlockSpec((None, None, N_MOD, d), lambda i, *_: (layer, mod_index(i), 0, 0))


def _ln_matmul_kernel(x_ref, mod_ref, g_ref, *refs, n_out, mod_off, col_chunk):
    w_refs, o_refs = refs[:n_out], refs[n_out:]
    h = _adaln(x_ref[...], g_ref[...], mod_ref[mod_off:mod_off + 1, :],
               mod_ref[mod_off + 1:mod_off + 2, :]).astype(BF16)
    for w_ref, o_ref in zip(w_refs, o_refs):
        n = w_ref.shape[1]
        for n0 in range(0, n, col_chunk):
            n1 = min(n, n0 + col_chunk)
            o_ref[:, n0:n1] = _dot(h, w_ref[:, n0:n1]).astype(o_ref.dtype)


def _ln_matmul(x2d, mod, layer, mod_index, gain, ws, out_dtypes, tm, time_major=None, name="ln_matmul"):
    rows, d = x2d.shape
    in_specs = [pl.BlockSpec((tm, d), lambda i: (i, 0)), _mod_spec(layer, d, mod_index),
                pl.BlockSpec((1, d), lambda i: (0, 0))]
    in_specs += [pl.BlockSpec(w.shape, lambda i: (0, 0)) for w in ws]
    if time_major is None:
        out_specs = [pl.BlockSpec((tm, w.shape[1]), lambda i: (i, 0)) for w in ws]
        out_shape = [jax.ShapeDtypeStruct((rows, w.shape[1]), dt) for w, dt in zip(ws, out_dtypes)]
    else:
        seq, bsz = time_major
        tpb = seq // tm
        out_specs = [pl.BlockSpec((tm, w.shape[1]), lambda i: (i % tpb, i // tpb)) for w in ws]
        out_shape = [jax.ShapeDtypeStruct((seq, bsz * w.shape[1]), dt) for w, dt in zip(ws, out_dtypes)]
    return pl.pallas_call(
        functools.partial(_ln_matmul_kernel, n_out=len(ws), mod_off=0, col_chunk=512),
        grid=(rows // tm,),
        in_specs=in_specs, out_specs=out_specs, out_shape=out_shape,
        compiler_params=_cparams("arbitrary"),
        name=name,
    )(x2d, mod, gain.reshape(1, d), *ws)


def _proj_res_kernel(y_ref, w_ref, x_ref, mod_ref, o_ref):
    o_ref[...] = x_ref[...] + mod_ref[2:3, :] * _dot(y_ref[...], w_ref[...])


def _gla_proj_res_kernel(of_ref, ob_ref, r_ref, ng_ref, w_ref, x_ref, mod_ref, o_ref):
    o = of_ref[...] + ob_ref[...]
    r = r_ref[...].astype(F32)
    ys = []
    for h in range(GLA_HEADS):
        sl = slice(h * GLA_DV, (h + 1) * GLA_DV)
        rh = r[:, sl]
        ys.append(_rms(o[:, sl], ng_ref[...]) * (rh * jax.nn.sigmoid(rh)))
    y = jnp.concatenate(ys, axis=1).astype(BF16)
    o_ref[...] = x_ref[...] + mod_ref[2:3, :] * _dot(y, w_ref[...])


def _lru_proj_res_kernel(hf_ref, hb_ref, gate_ref, w_ref, x_ref, mod_ref, o_ref):
    y = ((hf_ref[...] + hb_ref[...]) * jax.nn.gelu(gate_ref[...].astype(F32))).astype(BF16)
    o_ref[...] = x_ref[...] + mod_ref[2:3, :] * _dot(y, w_ref[...])


def _proj_res(kind, operands, operand_specs, w, x2d, mod, layer, mod_index, tm, extra=(), extra_specs=()):
    rows, d = x2d.shape
    body = {"plain": _proj_res_kernel, "gla": _gla_proj_res_kernel, "lru": _lru_proj_res_kernel}[kind]
    n_in = len(operands) + len(extra) + 1
    return pl.pallas_call(
        body,
        grid=(rows // tm,),
        in_specs=[*operand_specs, *extra_specs, pl.BlockSpec(w.shape, lambda i: (0, 0)),
                  pl.BlockSpec((tm, d), lambda i: (i, 0)), _mod_spec(layer, d, mod_index)],
        out_specs=pl.BlockSpec((tm, d), lambda i: (i, 0)),
        out_shape=jax.ShapeDtypeStruct((rows, d), F32),
        input_output_aliases={n_in: 0},
        compiler_params=_cparams("arbitrary"),
        name=kind + "_proj_res",
    )(*operands, *extra, w, x2d, mod)


def _mlp_kernel(x_ref, mod_ref, g_ref, w1_ref, w2_ref, o_ref, h_sc, acc_sc):
    j = pl.program_id(1)

    @pl.when(j == 0)
    def _():
        h_sc[...] = _adaln(x_ref[...], g_ref[...], mod_ref[3:4, :], mod_ref[4:5, :]).astype(BF16)
        acc_sc[...] = jnp.zeros_like(acc_sc)

    a = jnp.maximum(_dot(h_sc[...], w1_ref[...]), 0.0)
    acc_sc[...] += _dot((a * a).astype(BF16), w2_ref[...])

    @pl.when(j == pl.num_programs(1) - 1)
    def _():
        o_ref[...] = x_ref[...] + mod_ref[5:6, :] * acc_sc[...]


def _mlp(x2d, mod, layer, mod_index, gain, w1, w2, tm, tf):
    rows, d = x2d.shape
    dff = w1.shape[1]
    return pl.pallas_call(
        _mlp_kernel,
        grid=(rows // tm, dff // tf),
        in_specs=[pl.BlockSpec((tm, d), lambda i, j: (i, 0)), _mod_spec(layer, d, mod_index),
                  pl.BlockSpec((1, d), lambda i, j: (0, 0)),
                  pl.BlockSpec((d, tf), lambda i, j: (0, j)),
                  pl.BlockSpec((tf, d), lambda i, j: (j, 0))],
        out_specs=pl.BlockSpec((tm, d), lambda i, j: (i, 0)),
        out_shape=jax.ShapeDtypeStruct((rows, d), F32),
        scratch_shapes=[pltpu.VMEM((tm, d), BF16), pltpu.VMEM((tm, d), F32)],
        input_output_aliases={0: 0},
        compiler_params=_cparams("arbitrary", "arbitrary"),
        name="mlp",
    )(x2d, mod, gain.reshape(1, d), w1, w2)


def _final_norm_kernel(x_ref, g_ref, o_ref):
    o_ref[...] = _rms(x_ref[...], g_ref[...])


def _final_norm(x2d, gain, tm):
    rows, d = x2d.shape
    return pl.pallas_call(
        _final_norm_kernel,
        grid=(rows // tm,),
        in_specs=[pl.BlockSpec((tm, d), lambda i: (i, 0)), pl.BlockSpec((1, d), lambda i: (0, 0))],
        out_specs=pl.BlockSpec((tm, d), lambda i: (i, 0)),
        out_shape=jax.ShapeDtypeStruct((rows, d), F32),
        compiler_params=_cparams("arbitrary"),
        name="final_norm",
    )(x2d, gain.reshape(1, d))


def _gla_tables(chunk, reverse):
    levels = int(np.log2(chunk))
    idx = np.arange(chunk)
    t, r = idx[:, None], idx[None, :]
    if not reverse:
        q_incl = (r <= t)
        k_rest = (r > t)
    else:
        q_incl = (r >= t)
        k_rest = (r < t)
    mats = [q_incl, k_rest]
    masks = [np.eye(chunk, dtype=bool)]
    for lv in range(1, levels + 1):
        m = 1 << (lv - 1)
        mid = (idx // (2 * m)) * (2 * m) + m
        upper = idx >= mid
        midc = mid[:, None]
        if not reverse:
            pat = np.where(upper[:, None], (r >= midc) & (r <= t), (r > t) & (r < midc))
            mask = (upper[:, None] & ~upper[None, :])
        else:
            pat = np.where(upper[:, None], (r >= midc) & (r < t), (r >= t) & (r < midc))
            mask = (~upper[:, None] & upper[None, :])
        mask = mask & ((idx[:, None] // (2 * m)) == (idx[None, :] // (2 * m)))
        mats.append(pat)
        masks.append(mask)
    a = np.concatenate(mats, axis=0).astype(np.float32)
    return jnp.asarray(a, BF16), jnp.asarray(np.stack(masks).astype(np.float32), F32)


def _gla_kernel(qkvr_ref, gf_ref, wup_ref, bup_ref, a_ref, mask_ref, s0_ref, o_ref, sfin_ref,
                s_sc, sc_sc, *, reverse):
    c = pl.program_id(1)
    chunk = qkvr_ref.shape[0]
    levels = mask_ref.shape[0] - 1
    dq = GLA_HEADS * GLA_DK

    @pl.when(c == 0)
    def _():
        s_sc[...] = s0_ref[...]

    graw = _dot(gf_ref[...], wup_ref[...]) + bup_ref[...]
    g = (jnp.minimum(graw, 0.0) - jnp.log1p(jnp.exp(-jnp.abs(graw)))) * (1.0 / GLA_GATE_TAU)
    g_hi = g.astype(BF16)
    g_lo = (g - g_hi.astype(F32)).astype(BF16)

    def partial_sums(block):
        a = a_ref[block * chunk:(block + 1) * chunk, :]
        return _dot(a, g_hi) + _dot(a, g_lo)

    q_all = qkvr_ref[:, 0:dq].astype(F32) * (GLA_DK ** -0.5)
    k_all = qkvr_ref[:, dq:2 * dq].astype(F32)
    q_bf = q_all.astype(BF16)
    k_bf = k_all.astype(BF16)
    for h in range(GLA_HEADS):
        sl = slice(h * GLA_DK, (h + 1) * GLA_DK)
        sc_sc[h] = mask_ref[0] * _dot_nt(q_bf[:, sl], k_bf[:, sl])

    row = lax.broadcasted_iota(jnp.int32, (chunk, 1), 0)
    for lv in range(1, levels + 1):
        m = 1 << (lv - 1)
        use_q = ((row // m) % 2) == (0 if reverse else 1)
        z = (jnp.where(use_q, q_all, k_all) * jnp.exp(partial_sums(1 + lv))).astype(BF16)
        for h in range(GLA_HEADS):
            zh = z[:, h * GLA_DK:(h + 1) * GLA_DK]
            sc_sc[h] += mask_ref[lv] * _dot_nt(zh, zh)

    bq = partial_sums(0)
    bk = partial_sums(1)
    btot = bq[0:1, :] if reverse else bq[chunk - 1:chunk, :]
    qd = (q_all * jnp.exp(bq)).astype(BF16)
    kd = (k_all * jnp.exp(bk)).astype(BF16)
    etot = jnp.exp(btot)
    for h in range(GLA_HEADS):
        sl = slice(h * GLA_DK, (h + 1) * GLA_DK)
        vh = qkvr_ref[:, 2 * dq + h * GLA_DV:2 * dq + (h + 1) * GLA_DV]
        s_old = s_sc[h]
        o_ref[:, h * GLA_DV:(h + 1) * GLA_DV] = (_dot(sc_sc[h].astype(BF16), vh)
                                                + _dot(qd[:, sl], s_old.astype(BF16)))
        ecol = jnp.transpose(jnp.broadcast_to(etot[:, sl], (GLA_DK, GLA_DK)))
        s_sc[h] = s_old * jnp.concatenate([ecol] * (GLA_DV // GLA_DK), axis=1) + _dot_tn(kd[:, sl], vh)

    @pl.when(c == pl.num_programs(1) - 1)
    def _():
        sfin_ref[...] = s_sc[...]


def _gla_scan(qkvr, gfeat, wup, bup, s0, bsz, chunk, reverse):
    rows = qkvr.shape[0]
    nc = rows // bsz // chunk
    a_tab, masks = _gla_tables(chunk, reverse)
    dv = GLA_HEADS * GLA_DV

    def row_block(b, c):
        return (b * nc + (nc - 1 - c if reverse else c), 0)

    state_spec = pl.BlockSpec((None, GLA_HEADS, GLA_DK, GLA_DV), lambda b, c: (b, 0, 0, 0))
    return pl.pallas_call(
        functools.partial(_gla_kernel, reverse=reverse),
        grid=(bsz, nc),
        in_specs=[pl.BlockSpec((chunk, qkvr.shape[1]), row_block),
                  pl.BlockSpec((chunk, gfeat.shape[1]), row_block),
                  pl.BlockSpec(wup.shape, lambda b, c: (0, 0)),
                  pl.BlockSpec(bup.shape, lambda b, c: (0, 0)),
                  pl.BlockSpec(a_tab.shape, lambda b, c: (0, 0)),
                  pl.BlockSpec(masks.shape, lambda b, c: (0, 0, 0)),
                  state_spec],
        out_specs=[pl.BlockSpec((chunk, dv), row_block), state_spec],
        out_shape=[jax.ShapeDtypeStruct((rows, dv), F32),
                   jax.ShapeDtypeStruct((bsz, GLA_HEADS, GLA_DK, GLA_DV), F32)],
        scratch_shapes=[pltpu.VMEM((GLA_HEADS, GLA_DK, GLA_DV), F32),
                        pltpu.VMEM((GLA_HEADS, chunk, chunk), F32)],
        compiler_params=_cparams("arbitrary", "arbitrary"),
        name="gla_scan_bwd" if reverse else "gla_scan_fwd",
    )(qkvr, gfeat, wup, bup, a_tab, masks, s0)


def _lru_kernel(x_ref, prev_ref, next_ref, cw_ref, cb_ref, wcat_ref, ba_ref, bx_ref, lam_ref, h0_ref,
                hs_ref, hfin_ref, a_sc, u_sc, h_sc, *, reverse):
    i = pl.program_id(0)
    n = pl.num_programs(0)
    tidx = n - 1 - i if reverse else i
    tt, bsz, width = x_ref.shape

    @pl.when(i == 0)
    def _():
        h_sc[...] = h0_ref[...]

    xp = prev_ref[...].astype(F32) * (tidx > 0).astype(F32)
    xn = next_ref[...].astype(F32) * (tidx < n - 1).astype(F32)
    xcat = jnp.concatenate([xp, x_ref[...].astype(F32), xn], axis=0)
    lam = lam_ref[...]
    sp = jnp.maximum(-lam, 0.0) + jnp.log1p(jnp.exp(-jnp.abs(lam)))
    for nb in range(RNN_BLOCKS):
        sl = slice(nb * RNN_BLOCK_DIM, (nb + 1) * RNN_BLOCK_DIM)
        xc = cb_ref[:, sl].reshape(1, 1, RNN_BLOCK_DIM)
        for j in range(CONV_WIDTH):
            xc = xc + xcat[j:j + tt, :, sl] * cw_ref[j:j + 1, sl].reshape(1, 1, RNN_BLOCK_DIM)
        xc2 = xc.reshape(tt * bsz, RNN_BLOCK_DIM)
        ri = _dot(xc2.astype(BF16), wcat_ref[nb])
        r = jax.nn.sigmoid(ri[:, :RNN_BLOCK_DIM] + ba_ref[:, sl])
        ig = jax.nn.sigmoid(ri[:, RNN_BLOCK_DIM:] + bx_ref[:, sl])
        log_a = (-LRU_C) * r * sp[:, sl]
        th = jnp.tanh(log_a)
        one_minus_a2 = (-2.0 * th) / (1.0 - th)
        a_sc[:, :, sl] = jnp.exp(log_a).reshape(tt, bsz, RNN_BLOCK_DIM)
        u_sc[:, :, sl] = (jnp.sqrt(one_minus_a2) * (ig * xc2)).reshape(tt, bsz, RNN_BLOCK_DIM)

    def step(k, h):
        t = tt - 1 - k if reverse else k
        h = a_sc[t] * h + u_sc[t]
        hs_ref[t] = h
        return h

    h_last = lax.fori_loop(0, tt, step, h_sc[...], unroll=8)
    h_sc[...] = h_last

    @pl.when(i == n - 1)
    def _():
        hfin_ref[...] = h_last


def _lru_scan(p_tm, conv_w, conv_b, wcat, ba, bx, lam, h0, tt, reverse):
    seq, bsz, two_w = p_tm.shape
    width = two_w // 2
    n = seq // tt

    def tix(i):
        return n - 1 - i if reverse else i

    vec = lambda: pl.BlockSpec((1, width), lambda i: (0, 0))
    return pl.pallas_call(
        functools.partial(_lru_kernel, reverse=reverse),
        grid=(n,),
        in_specs=[pl.BlockSpec((tt, bsz, width), lambda i: (tix(i), 0, 1)),
                  pl.BlockSpec((2, bsz, width), lambda i: (jnp.maximum(tix(i) * (tt // 2) - 1, 0), 0, 1)),
                  pl.BlockSpec((1, bsz, width), lambda i: (jnp.minimum((tix(i) + 1) * tt, seq - 1), 0, 1)),
                  pl.BlockSpec((CONV_WIDTH, width), lambda i: (0, 0)), vec(),
                  pl.BlockSpec(wcat.shape, lambda i: (0, 0, 0)), vec(), vec(), vec(),
                  pl.BlockSpec((bsz, width), lambda i: (0, 0))],
        out_specs=[pl.BlockSpec((tt, bsz, width), lambda i: (tix(i), 0, 0)),
                   pl.BlockSpec((bsz, width), lambda i: (0, 0))],
        out_shape=[jax.ShapeDtypeStruct((seq, bsz, width), F32),
                   jax.ShapeDtypeStruct((bsz, width), F32)],
        scratch_shapes=[pltpu.VMEM((tt, bsz, width), F32), pltpu.VMEM((tt, bsz, width), F32),
                        pltpu.VMEM((bsz, width), F32)],
        compiler_params=_cparams("arbitrary"),
        name="lru_scan_bwd" if reverse else "lru_scan_fwd",
    )(p_tm, p_tm, p_tm, conv_w, conv_b.reshape(1, width), wcat, ba.reshape(1, width),
      bx.reshape(1, width), lam.reshape(1, width), h0)


def _rope(x, cos, sin):
    return x * cos + pltpu.roll(x, HEAD_DIM // 2, axis=1) * sin


def _attn_kernel(q_ref, kvc_ref, *rest, has_latent, tk):
    if has_latent:
        kvl_ref, cos_ref, sin_ref, qg_ref, kg_ref, o_ref, k_sc, v_sc = rest
    else:
        qg_ref, kg_ref, o_ref, k_sc, v_sc = rest
    i = pl.program_id(1)
    tq = q_ref.shape[0]
    n_ctx = kvc_ref.shape[0]
    kv_w = KV_HEADS * HEAD_DIM
    n_lat_tiles = (k_sc.shape[1] - n_ctx) // tk if has_latent else 0

    @pl.when(i == 0)
    def _():
        for hk in range(KV_HEADS):
            ksl = slice(hk * HEAD_DIM, (hk + 1) * HEAD_DIM)
            vsl = slice(kv_w + hk * HEAD_DIM, kv_w + (hk + 1) * HEAD_DIM)
            k_sc[hk, 0:n_ctx, :] = _rms(kvc_ref[:, ksl].astype(F32), kg_ref[...]).astype(BF16)
            v_sc[hk, 0:n_ctx, :] = kvc_ref[:, vsl]
            if has_latent:
                def fill(t, carry):
                    r0 = pl.multiple_of(t * tk, tk)
                    kl = _rms(kvl_ref[pl.ds(r0, tk), ksl].astype(F32), kg_ref[...])
                    kl = _rope(kl, cos_ref[pl.ds(r0, tk), :], sin_ref[pl.ds(r0, tk), :])
                    k_sc[hk, pl.ds(n_ctx + r0, tk), :] = kl.astype(BF16)
                    v_sc[hk, pl.ds(n_ctx + r0, tk), :] = kvl_ref[pl.ds(r0, tk), vsl]
                    return carry
                lax.fori_loop(0, n_lat_tiles, fill, 0)

    for hk in range(KV_HEADS):
        qs = []
        for gi in range(GROUP):
            hd = hk * GROUP + gi
            qh = _rms(q_ref[:, hd * HEAD_DIM:(hd + 1) * HEAD_DIM].astype(F32), qg_ref[...])
            if has_latent:
                q0 = pl.multiple_of(i * tq, tq)
                qh = _rope(qh, cos_ref[pl.ds(q0, tq), :], sin_ref[pl.ds(q0, tq), :])
            qs.append((qh * (HEAD_DIM ** -0.5)).astype(BF16))
        qcat = jnp.concatenate(qs, axis=0)

        def tile(kt, vt, carry):
            m, l, acc = carry
            s = _dot_nt(qcat, kt)
            m_new = jnp.maximum(m, jnp.max(s, axis=-1, keepdims=True))
            p = jnp.exp(s - m_new)
            alpha = jnp.exp(m - m_new)
            l = alpha * l + jnp.sum(p, axis=-1, keepdims=True)
            acc = alpha * acc + _dot(p.astype(BF16), vt)
            return m_new, l, acc

        rows = GROUP * tq
        carry = (jnp.full((rows, 1), -jnp.inf, F32), jnp.zeros((rows, 1), F32),
                 jnp.zeros((rows, HEAD_DIM), F32))
        carry = tile(k_sc[hk, 0:n_ctx, :], v_sc[hk, 0:n_ctx, :], carry)
        if has_latent:
            def body(t, cr):
                r0 = pl.multiple_of(n_ctx + t * tk, tk)
                return tile(k_sc[hk, pl.ds(r0, tk), :], v_sc[hk, pl.ds(r0, tk), :], cr)
            carry = lax.fori_loop(0, n_lat_tiles, body, carry)
        _, l, acc = carry
        o = acc / l
        for gi in range(GROUP):
            hd = hk * GROUP + gi
            o_ref[:, hd * HEAD_DIM:(hd + 1) * HEAD_DIM] = o[gi * tq:(gi + 1) * tq].astype(o_ref.dtype)


def _attention(q, kv_ctx, kv_lat, cos, sin, q_g, k_g, bsz, tq, tk):
    rows = q.shape[0]
    nq = rows // bsz // tq
    n_ctx = kv_ctx.shape[0] // bsz
    has_latent = kv_lat is not None
    n_keys = n_ctx + (kv_lat.shape[0] // bsz if has_latent else 0)
    vec = lambda: pl.BlockSpec((1, HEAD_DIM), lambda b, i: (0, 0))
    in_specs = [pl.BlockSpec((tq, q.shape[1]), lambda b, i: (b * nq + i, 0)),
                pl.BlockSpec((n_ctx, kv_ctx.shape[1]), lambda b, i: (b, 0))]
    args = [q, kv_ctx]
    if has_latent:
        in_specs += [pl.BlockSpec((n_keys - n_ctx, kv_lat.shape[1]), lambda b, i: (b, 0)),
                     pl.BlockSpec(cos.shape, lambda b, i: (0, 0)),
                     pl.BlockSpec(sin.shape, lambda b, i: (0, 0))]
        args += [kv_lat, cos, sin]
    in_specs += [vec(), vec()]
    args += [q_g.reshape(1, HEAD_DIM), k_g.reshape(1, HEAD_DIM)]
    return pl.pallas_call(
        functools.partial(_attn_kernel, has_latent=has_latent, tk=tk),
        grid=(bsz, nq),
        in_specs=in_specs,
        out_specs=pl.BlockSpec((tq, q.shape[1]), lambda b, i: (b * nq + i, 0)),
        out_shape=jax.ShapeDtypeStruct(q.shape, BF16),
        scratch_shapes=[pltpu.VMEM((KV_HEADS, n_keys, HEAD_DIM), BF16),
                        pltpu.VMEM((KV_HEADS, n_keys, HEAD_DIM), BF16)],
        compiler_params=_cparams("arbitrary", "arbitrary"),
        name="attention" if has_latent else "attention_ctx",
    )(*args)


def _rope_tables(seq):
    n_rows = seq // GRID_W
    row = jnp.repeat(jnp.arange(n_rows), GRID_W)
    col = jnp.tile(jnp.arange(GRID_W), n_rows)
    n_pairs_axis = HEAD_DIM // 4
    inv_freq = ROPE_THETA ** (-jnp.arange(n_pairs_axis, dtype=F32) / n_pairs_axis)
    ang = jnp.concatenate([row[:, None] * inv_freq, col[:, None] * inv_freq], axis=-1)
    cos, sin = jnp.cos(ang), jnp.sin(ang)
    return jnp.concatenate([cos, cos], axis=-1), jnp.concatenate([-sin, sin], axis=-1)


def _tile(n, pref):
    t = min(n, pref)
    assert n % t == 0, (n, pref)
    return t


def kernel(x, c, ctx, c_ctx, norm_mix_g, norm_mlp_g, w_mod, b_mod, w_mlp1, w_mlp2, gla_w_in, gla_w_up_f, gla_b_f, gla_w_up_b, gla_b_b, gla_norm_g, gla_w_o, lru_w_in, lru_conv_w, lru_conv_b, lru_wa_f, lru_ba_f, lru_wx_f, lru_bx_f, lru_lam_f, lru_wa_b, lru_ba_b, lru_wx_b, lru_bx_b, lru_lam_b, lru_w_o, attn_w_in, attn_q_g, attn_k_g, attn_w_o, final_g):
    bsz, seq, d = x.shape
    n_ctx = ctx.shape[1]
    depth = w_mod.shape[0]
    assert bsz + 1 <= MOD_ROWS and d == GLA_HEADS * GLA_DV == Q_HEADS * HEAD_DIM

    mod = _modulation(c, c_ctx, w_mod, b_mod)
    xl = x.reshape(bsz * seq, d)
    xc = ctx.reshape(bsz * n_ctx, d)

    tm_l = _tile(seq, 512)
    tm_c = _tile(n_ctx, 256)
    lat_idx = lambda i: 1 + (i * tm_l) // seq
    ctx_idx = lambda i: 0

    for layer in range(depth):
        need_ctx = layer < depth - 1
        kind, j = layer % N_MIXERS, layer // N_MIXERS
        gmix = norm_mix_g[layer]

        if kind == 0:
            dq = GLA_HEADS * GLA_DK
            main_w = gla_w_in[j][:, :2 * dq + 2 * d].astype(BF16)
            gate_w = jnp.pad(gla_w_in[j][:, 2 * dq + 2 * d:], ((0, 0), (0, GLA_GATE_PAD - 2 * GLA_GATE_RANK))).astype(BF16)
            zpad = jnp.zeros((GLA_GATE_PAD - 2 * GLA_GATE_RANK, dq), F32)
            wup_f = jnp.concatenate([gla_w_up_f[j], jnp.zeros_like(gla_w_up_b[j]), zpad], axis=0).astype(BF16)
            wup_b = jnp.concatenate([jnp.zeros_like(gla_w_up_f[j]), gla_w_up_b[j], zpad], axis=0).astype(BF16)
            bup_f, bup_b = gla_b_f[j].reshape(1, dq), gla_b_b[j].reshape(1, dq)
            w_o = gla_w_o[j].astype(BF16)
            proj = {}
            for name, (xs, midx, tm, slen) in zip(("lat", "ctx"), [(xl, lat_idx, tm_l, seq), (xc, ctx_idx, tm_c, n_ctx)]):
                proj[name] = _ln_matmul(xs, mod, layer, midx, gmix, [main_w, gate_w], [BF16, BF16], tm,
                                        name="gla_in_" + name)
            s0 = jnp.zeros((bsz, GLA_HEADS, GLA_DK, GLA_DV), F32)
            ch_c, ch_l = _tile(n_ctx, 128), _tile(seq, 128)
            ocf, s_cf = _gla_scan(*proj["ctx"], wup_f, bup_f, s0, bsz, ch_c, False)
            ocb, s_cb = _gla_scan(*proj["ctx"], wup_b, bup_b, s0, bsz, ch_c, True)
            olf, _ = _gla_scan(*proj["lat"], wup_f, bup_f, s_cf, bsz, ch_l, False)
            olb, _ = _gla_scan(*proj["lat"], wup_b, bup_b, s_cb, bsz, ch_l, True)
            ng = gla_norm_g[j].reshape(1, GLA_DV)

            def gla_out(of, ob, qkvr, xs, midx, tm):
                blk = lambda i: (i, 0)
                return _proj_res("gla", [of, ob, qkvr], [pl.BlockSpec((tm, d), blk), pl.BlockSpec((tm, d), blk),
                                                        pl.BlockSpec((tm, d), lambda i: (i, (2 * dq + d) // d))],
                                 w_o, xs, mod, layer, midx, tm, extra=[ng],
                                 extra_specs=[pl.BlockSpec((1, GLA_DV), lambda i: (0, 0))])

            xl = gla_out(olf, olb, proj["lat"][0], xl, lat_idx, tm_l)
            if need_ctx:
                xc = gla_out(ocf, ocb, proj["ctx"][0], xc, ctx_idx, tm_c)

        elif kind == 1:
            w_in = lru_w_in[j].astype(BF16)
            w_o = lru_w_o[j].astype(BF16)
            wcat_f = jnp.concatenate([lru_wa_f[j], lru_wx_f[j]], axis=-1).astype(BF16)
            wcat_b = jnp.concatenate([lru_wa_b[j], lru_wx_b[j]], axis=-1).astype(BF16)
            p_l, = _ln_matmul(xl, mod, layer, lat_idx, gmix, [w_in], [BF16], tm_l, time_major=(seq, bsz),
                              name="lru_in_lat")
            p_c, = _ln_matmul(xc, mod, layer, ctx_idx, gmix, [w_in], [BF16], tm_c, time_major=(n_ctx, bsz),
                              name="lru_in_ctx")
            h0 = jnp.zeros((bsz, d), F32)
            fwd = (lru_conv_w[j], lru_conv_b[j], wcat_f, lru_ba_f[j], lru_bx_f[j], lru_lam_f[j])
            bwd = (lru_conv_w[j], lru_conv_b[j], wcat_b, lru_ba_b[j], lru_bx_b[j], lru_lam_b[j])
            tt_c, tt_l = _tile(n_ctx, 64), _tile(seq, 64)
            p_c3, p_l3 = p_c.reshape(n_ctx, bsz, 2 * d), p_l.reshape(seq, bsz, 2 * d)
            hcf, s_cf = _lru_scan(p_c3, *fwd, h0, tt_c, False)
            hcb, s_cb = _lru_scan(p_c3, *bwd, h0, tt_c, True)
            hlf, _ = _lru_scan(p_l3, *fwd, s_cf, tt_l, False)
            hlb, _ = _lru_scan(p_l3, *bwd, s_cb, tt_l, True)

            def lru_out(hf, hb, p_tm, xs, midx, tm, slen):
                tpb = slen // tm
                spec = lambda: pl.BlockSpec((tm, d), lambda i: (i % tpb, i // tpb))
                gate_spec = pl.BlockSpec((tm, d), lambda i: (i % tpb, 2 * (i // tpb)))
                return _proj_res("lru", [hf.reshape(slen, bsz * d), hb.reshape(slen, bsz * d), p_tm],
                                 [spec(), spec(), gate_spec], w_o, xs, mod, layer, midx, tm)

            xl = lru_out(hlf, hlb, p_l, xl, lat_idx, tm_l, seq)
            if need_ctx:
                xc = lru_out(hcf, hcb, p_c, xc, ctx_idx, tm_c, n_ctx)

        else:
            qw = Q_HEADS * HEAD_DIM
            w_q = attn_w_in[j][:, :qw].astype(BF16)
            w_kv = attn_w_in[j][:, qw:].astype(BF16)
            w_o = attn_w_o[j].astype(BF16)
            q_l, kv_l = _ln_matmul(xl, mod, layer, lat_idx, gmix, [w_q, w_kv], [BF16, BF16], tm_l, name="attn_in_lat")
            q_c, kv_c = _ln_matmul(xc, mod, layer, ctx_idx, gmix, [w_q, w_kv], [BF16, BF16], tm_c, name="attn_in_ctx")
            cos, sin = _rope_tables(seq)
            o_l = _attention(q_l, kv_c, kv_l, cos, sin, attn_q_g[j], attn_k_g[j], bsz, _tile(seq, 128), _tile(seq, 512))
            blk = lambda i: (i, 0)
            xl = _proj_res("plain", [o_l], [pl.BlockSpec((tm_l, d), blk)], w_o, xl, mod, layer, lat_idx, tm_l)
            if need_ctx:
                o_c = _attention(q_c, kv_c, None, None, None, attn_q_g[j], attn_k_g[j], bsz, _tile(n_ctx, 128), 0)
                xc = _proj_res("plain", [o_c], [pl.BlockSpec((tm_c, d), blk)], w_o, xc, mod, layer, ctx_idx, tm_c)

        w1, w2 = w_mlp1[layer].astype(BF16), w_mlp2[layer].astype(BF16)
        tf = _tile(w1.shape[1], 1024)
        tmm_l = _tile(seq, 1024)
        xl = _mlp(xl, mod, layer, lambda i: 1 + (i * tmm_l) // seq, norm_mlp_g[layer], w1, w2, tmm_l, tf)
        if need_ctx:
            xc = _mlp(xc, mod, layer, ctx_idx, norm_mlp_g[layer], w1, w2, _tile(bsz * n_ctx, 1024), tf)

    return _final_norm(xl, final_g, tm_l).reshape(bsz, seq, d)
```

```python
import functools

import numpy as np
import jax
import jax.numpy as jnp
from jax import lax
from jax.experimental import pallas as pl
from jax.experimental.pallas import tpu as pltpu

F32 = jnp.float32
BF16 = jnp.bfloat16

EPS = 1e-6
N_MOD = 6
N_MIXERS = 3
MOD_ROWS = 16

GLA_HEADS = 4
GLA_DK = 128
GLA_DV = 256
GLA_GATE_RANK = 16
GLA_GATE_TAU = 16.0
GLA_GATE_PAD = 128

RNN_BLOCKS = 8
RNN_BLOCK_DIM = 128
CONV_WIDTH = 4
CONV_LEFT = 2
LRU_C = 8.0

HEAD_DIM = 128
Q_HEADS = 8
KV_HEADS = 2
GROUP = Q_HEADS // KV_HEADS
GRID_W = 64
ROPE_THETA = 10000.0

VMEM_LIMIT_BYTES = 56 * 1024 * 1024


def _cparams(*sem):
    return pltpu.CompilerParams(dimension_semantics=sem, vmem_limit_bytes=VMEM_LIMIT_BYTES)


def _dot(a, b):
    return jnp.dot(a, b, preferred_element_type=F32)


def _dot_nt(a, b):
    return lax.dot_general(a, b, (((1,), (1,)), ((), ())), preferred_element_type=F32)


def _dot_tn(a, b):
    return lax.dot_general(a, b, (((0,), (0,)), ((), ())), preferred_element_type=F32)


def _rms(x, g):
    return x * lax.rsqrt(jnp.mean(x * x, axis=-1, keepdims=True) + EPS) * g


def _adaln(x, g, shift, scale):
    return _rms(x, g) * (1.0 + scale) + shift


def _mod_kernel(s_ref, w_ref, b_ref, o_ref):
    s = s_ref[...]
    s = s * jax.nn.sigmoid(s)
    o_ref[...] = _dot(s.astype(BF16), w_ref[...].astype(BF16)) + b_ref[...]


def _modulation(c, c_ctx, w_mod, b_mod):
    depth, d, nd = w_mod.shape
    bsz = c.shape[0]
    rows = jnp.concatenate([c_ctx[None, :], c, jnp.zeros((MOD_ROWS - 1 - bsz, d), F32)], axis=0)
    tn = min(nd, 1536)
    out = pl.pallas_call(
        _mod_kernel,
        grid=(depth, nd // tn),
        in_specs=[pl.BlockSpec((MOD_ROWS, d), lambda l, j: (0, 0)),
                  pl.BlockSpec((None, d, tn), lambda l, j: (l, 0, j)),
                  pl.BlockSpec((None, 1, tn), lambda l, j: (l, 0, j))],
        out_specs=pl.BlockSpec((None, MOD_ROWS, tn), lambda l, j: (l, 0, j)),
        out_shape=jax.ShapeDtypeStruct((depth, MOD_ROWS, nd), F32),
        compiler_params=_cparams("arbitrary", "arbitrary"),
        name="modulation",
    )(rows, w_mod, b_mod.reshape(depth, 1, nd))
    return out.reshape(depth, MOD_ROWS, N_MOD, d)


def _mod_spec(layer, d, mod_index):
    return pl.BlockSpec((None, None, N_MOD, d), lambda i, *_: (layer, mod_index(i), 0, 0))


def _ln_matmul_kernel(x_ref, mod_ref, g_ref, *refs, n_out, mod_off, col_chunk):
    w_refs, o_refs = refs[:n_out], refs[n_out:]
    h = _adaln(x_ref[...], g_ref[...], mod_ref[mod_off:mod_off + 1, :],
               mod_ref[mod_off + 1:mod_off + 2, :]).astype(BF16)
    for w_ref, o_ref in zip(w_refs, o_refs):
        n = w_ref.shape[1]
        for n0 in range(0, n, col_chunk):
            n1 = min(n, n0 + col_chunk)
            o_ref[:, n0:n1] = _dot(h, w_ref[:, n0:n1]).astype(o_ref.dtype)


def _ln_matmul(x2d, mod, layer, mod_index, gain, ws, out_dtypes, tm, name="ln_matmul"):
    rows, d = x2d.shape
    in_specs = [pl.BlockSpec((tm, d), lambda i: (i, 0)), _mod_spec(layer, d, mod_index),
                pl.BlockSpec((1, d), lambda i: (0, 0))]
    in_specs += [pl.BlockSpec(w.shape, lambda i: (0, 0)) for w in ws]
    return pl.pallas_call(
        functools.partial(_ln_matmul_kernel, n_out=len(ws), mod_off=0, col_chunk=512),
        grid=(rows // tm,),
        in_specs=in_specs,
        out_specs=[pl.BlockSpec((tm, w.shape[1]), lambda i: (i, 0)) for w in ws],
        out_shape=[jax.ShapeDtypeStruct((rows, w.shape[1]), dt) for w, dt in zip(ws, out_dtypes)],
        compiler_params=_cparams("arbitrary"),
        name=name,
    )(x2d, mod, gain.reshape(1, d), *ws)


def _proj_res_kernel(y_ref, w_ref, x_ref, mod_ref, o_ref):
    o_ref[...] = x_ref[...] + mod_ref[2:3, :] * _dot(y_ref[...], w_ref[...])


def _gla_proj_res_kernel(of_ref, ob_ref, r_ref, ng_ref, w_ref, x_ref, mod_ref, o_ref):
    o = of_ref[...] + ob_ref[...]
    r = r_ref[...].astype(F32)
    ys = []
    for h in range(GLA_HEADS):
        sl = slice(h * GLA_DV, (h + 1) * GLA_DV)
        rh = r[:, sl]
        ys.append(_rms(o[:, sl], ng_ref[...]) * (rh * jax.nn.sigmoid(rh)))
    y = jnp.concatenate(ys, axis=1).astype(BF16)
    o_ref[...] = x_ref[...] + mod_ref[2:3, :] * _dot(y, w_ref[...])


def _lru_proj_res_kernel(hf_ref, hb_ref, gate_ref, w_ref, x_ref, mod_ref, o_ref):
    y = ((hf_ref[...] + hb_ref[...]) * jax.nn.gelu(gate_ref[...].astype(F32))).astype(BF16)
    o_ref[...] = x_ref[...] + mod_ref[2:3, :] * _dot(y, w_ref[...])


def _proj_res(kind, operands, operand_specs, w, x2d, mod, layer, mod_index, tm, extra=(), extra_specs=()):
    rows, d = x2d.shape
    body = {"plain": _proj_res_kernel, "gla": _gla_proj_res_kernel, "lru": _lru_proj_res_kernel}[kind]
    n_in = len(operands) + len(extra) + 1
    return pl.pallas_call(
        body,
        grid=(rows // tm,),
        in_specs=[*operand_specs, *extra_specs, pl.BlockSpec(w.shape, lambda i: (0, 0)),
                  pl.BlockSpec((tm, d), lambda i: (i, 0)), _mod_spec(layer, d, mod_index)],
        out_specs=pl.BlockSpec((tm, d), lambda i: (i, 0)),
        out_shape=jax.ShapeDtypeStruct((rows, d), F32),
        input_output_aliases={n_in: 0},
        compiler_params=_cparams("arbitrary"),
        name=kind + "_proj_res",
    )(*operands, *extra, w, x2d, mod)


def _mlp_kernel(x_ref, mod_ref, g_ref, w1_ref, w2_ref, *rest, final_norm):
    if final_norm:
        fg_ref, o_ref, h_sc, acc_sc = rest
    else:
        o_ref, h_sc, acc_sc = rest
    j = pl.program_id(1)

    @pl.when(j == 0)
    def _():
        h_sc[...] = _adaln(x_ref[...], g_ref[...], mod_ref[3:4, :], mod_ref[4:5, :]).astype(BF16)
        acc_sc[...] = jnp.zeros_like(acc_sc)

    a = jnp.maximum(_dot(h_sc[...], w1_ref[...]), 0.0)
    acc_sc[...] += _dot((a * a).astype(BF16), w2_ref[...])

    @pl.when(j == pl.num_programs(1) - 1)
    def _():
        y = x_ref[...] + mod_ref[5:6, :] * acc_sc[...]
        o_ref[...] = _rms(y, fg_ref[...]) if final_norm else y


def _mlp(x2d, mod, layer, mod_index, gain, w1, w2, tm, tf, final_gain=None):
    rows, d = x2d.shape
    dff = w1.shape[1]
    vec = lambda: pl.BlockSpec((1, d), lambda i, j: (0, 0))
    in_specs = [pl.BlockSpec((tm, d), lambda i, j: (i, 0)), _mod_spec(layer, d, mod_index), vec(),
                pl.BlockSpec((d, tf), lambda i, j: (0, j)),
                pl.BlockSpec((tf, d), lambda i, j: (j, 0))]
    args = [x2d, mod, gain.reshape(1, d), w1, w2]
    if final_gain is not None:
        in_specs.append(vec())
        args.append(final_gain.reshape(1, d))
    return pl.pallas_call(
        functools.partial(_mlp_kernel, final_norm=final_gain is not None),
        grid=(rows // tm, dff // tf),
        in_specs=in_specs,
        out_specs=pl.BlockSpec((tm, d), lambda i, j: (i, 0)),
        out_shape=jax.ShapeDtypeStruct((rows, d), F32),
        scratch_shapes=[pltpu.VMEM((tm, d), BF16), pltpu.VMEM((tm, d), F32)],
        input_output_aliases={0: 0},
        compiler_params=_cparams("arbitrary", "arbitrary"),
        name="mlp",
    )(*args)


GLA_MATMUL_LEVEL_ROWS = 8


def _gla_tables(chunk, reverse):
    levels = int(np.log2(chunk))
    idx = np.arange(chunk)
    t, r = idx[:, None], idx[None, :]
    if not reverse:
        q_incl = (r <= t)
        k_rest = (r > t)
    else:
        q_incl = (r >= t)
        k_rest = (r < t)
    mats = [q_incl, k_rest]
    masks = [np.eye(chunk, dtype=bool)]
    for lv in range(1, levels + 1):
        m = 1 << (lv - 1)
        mid = (idx // (2 * m)) * (2 * m) + m
        upper = idx >= mid
        midc = mid[:, None]
        if not reverse:
            pat = np.where(upper[:, None], (r >= midc) & (r <= t), (r > t) & (r < midc))
            mask = (upper[:, None] & ~upper[None, :])
        else:
            pat = np.where(upper[:, None], (r >= midc) & (r < t), (r >= t) & (r < midc))
            mask = (~upper[:, None] & upper[None, :])
        mask = mask & ((idx[:, None] // (2 * m)) == (idx[None, :] // (2 * m)))
        if 2 * m <= GLA_MATMUL_LEVEL_ROWS:
            mats.append(pat)
        masks.append(mask)
    a = np.concatenate(mats, axis=0).astype(np.float32)
    a2 = np.concatenate([a, a], axis=1)
    return jnp.asarray(a2, BF16), jnp.asarray(np.stack(masks).astype(np.float32), F32)


def _gla_kernel(qkvr_ref, gf_ref, wup_ref, bup_ref, a_ref, mask_ref, s0_ref, o_ref, sfin_ref,
                s_sc, z_sc, *, reverse):
    c = pl.program_id(1)
    chunk = qkvr_ref.shape[0]
    levels = mask_ref.shape[0] - 1
    dq = GLA_HEADS * GLA_DK

    @pl.when(c == 0)
    def _():
        s_sc[...] = s0_ref[...]

    graw = _dot(gf_ref[...], wup_ref[...]) + bup_ref[...]
    g = (jnp.minimum(graw, 0.0) - jnp.log1p(jnp.exp(-jnp.abs(graw)))) * (1.0 / GLA_GATE_TAU)
    g_hi = g.astype(BF16)
    g_cat = jnp.concatenate([g_hi, (g - g_hi.astype(F32)).astype(BF16)], axis=0)

    def partial_sums(block):
        return _dot(a_ref[block * chunk:(block + 1) * chunk, :], g_cat)

    bq = partial_sums(0)
    bk = partial_sums(1)
    q_all = qkvr_ref[:, 0:dq].astype(F32) * (GLA_DK ** -0.5)
    k_all = qkvr_ref[:, dq:2 * dq].astype(F32)

    row = lax.broadcasted_iota(jnp.int32, (chunk, 1), 0)
    for lv in range(1, levels + 1):
        m = 1 << (lv - 1)
        if 2 * m <= GLA_MATMUL_LEVEL_ROWS:
            e = partial_sums(1 + lv)
        else:
            pieces = []
            for lo in range(0, chunk, 2 * m):
                mid, hi = lo + m, lo + 2 * m
                if not reverse:
                    pieces.append(bk[lo:mid] - bk[mid - 1:mid])
                    pieces.append(bq[mid:hi] - bq[mid - 1:mid])
                else:
                    pieces.append(bq[lo:mid] - bq[mid:mid + 1])
                    pieces.append(bk[mid:hi] - bk[mid:mid + 1])
            e = jnp.concatenate(pieces, axis=0)
        use_q = ((row // m) % 2) == (0 if reverse else 1)
        z_sc[lv - 1] = (jnp.where(use_q, q_all, k_all) * jnp.exp(e)).astype(BF16)

    btot = bq[0:1, :] if reverse else bq[chunk - 1:chunk, :]
    qd = (q_all * jnp.exp(bq)).astype(BF16)
    kd = (k_all * jnp.exp(bk)).astype(BF16)
    etot = jnp.exp(btot)
    for h in range(GLA_HEADS):
        sl = slice(h * GLA_DK, (h + 1) * GLA_DK)
        vh = qkvr_ref[:, 2 * dq + h * GLA_DV:2 * dq + (h + 1) * GLA_DV]
        scores = mask_ref[0] * jnp.sum(q_all[:, sl] * k_all[:, sl], axis=-1, keepdims=True)
        for lv in range(1, levels + 1):
            zh = z_sc[lv - 1, :, sl]
            scores = scores + mask_ref[lv] * _dot_nt(zh, zh)
        s_old = s_sc[h]
        o_ref[:, h * GLA_DV:(h + 1) * GLA_DV] = (_dot(scores.astype(BF16), vh)
                                                + _dot(qd[:, sl], s_old.astype(BF16)))
        ecol = jnp.transpose(jnp.broadcast_to(etot[:, sl], (GLA_DK, GLA_DK)))
        s_sc[h] = s_old * jnp.concatenate([ecol] * (GLA_DV // GLA_DK), axis=1) + _dot_tn(kd[:, sl], vh)

    @pl.when(c == pl.num_programs(1) - 1)
    def _():
        sfin_ref[...] = s_sc[...]


def _gla_scan(qkvr, gfeat, wup, bup, s0, bsz, chunk, reverse):
    rows = qkvr.shape[0]
    nc = rows // bsz // chunk
    a_tab, masks = _gla_tables(chunk, reverse)
    dv = GLA_HEADS * GLA_DV

    def row_block(b, c):
        return (b * nc + (nc - 1 - c if reverse else c), 0)

    state_spec = pl.BlockSpec((None, GLA_HEADS, GLA_DK, GLA_DV), lambda b, c: (b, 0, 0, 0))
    return pl.pallas_call(
        functools.partial(_gla_kernel, reverse=reverse),
        grid=(bsz, nc),
        in_specs=[pl.BlockSpec((chunk, qkvr.shape[1]), row_block),
                  pl.BlockSpec((chunk, gfeat.shape[1]), row_block),
                  pl.BlockSpec(wup.shape, lambda b, c: (0, 0)),
                  pl.BlockSpec(bup.shape, lambda b, c: (0, 0)),
                  pl.BlockSpec(a_tab.shape, lambda b, c: (0, 0)),
                  pl.BlockSpec(masks.shape, lambda b, c: (0, 0, 0)),
                  state_spec],
        out_specs=[pl.BlockSpec((chunk, dv), row_block), state_spec],
        out_shape=[jax.ShapeDtypeStruct((rows, dv), F32),
                   jax.ShapeDtypeStruct((bsz, GLA_HEADS, GLA_DK, GLA_DV), F32)],
        scratch_shapes=[pltpu.VMEM((GLA_HEADS, GLA_DK, GLA_DV), F32),
                        pltpu.VMEM((masks.shape[0] - 1, chunk, GLA_HEADS * GLA_DK), BF16)],
        compiler_params=_cparams("arbitrary", "arbitrary"),
        name="gla_scan_bwd" if reverse else "gla_scan_fwd",
    )(qkvr, gfeat, wup, bup, a_tab, masks, s0)


LRU_HALO = 8


def _lru_kernel(x_ref, prev_ref, next_ref, cw_ref, cb_ref, wcat_ref, ba_ref, bx_ref, lam_ref, h0_ref,
                hs_ref, hfin_ref, xs_sc, a_sc, u_sc, hs_sc, h_sc, *, reverse):
    i = pl.program_id(0)
    n = pl.num_programs(0)
    tidx = n - 1 - i if reverse else i
    bsz, tt, width = x_ref.shape
    pitch = a_sc.shape[1] // bsz
    bd = RNN_BLOCK_DIM

    @pl.when(i == 0)
    def _():
        h_sc[...] = h0_ref[...]

    xs_sc[:, 0:LRU_HALO, :] = prev_ref[...].astype(F32) * (tidx > 0).astype(F32)
    xs_sc[:, LRU_HALO:LRU_HALO + tt, :] = x_ref[...].astype(F32)
    xs_sc[:, LRU_HALO + tt:2 * LRU_HALO + tt, :] = next_ref[...].astype(F32) * (tidx < n - 1).astype(F32)

    lam = lam_ref[...]
    sp = jnp.maximum(-lam, 0.0) + jnp.log1p(jnp.exp(-jnp.abs(lam)))
    for nb in range(RNN_BLOCKS):
        sl = slice(nb * bd, (nb + 1) * bd)
        xc = cb_ref[:, sl].reshape(1, 1, bd)
        for j in range(CONV_WIDTH):
            r0 = LRU_HALO - CONV_LEFT + j
            xc = xc + xs_sc[:, r0:r0 + tt, sl] * cw_ref[j:j + 1, sl].reshape(1, 1, bd)
        xc2 = xc.reshape(bsz * tt, bd)
        ri = _dot(xc2.astype(BF16), wcat_ref[nb])
        r = jax.nn.sigmoid(ri[:, :bd] + ba_ref[:, sl])
        ig = jax.nn.sigmoid(ri[:, bd:] + bx_ref[:, sl])
        log_a = (-LRU_C) * r * sp[:, sl]
        th = jnp.tanh(log_a)
        one_minus_a2 = (-2.0 * th) / (1.0 - th)
        a = jnp.exp(log_a)
        u = jnp.sqrt(one_minus_a2) * (ig * xc2)
        for b in range(bsz):
            a_sc[nb, b * pitch:b * pitch + tt, :] = a[b * tt:(b + 1) * tt]
            u_sc[nb, b * pitch:b * pitch + tt, :] = u[b * tt:(b + 1) * tt]

    def step(k, hs):
        t = tt - 1 - k if reverse else k
        rows = pl.ds(t, bsz, stride=pitch)
        new = []
        for nb in range(RNN_BLOCKS):
            h = a_sc[nb, rows, :] * hs[nb] + u_sc[nb, rows, :]
            hs_sc[nb, rows, :] = h
            new.append(h)
        return tuple(new)

    h_init = tuple(h_sc[:, nb * bd:(nb + 1) * bd] for nb in range(RNN_BLOCKS))
    h_last = lax.fori_loop(0, tt, step, h_init, unroll=4)
    for nb in range(RNN_BLOCKS):
        sl = slice(nb * bd, (nb + 1) * bd)
        h_sc[:, sl] = h_last[nb]
        for b in range(bsz):
            hs_ref[b, :, sl] = hs_sc[nb, b * pitch:b * pitch + tt, :]

    @pl.when(i == n - 1)
    def _():
        hfin_ref[...] = h_sc[...]


def _lru_scan(p3, conv_w, conv_b, wcat, ba, bx, lam, h0, tt, reverse):
    bsz, seq, two_w = p3.shape
    width = two_w // 2
    n = seq // tt
    hb = tt // LRU_HALO
    pitch = tt + 8
    assert tt % 16 == 0 and width == RNN_BLOCKS * RNN_BLOCK_DIM

    def tix(i):
        return n - 1 - i if reverse else i

    vec = lambda: pl.BlockSpec((1, width), lambda i: (0, 0))
    slab = lambda: pltpu.VMEM((RNN_BLOCKS, bsz * pitch, RNN_BLOCK_DIM), F32)
    return pl.pallas_call(
        functools.partial(_lru_kernel, reverse=reverse),
        grid=(n,),
        in_specs=[pl.BlockSpec((bsz, tt, width), lambda i: (0, tix(i), 1)),
                  pl.BlockSpec((bsz, LRU_HALO, width), lambda i: (0, jnp.maximum(tix(i) * hb - 1, 0), 1)),
                  pl.BlockSpec((bsz, LRU_HALO, width), lambda i: (0, jnp.minimum((tix(i) + 1) * hb, seq // LRU_HALO - 1), 1)),
                  pl.BlockSpec((CONV_WIDTH, width), lambda i: (0, 0)), vec(),
                  pl.BlockSpec(wcat.shape, lambda i: (0, 0, 0)), vec(), vec(), vec(),
                  pl.BlockSpec((bsz, width), lambda i: (0, 0))],
        out_specs=[pl.BlockSpec((bsz, tt, width), lambda i: (0, tix(i), 0)),
                   pl.BlockSpec((bsz, width), lambda i: (0, 0))],
        out_shape=[jax.ShapeDtypeStruct((bsz, seq, width), F32),
                   jax.ShapeDtypeStruct((bsz, width), F32)],
        scratch_shapes=[pltpu.VMEM((bsz, tt + 2 * LRU_HALO, width), F32), slab(), slab(), slab(),
                        pltpu.VMEM((bsz, width), F32)],
        compiler_params=_cparams("arbitrary"),
        name="lru_scan_bwd" if reverse else "lru_scan_fwd",
    )(p3, p3, p3, conv_w, conv_b.reshape(1, width), wcat, ba.reshape(1, width),
      bx.reshape(1, width), lam.reshape(1, width), h0)


def _rope(x, cos, sin):
    return x * cos + pltpu.roll(x, HEAD_DIM // 2, axis=1) * sin


KV_OUT_WIDTH = 3 * KV_HEADS * HEAD_DIM
PV_SPLITS = 4


def _attn_in_kernel(x_ref, mod_ref, g_ref, wq_ref, wkv_ref, qg_ref, kg_ref, *rest, rope):
    if rope:
        cos_ref, sin_ref, q_ref, kv_ref = rest
    else:
        q_ref, kv_ref = rest
    h = _adaln(x_ref[...], g_ref[...], mod_ref[0:1, :], mod_ref[1:2, :]).astype(BF16)

    def head(xh, gain, scale):
        xh = _rms(xh, gain)
        if rope:
            xh = _rope(xh, cos_ref[...], sin_ref[...])
        return (xh * scale).astype(BF16)

    group_w = GROUP * HEAD_DIM
    for n0 in range(0, Q_HEADS * HEAD_DIM, group_w):
        res = _dot(h, wq_ref[:, n0:n0 + group_w])
        for j in range(GROUP):
            sl = slice(j * HEAD_DIM, (j + 1) * HEAD_DIM)
            q_ref[:, n0 + j * HEAD_DIM:n0 + (j + 1) * HEAD_DIM] = head(res[:, sl], qg_ref[...], HEAD_DIM ** -0.5)
    kv = _dot(h, wkv_ref[...])
    kw = KV_HEADS * HEAD_DIM
    ones = jnp.ones((kv.shape[0], HEAD_DIM), BF16)
    for j in range(KV_HEADS):
        sl = slice(j * HEAD_DIM, (j + 1) * HEAD_DIM)
        kv_ref[:, sl] = head(kv[:, sl], kg_ref[...], 1.0)
        v0 = kw + 2 * j * HEAD_DIM
        kv_ref[:, v0:v0 + HEAD_DIM] = kv[:, kw + j * HEAD_DIM:kw + (j + 1) * HEAD_DIM].astype(BF16)
        kv_ref[:, v0 + HEAD_DIM:v0 + 2 * HEAD_DIM] = ones


def _attn_in(x2d, mod, layer, mod_index, gain, w_q, w_kv, q_g, k_g, tm, rope_tables, seq, name):
    rows, d = x2d.shape
    vec = lambda: pl.BlockSpec((1, HEAD_DIM), lambda i: (0, 0))
    in_specs = [pl.BlockSpec((tm, d), lambda i: (i, 0)), _mod_spec(layer, d, mod_index),
                pl.BlockSpec((1, d), lambda i: (0, 0)),
                pl.BlockSpec(w_q.shape, lambda i: (0, 0)), pl.BlockSpec(w_kv.shape, lambda i: (0, 0)), vec(), vec()]
    args = [x2d, mod, gain.reshape(1, d), w_q, w_kv, q_g.reshape(1, HEAD_DIM), k_g.reshape(1, HEAD_DIM)]
    if rope_tables is not None:
        tpb = seq // tm
        in_specs += [pl.BlockSpec((tm, HEAD_DIM), lambda i: (i % tpb, 0))] * 2
        args += list(rope_tables)
    return pl.pallas_call(
        functools.partial(_attn_in_kernel, rope=rope_tables is not None),
        grid=(rows // tm,),
        in_specs=in_specs,
        out_specs=[pl.BlockSpec((tm, w_q.shape[1]), lambda i: (i, 0)),
                   pl.BlockSpec((tm, KV_OUT_WIDTH), lambda i: (i, 0))],
        out_shape=[jax.ShapeDtypeStruct((rows, w_q.shape[1]), BF16),
                   jax.ShapeDtypeStruct((rows, KV_OUT_WIDTH), BF16)],
        compiler_params=_cparams("arbitrary"),
        name=name,
    )(*args)


def _attn_kernel(q_ref, kvc_ref, *rest, has_latent, tk):
    if has_latent:
        kvl_ref, o_ref, s_sc, p_sc = rest
    else:
        o_ref, s_sc, p_sc = rest
    tq = q_ref.shape[0]
    rows = GROUP * tq
    kw = KV_HEADS * HEAD_DIM
    lanes = HEAD_DIM
    n_ctx = kvc_ref.shape[0]
    n_lat = kvl_ref.shape[0] if has_latent else 0

    def lane_groups(s):
        return [s[:, c * lanes:(c + 1) * lanes] for c in range(s.shape[1] // lanes)]

    tkc = min(tk, n_ctx)
    tiles = [(kvc_ref, r0, tkc, r0) for r0 in range(0, n_ctx, tkc)]
    tiles += [(kvl_ref, r0, tk, n_ctx + r0) for r0 in range(0, n_lat, tk)]
    pv_parts = [(kvc_ref, 0, n_ctx, 0)]
    pv_parts += [(kvl_ref, r0, n_lat // PV_SPLITS, n_ctx + r0) for r0 in range(0, n_lat, max(n_lat // PV_SPLITS, 1))]

    for hk in range(KV_HEADS):
        ksl = slice(hk * HEAD_DIM, (hk + 1) * HEAD_DIM)
        vsl = slice(kw + 2 * hk * HEAD_DIM, kw + 2 * (hk + 1) * HEAD_DIM)
        qcat = jnp.concatenate([q_ref[:, (hk * GROUP + gi) * HEAD_DIM:(hk * GROUP + gi + 1) * HEAD_DIM]
                                for gi in range(GROUP)], axis=0)

        mrun = jnp.full((rows, lanes), -jnp.inf, F32)
        for kv_ref, r0, w, c0 in tiles:
            s = _dot_nt(qcat, kv_ref[r0:r0 + w, ksl])
            s_sc[:, c0:c0 + w] = s
            mrun = functools.reduce(jnp.maximum, [mrun] + lane_groups(s))
        m = jnp.broadcast_to(jnp.max(mrun, axis=-1, keepdims=True), (rows, lanes))

        for _, _, w, c0 in tiles:
            p = [jnp.exp(sg - m) for sg in lane_groups(s_sc[:, c0:c0 + w])]
            p_sc[:, c0:c0 + w] = jnp.concatenate(p, axis=1).astype(BF16)
        acc = functools.reduce(jnp.add, [_dot(p_sc[:, c0:c0 + w], kv_ref[r0:r0 + w, vsl])
                                         for kv_ref, r0, w, c0 in pv_parts])
        o = acc[:, 0:HEAD_DIM] / acc[:, HEAD_DIM:HEAD_DIM + 1]
        for gi in range(GROUP):
            hd = hk * GROUP + gi
            o_ref[:, hd * HEAD_DIM:(hd + 1) * HEAD_DIM] = o[gi * tq:(gi + 1) * tq].astype(o_ref.dtype)


def _attention(q, kv_ctx, kv_lat, bsz, tq, tk):
    rows = q.shape[0]
    nq = rows // bsz // tq
    n_ctx = kv_ctx.shape[0] // bsz
    has_latent = kv_lat is not None
    n_keys = n_ctx + (kv_lat.shape[0] // bsz if has_latent else 0)
    assert n_ctx % min(tk, n_ctx) == 0 and (n_keys - n_ctx) % (tk * PV_SPLITS) == 0
    in_specs = [pl.BlockSpec((tq, q.shape[1]), lambda b, i: (b * nq + i, 0)),
                pl.BlockSpec((n_ctx, kv_ctx.shape[1]), lambda b, i: (b, 0))]
    args = [q, kv_ctx]
    if has_latent:
        in_specs.append(pl.BlockSpec((n_keys - n_ctx, kv_lat.shape[1]), lambda b, i: (b, 0)))
        args.append(kv_lat)
    return pl.pallas_call(
        functools.partial(_attn_kernel, has_latent=has_latent, tk=tk),
        grid=(bsz, nq),
        in_specs=in_specs,
        out_specs=pl.BlockSpec((tq, q.shape[1]), lambda b, i: (b * nq + i, 0)),
        out_shape=jax.ShapeDtypeStruct(q.shape, BF16),
        scratch_shapes=[pltpu.VMEM((GROUP * tq, n_keys), F32), pltpu.VMEM((GROUP * tq, n_keys), BF16)],
        compiler_params=_cparams("arbitrary", "arbitrary"),
        name="attention" if has_latent else "attention_ctx",
    )(*args)


def _rope_tables(seq):
    n_rows = seq // GRID_W
    row = jnp.repeat(jnp.arange(n_rows), GRID_W)
    col = jnp.tile(jnp.arange(GRID_W), n_rows)
    n_pairs_axis = HEAD_DIM // 4
    inv_freq = ROPE_THETA ** (-jnp.arange(n_pairs_axis, dtype=F32) / n_pairs_axis)
    ang = jnp.concatenate([row[:, None] * inv_freq, col[:, None] * inv_freq], axis=-1)
    cos, sin = jnp.cos(ang), jnp.sin(ang)
    return jnp.concatenate([cos, cos], axis=-1), jnp.concatenate([-sin, sin], axis=-1)


def _tile(n, pref):
    t = min(n, pref)
    assert n % t == 0, (n, pref)
    return t


def kernel(x, c, ctx, c_ctx, norm_mix_g, norm_mlp_g, w_mod, b_mod, w_mlp1, w_mlp2, gla_w_in, gla_w_up_f, gla_b_f, gla_w_up_b, gla_b_b, gla_norm_g, gla_w_o, lru_w_in, lru_conv_w, lru_conv_b, lru_wa_f, lru_ba_f, lru_wx_f, lru_bx_f, lru_lam_f, lru_wa_b, lru_ba_b, lru_wx_b, lru_bx_b, lru_lam_b, lru_w_o, attn_w_in, attn_q_g, attn_k_g, attn_w_o, final_g):
    bsz, seq, d = x.shape
    n_ctx = ctx.shape[1]
    depth = w_mod.shape[0]
    assert bsz + 1 <= MOD_ROWS and d == GLA_HEADS * GLA_DV == Q_HEADS * HEAD_DIM

    mod = _modulation(c, c_ctx, w_mod, b_mod)
    xl = x.reshape(bsz * seq, d)
    xc = ctx.reshape(bsz * n_ctx, d)

    tm_l = _tile(seq, 512)
    tm_c = _tile(n_ctx, 256)
    lat_idx = lambda i: 1 + (i * tm_l) // seq
    ctx_idx = lambda i: 0

    for layer in range(depth):
        need_ctx = layer < depth - 1
        kind, j = layer % N_MIXERS, layer // N_MIXERS
        gmix = norm_mix_g[layer]

        if kind == 0:
            dq = GLA_HEADS * GLA_DK
            main_w = gla_w_in[j][:, :2 * dq + 2 * d].astype(BF16)
            gate_w = jnp.pad(gla_w_in[j][:, 2 * dq + 2 * d:], ((0, 0), (0, GLA_GATE_PAD - 2 * GLA_GATE_RANK))).astype(BF16)
            zpad = jnp.zeros((GLA_GATE_PAD - 2 * GLA_GATE_RANK, dq), F32)
            wup_f = jnp.concatenate([gla_w_up_f[j], jnp.zeros_like(gla_w_up_b[j]), zpad], axis=0).astype(BF16)
            wup_b = jnp.concatenate([jnp.zeros_like(gla_w_up_f[j]), gla_w_up_b[j], zpad], axis=0).astype(BF16)
            bup_f, bup_b = gla_b_f[j].reshape(1, dq), gla_b_b[j].reshape(1, dq)
            w_o = gla_w_o[j].astype(BF16)
            proj = {}
            for name, (xs, midx, tm, slen) in zip(("lat", "ctx"), [(xl, lat_idx, tm_l, seq), (xc, ctx_idx, tm_c, n_ctx)]):
                proj[name] = _ln_matmul(xs, mod, layer, midx, gmix, [main_w, gate_w], [BF16, BF16], tm,
                                        name="gla_in_" + name)
            s0 = jnp.zeros((bsz, GLA_HEADS, GLA_DK, GLA_DV), F32)
            ch_c, ch_l = _tile(n_ctx, 128), _tile(seq, 128)
            ocf, s_cf = _gla_scan(*proj["ctx"], wup_f, bup_f, s0, bsz, ch_c, False)
            ocb, s_cb = _gla_scan(*proj["ctx"], wup_b, bup_b, s0, bsz, ch_c, True)
            olf, _ = _gla_scan(*proj["lat"], wup_f, bup_f, s_cf, bsz, ch_l, False)
            olb, _ = _gla_scan(*proj["lat"], wup_b, bup_b, s_cb, bsz, ch_l, True)
            ng = gla_norm_g[j].reshape(1, GLA_DV)

            def gla_out(of, ob, qkvr, xs, midx, tm):
                blk = lambda i: (i, 0)
                return _proj_res("gla", [of, ob, qkvr], [pl.BlockSpec((tm, d), blk), pl.BlockSpec((tm, d), blk),
                                                        pl.BlockSpec((tm, d), lambda i: (i, (2 * dq + d) // d))],
                                 w_o, xs, mod, layer, midx, tm, extra=[ng],
                                 extra_specs=[pl.BlockSpec((1, GLA_DV), lambda i: (0, 0))])

            xl = gla_out(olf, olb, proj["lat"][0], xl, lat_idx, tm_l)
            if need_ctx:
                xc = gla_out(ocf, ocb, proj["ctx"][0], xc, ctx_idx, tm_c)

        elif kind == 1:
            w_in = lru_w_in[j].astype(BF16)
            w_o = lru_w_o[j].astype(BF16)
            wcat_f = jnp.concatenate([lru_wa_f[j], lru_wx_f[j]], axis=-1).astype(BF16)
            wcat_b = jnp.concatenate([lru_wa_b[j], lru_wx_b[j]], axis=-1).astype(BF16)
            p_l, = _ln_matmul(xl, mod, layer, lat_idx, gmix, [w_in], [BF16], tm_l, name="lru_in_lat")
            p_c, = _ln_matmul(xc, mod, layer, ctx_idx, gmix, [w_in], [BF16], tm_c, name="lru_in_ctx")
            h0 = jnp.zeros((bsz, d), F32)
            fwd = (lru_conv_w[j], lru_conv_b[j], wcat_f, lru_ba_f[j], lru_bx_f[j], lru_lam_f[j])
            bwd = (lru_conv_w[j], lru_conv_b[j], wcat_b, lru_ba_b[j], lru_bx_b[j], lru_lam_b[j])
            tt_c, tt_l = _tile(n_ctx, 128), _tile(seq, 128)
            p_c3, p_l3 = p_c.reshape(bsz, n_ctx, 2 * d), p_l.reshape(bsz, seq, 2 * d)
            hcf, s_cf = _lru_scan(p_c3, *fwd, h0, tt_c, False)
            hcb, s_cb = _lru_scan(p_c3, *bwd, h0, tt_c, True)
            hlf, _ = _lru_scan(p_l3, *fwd, s_cf, tt_l, False)
            hlb, _ = _lru_scan(p_l3, *bwd, s_cb, tt_l, True)

            def lru_out(hf, hb, p2d, xs, midx, tm):
                blk = lambda i: (i, 0)
                spec = lambda: pl.BlockSpec((tm, d), blk)
                return _proj_res("lru", [hf.reshape(-1, d), hb.reshape(-1, d), p2d], [spec(), spec(), spec()],
                                 w_o, xs, mod, layer, midx, tm)

            xl = lru_out(hlf, hlb, p_l, xl, lat_idx, tm_l)
            if need_ctx:
                xc = lru_out(hcf, hcb, p_c, xc, ctx_idx, tm_c)

        else:
            qw = Q_HEADS * HEAD_DIM
            w_q = attn_w_in[j][:, :qw].astype(BF16)
            w_kv = attn_w_in[j][:, qw:].astype(BF16)
            w_o = attn_w_o[j].astype(BF16)
            qg, kg = attn_q_g[j], attn_k_g[j]
            q_l, kv_l = _attn_in(xl, mod, layer, lat_idx, gmix, w_q, w_kv, qg, kg, tm_l, _rope_tables(seq), seq,
                                 "attn_in_lat")
            q_c, kv_c = _attn_in(xc, mod, layer, ctx_idx, gmix, w_q, w_kv, qg, kg, tm_c, None, n_ctx, "attn_in_ctx")
            tk = _tile(seq // PV_SPLITS, 512)
            o_l = _attention(q_l, kv_c, kv_l, bsz, _tile(seq, 128), tk)
            blk = lambda i: (i, 0)
            xl = _proj_res("plain", [o_l], [pl.BlockSpec((tm_l, d), blk)], w_o, xl, mod, layer, lat_idx, tm_l)
            if need_ctx:
                o_c = _attention(q_c, kv_c, None, bsz, _tile(n_ctx, 128), tk)
                xc = _proj_res("plain", [o_c], [pl.BlockSpec((tm_c, d), blk)], w_o, xc, mod, layer, ctx_idx, tm_c)

        w1, w2 = w_mlp1[layer].astype(BF16), w_mlp2[layer].astype(BF16)
        tf = _tile(w1.shape[1], 1024)
        tmm_l = _tile(seq, 1024)
        xl = _mlp(xl, mod, layer, lambda i: 1 + (i * tmm_l) // seq, norm_mlp_g[layer], w1, w2, tmm_l, tf,
                  final_gain=None if need_ctx else final_g)
        if need_ctx:
            xc = _mlp(xc, mod, layer, ctx_idx, norm_mlp_g[layer], w1, w2, _tile(bsz * n_ctx, 1024), tf)

    return xl.reshape(bsz, seq, d)
```

```python
import functools

import numpy as np
import jax
import jax.numpy as jnp
from jax import lax
from jax.experimental import pallas as pl
from jax.experimental.pallas import tpu as pltpu

F32 = jnp.float32
BF16 = jnp.bfloat16

EPS = 1e-6
N_MOD = 6
N_MIXERS = 3
MOD_ROWS = 16

GLA_HEADS = 4
GLA_DK = 128
GLA_DV = 256
GLA_GATE_RANK = 16
GLA_GATE_TAU = 16.0
GLA_GATE_PAD = 128

RNN_BLOCKS = 8
RNN_BLOCK_DIM = 128
CONV_WIDTH = 4
CONV_LEFT = 2
LRU_C = 8.0

HEAD_DIM = 128
Q_HEADS = 8
KV_HEADS = 2
GROUP = Q_HEADS // KV_HEADS
GRID_W = 64
ROPE_THETA = 10000.0

VMEM_LIMIT_BYTES = 56 * 1024 * 1024


def _cparams(*sem):
    return pltpu.CompilerParams(dimension_semantics=sem, vmem_limit_bytes=VMEM_LIMIT_BYTES)


def _dot(a, b):
    return jnp.dot(a, b, preferred_element_type=F32)


def _dot_nt(a, b):
    return lax.dot_general(a, b, (((1,), (1,)), ((), ())), preferred_element_type=F32)


def _dot_tn(a, b):
    return lax.dot_general(a, b, (((0,), (0,)), ((), ())), preferred_element_type=F32)


def _rms(x, g):
    return x * lax.rsqrt(jnp.mean(x * x, axis=-1, keepdims=True) + EPS) * g


def _adaln(x, g, shift, scale):
    return _rms(x, g) * (1.0 + scale) + shift


def _mod_kernel(s_ref, w_ref, b_ref, o_ref):
    s = s_ref[...]
    s = s * jax.nn.sigmoid(s)
    o_ref[...] = _dot(s.astype(BF16), w_ref[...].astype(BF16)) + b_ref[...]


def _modulation(c, c_ctx, w_mod, b_mod):
    depth, d, nd = w_mod.shape
    bsz = c.shape[0]
    rows = jnp.concatenate([c_ctx[None, :], c, jnp.zeros((MOD_ROWS - 1 - bsz, d), F32)], axis=0)
    tn = min(nd, 1536)
    out = pl.pallas_call(
        _mod_kernel,
        grid=(depth, nd // tn),
        in_specs=[pl.BlockSpec((MOD_ROWS, d), lambda l, j: (0, 0)),
                  pl.BlockSpec((None, d, tn), lambda l, j: (l, 0, j)),
                  pl.BlockSpec((None, 1, tn), lambda l, j: (l, 0, j))],
        out_specs=pl.BlockSpec((None, MOD_ROWS, tn), lambda l, j: (l, 0, j)),
        out_shape=jax.ShapeDtypeStruct((depth, MOD_ROWS, nd), F32),
        compiler_params=_cparams("arbitrary", "arbitrary"),
        name="modulation",
    )(rows, w_mod, b_mod.reshape(depth, 1, nd))
    return out.reshape(depth, MOD_ROWS, N_MOD, d)


def _mod_spec(layer, d, mod_index):
    return pl.BlockSpec((None, None, N_MOD, d), lambda i, *_: (layer, mod_index(i), 0, 0))


def _ln_matmul_kernel(x_ref, mod_ref, g_ref, *refs, n_out, mod_off, col_chunk):
    w_refs, o_refs = refs[:n_out], refs[n_out:]
    h = _adaln(x_ref[...], g_ref[...], mod_ref[mod_off:mod_off + 1, :],
               mod_ref[mod_off + 1:mod_off + 2, :]).astype(BF16)
    for w_ref, o_ref in zip(w_refs, o_refs):
        n = w_ref.shape[1]
        for n0 in range(0, n, col_chunk):
            n1 = min(n, n0 + col_chunk)
            o_ref[:, n0:n1] = _dot(h, w_ref[:, n0:n1]).astype(o_ref.dtype)


def _ln_matmul(x2d, mod, layer, mod_index, gain, ws, out_dtypes, tm, name="ln_matmul"):
    rows, d = x2d.shape
    in_specs = [pl.BlockSpec((tm, d), lambda i: (i, 0)), _mod_spec(layer, d, mod_index),
                pl.BlockSpec((1, d), lambda i: (0, 0))]
    in_specs += [pl.BlockSpec(w.shape, lambda i: (0, 0)) for w in ws]
    return pl.pallas_call(
        functools.partial(_ln_matmul_kernel, n_out=len(ws), mod_off=0, col_chunk=512),
        grid=(rows // tm,),
        in_specs=in_specs,
        out_specs=[pl.BlockSpec((tm, w.shape[1]), lambda i: (i, 0)) for w in ws],
        out_shape=[jax.ShapeDtypeStruct((rows, w.shape[1]), dt) for w, dt in zip(ws, out_dtypes)],
        compiler_params=_cparams("arbitrary"),
        name=name,
    )(x2d, mod, gain.reshape(1, d), *ws)


def _proj_res_kernel(y0_ref, y1_ref, w_ref, x_ref, mod_ref, o_ref):
    y = jnp.concatenate([y0_ref[...], y1_ref[...]], axis=1)
    o_ref[...] = x_ref[...] + mod_ref[2:3, :] * _dot(y, w_ref[...])


def _gla_proj_res_kernel(of_ref, ob_ref, r_ref, ng_ref, w_ref, x_ref, mod_ref, o_ref):
    o = of_ref[...] + ob_ref[...]
    r = r_ref[...].astype(F32)
    ys = []
    for h in range(GLA_HEADS):
        sl = slice(h * GLA_DV, (h + 1) * GLA_DV)
        rh = r[:, sl]
        ys.append(_rms(o[:, sl], ng_ref[...]) * (rh * jax.nn.sigmoid(rh)))
    y = jnp.concatenate(ys, axis=1).astype(BF16)
    o_ref[...] = x_ref[...] + mod_ref[2:3, :] * _dot(y, w_ref[...])


def _lru_proj_res_kernel(hf_ref, hb_ref, gate_ref, w_ref, x_ref, mod_ref, o_ref):
    y = ((hf_ref[...] + hb_ref[...]) * jax.nn.gelu(gate_ref[...].astype(F32))).astype(BF16)
    o_ref[...] = x_ref[...] + mod_ref[2:3, :] * _dot(y, w_ref[...])


def _proj_res(kind, operands, operand_specs, w, x2d, mod, layer, mod_index, tm, extra=(), extra_specs=()):
    rows, d = x2d.shape
    body = {"plain": _proj_res_kernel, "gla": _gla_proj_res_kernel, "lru": _lru_proj_res_kernel}[kind]
    n_in = len(operands) + len(extra) + 1
    return pl.pallas_call(
        body,
        grid=(rows // tm,),
        in_specs=[*operand_specs, *extra_specs, pl.BlockSpec(w.shape, lambda i: (0, 0)),
                  pl.BlockSpec((tm, d), lambda i: (i, 0)), _mod_spec(layer, d, mod_index)],
        out_specs=pl.BlockSpec((tm, d), lambda i: (i, 0)),
        out_shape=jax.ShapeDtypeStruct((rows, d), F32),
        input_output_aliases={n_in: 0},
        compiler_params=_cparams("arbitrary"),
        name=kind + "_proj_res",
    )(*operands, *extra, w, x2d, mod)


def _mlp_kernel(x_ref, mod_ref, g_ref, w1_ref, w2_ref, *rest, final_norm):
    if final_norm:
        fg_ref, o_ref, h_sc, acc_sc = rest
    else:
        o_ref, h_sc, acc_sc = rest
    j = pl.program_id(1)

    @pl.when(j == 0)
    def _():
        h_sc[...] = _adaln(x_ref[...], g_ref[...], mod_ref[3:4, :], mod_ref[4:5, :]).astype(BF16)
        acc_sc[...] = jnp.zeros_like(acc_sc)

    a = jnp.maximum(_dot(h_sc[...], w1_ref[...]), 0.0)
    acc_sc[...] += _dot((a * a).astype(BF16), w2_ref[...])

    @pl.when(j == pl.num_programs(1) - 1)
    def _():
        y = x_ref[...] + mod_ref[5:6, :] * acc_sc[...]
        o_ref[...] = _rms(y, fg_ref[...]) if final_norm else y


def _mlp(x2d, mod, layer, mod_index, gain, w1, w2, tm, tf, final_gain=None):
    rows, d = x2d.shape
    dff = w1.shape[1]
    vec = lambda: pl.BlockSpec((1, d), lambda i, j: (0, 0))
    in_specs = [pl.BlockSpec((tm, d), lambda i, j: (i, 0)), _mod_spec(layer, d, mod_index), vec(),
                pl.BlockSpec((d, tf), lambda i, j: (0, j)),
                pl.BlockSpec((tf, d), lambda i, j: (j, 0))]
    args = [x2d, mod, gain.reshape(1, d), w1, w2]
    if final_gain is not None:
        in_specs.append(vec())
        args.append(final_gain.reshape(1, d))
    return pl.pallas_call(
        functools.partial(_mlp_kernel, final_norm=final_gain is not None),
        grid=(rows // tm, dff // tf),
        in_specs=in_specs,
        out_specs=pl.BlockSpec((tm, d), lambda i, j: (i, 0)),
        out_shape=jax.ShapeDtypeStruct((rows, d), F32),
        scratch_shapes=[pltpu.VMEM((tm, d), BF16), pltpu.VMEM((tm, d), F32)],
        input_output_aliases={0: 0},
        compiler_params=_cparams("arbitrary", "arbitrary"),
        name="mlp",
    )(*args)


GLA_MATMUL_LEVEL_ROWS = 8


def _gla_tables(chunk, reverse):
    levels = int(np.log2(chunk))
    idx = np.arange(chunk)
    t, r = idx[:, None], idx[None, :]
    if not reverse:
        q_incl = (r <= t)
        k_rest = (r > t)
    else:
        q_incl = (r >= t)
        k_rest = (r < t)
    mats = [q_incl, k_rest]
    masks = [np.eye(chunk, dtype=bool)]
    for lv in range(1, levels + 1):
        m = 1 << (lv - 1)
        mid = (idx // (2 * m)) * (2 * m) + m
        upper = idx >= mid
        midc = mid[:, None]
        if not reverse:
            pat = np.where(upper[:, None], (r >= midc) & (r <= t), (r > t) & (r < midc))
            mask = (upper[:, None] & ~upper[None, :])
        else:
            pat = np.where(upper[:, None], (r >= midc) & (r < t), (r >= t) & (r < midc))
            mask = (~upper[:, None] & upper[None, :])
        mask = mask & ((idx[:, None] // (2 * m)) == (idx[None, :] // (2 * m)))
        if 2 * m <= GLA_MATMUL_LEVEL_ROWS:
            mats.append(pat)
        masks.append(mask)
    a = np.concatenate(mats, axis=0).astype(np.float32)
    a2 = np.concatenate([a, a], axis=1)
    return jnp.asarray(a2, BF16), jnp.asarray(np.stack(masks).astype(np.float32), F32)


def _gla_gates(qkvr_ref, gf_ref, wup_ref, bup_ref, a_ref, reverse):
    chunk = qkvr_ref.shape[0]
    dq = GLA_HEADS * GLA_DK
    graw = _dot(gf_ref[...], wup_ref[...]) + bup_ref[...]
    g = (jnp.minimum(graw, 0.0) - jnp.log(1.0 + jnp.exp(-jnp.abs(graw)))) * (1.0 / GLA_GATE_TAU)
    g_hi = g.astype(BF16)
    g_cat = jnp.concatenate([g_hi, (g - g_hi.astype(F32)).astype(BF16)], axis=0)

    def partial_sums(block):
        return _dot(a_ref[block * chunk:(block + 1) * chunk, :], g_cat)

    return dict(partial_sums=partial_sums, bq=partial_sums(0), bk=partial_sums(1), reverse=reverse, chunk=chunk,
                q_all=qkvr_ref[:, 0:dq].astype(F32) * (GLA_DK ** -0.5), k_all=qkvr_ref[:, dq:2 * dq].astype(F32))


def _gla_levels(st, z_sc, lv_range):
    chunk, reverse, bq, bk = st["chunk"], st["reverse"], st["bq"], st["bk"]
    q_all, k_all = st["q_all"], st["k_all"]
    row = lax.broadcasted_iota(jnp.int32, (chunk, 1), 0)
    for lv in lv_range:
        m = 1 << (lv - 1)
        if 2 * m <= GLA_MATMUL_LEVEL_ROWS:
            use_q = ((row // m) % 2) == (0 if reverse else 1)
            z = jnp.where(use_q, q_all, k_all) * jnp.exp(st["partial_sums"](1 + lv))
        else:
            pieces = []
            for lo in range(0, chunk, 2 * m):
                mid, hi = lo + m, lo + 2 * m
                if not reverse:
                    pieces.append(k_all[lo:mid] * jnp.exp(bk[lo:mid] - bk[mid - 1:mid]))
                    pieces.append(q_all[mid:hi] * jnp.exp(bq[mid:hi] - bq[mid - 1:mid]))
                else:
                    pieces.append(q_all[lo:mid] * jnp.exp(bq[lo:mid] - bq[mid:mid + 1]))
                    pieces.append(k_all[mid:hi] * jnp.exp(bk[mid:hi] - bk[mid:mid + 1]))
            z = jnp.concatenate(pieces, axis=0)
        z_sc[lv - 1] = z.astype(BF16)


def _gla_carry_factors(st):
    chunk, reverse, bq = st["chunk"], st["reverse"], st["bq"]
    st["qd"] = (st["q_all"] * jnp.exp(bq)).astype(BF16)
    st["kd"] = (st["k_all"] * jnp.exp(st["bk"])).astype(BF16)
    st["etot"] = jnp.exp(bq[0:1, :] if reverse else bq[chunk - 1:chunk, :])


def _gla_head(st, h, qkvr_ref, mask_ref, z_sc, s_sc, o_ref, levels):
    dq = GLA_HEADS * GLA_DK
    sl = slice(h * GLA_DK, (h + 1) * GLA_DK)
    vh = qkvr_ref[:, 2 * dq + h * GLA_DV:2 * dq + (h + 1) * GLA_DV]
    chunk, reverse = st["chunk"], st["reverse"]
    sub = 8
    diag = mask_ref[0] * jnp.sum(st["q_all"][:, sl] * st["k_all"][:, sl], axis=-1, keepdims=True)
    tiles = [diag[r:r + sub] for r in range(0, chunk, sub)]
    for lv in range(1, levels + 1):
        m = 1 << (lv - 1)
        if 2 * m <= GLA_MATMUL_LEVEL_ROWS:
            q_rows = [(0, chunk)]
        else:
            q_rows = [(lo, lo + m) if reverse else (lo + m, lo + 2 * m) for lo in range(0, chunk, 2 * m)]
        zq = jnp.concatenate([z_sc[lv - 1, lo:hi, sl] for lo, hi in q_rows], axis=0)
        mq = jnp.concatenate([mask_ref[lv, lo:hi, :] for lo, hi in q_rows], axis=0)
        part = mq * _dot_nt(zq, z_sc[lv - 1, :, sl])
        src = 0
        for lo, hi in q_rows:
            for r in range(lo, hi, sub):
                tiles[r // sub] = tiles[r // sub] + part[src:src + sub]
                src += sub
    scores = jnp.concatenate(tiles, axis=0)
    s_old = s_sc[h]
    o_ref[:, h * GLA_DV:(h + 1) * GLA_DV] = (_dot(scores.astype(BF16), vh)
                                            + _dot(st["qd"][:, sl], s_old.astype(BF16)))
    ecol = jnp.transpose(jnp.broadcast_to(st["etot"][:, sl], (GLA_DK, GLA_DK)))
    s_sc[h] = s_old * jnp.concatenate([ecol] * (GLA_DV // GLA_DK), axis=1) + _dot_tn(st["kd"][:, sl], vh)


def _gla_kernel(qkvr_f, gf_f, qkvr_b, gf_b, wup_f, wup_b, bup_f, bup_b, a_f, a_b, mask_f, mask_b, s0_f, s0_b,
                o_f, o_b, sfin_f, sfin_b, s_sc_f, s_sc_b, z_sc_f, z_sc_b):
    c = pl.program_id(1)
    levels = mask_f.shape[0] - 1

    @pl.when(c == 0)
    def _():
        s_sc_f[...] = s0_f[...]
        s_sc_b[...] = s0_b[...]

    st_f = _gla_gates(qkvr_f, gf_f, wup_f, bup_f, a_f, False)
    st_b = _gla_gates(qkvr_b, gf_b, wup_b, bup_b, a_b, True)
    _gla_levels(st_f, z_sc_f, range(1, levels + 1))
    _gla_carry_factors(st_f)
    per_head = -(-levels // GLA_HEADS)
    for h in range(GLA_HEADS):
        _gla_head(st_f, h, qkvr_f, mask_f, z_sc_f, s_sc_f, o_f, levels)
        _gla_levels(st_b, z_sc_b, range(1 + h * per_head, min(levels, (h + 1) * per_head) + 1))
    _gla_carry_factors(st_b)
    for h in range(GLA_HEADS):
        _gla_head(st_b, h, qkvr_b, mask_b, z_sc_b, s_sc_b, o_b, levels)

    @pl.when(c == pl.num_programs(1) - 1)
    def _():
        sfin_f[...] = s_sc_f[...]
        sfin_b[...] = s_sc_b[...]


def _gla_scan(qkvr, gfeat, wup_f, wup_b, bup_f, bup_b, s0_f, s0_b, bsz, chunk):
    rows = qkvr.shape[0]
    nc = rows // bsz // chunk
    a_f, mask_f = _gla_tables(chunk, False)
    a_b, mask_b = _gla_tables(chunk, True)
    dv = GLA_HEADS * GLA_DV
    fwd_block = lambda b, c: (b * nc + c, 0)
    bwd_block = lambda b, c: (b * nc + nc - 1 - c, 0)
    const2 = lambda x: pl.BlockSpec(x.shape, lambda b, c: (0, 0))
    const3 = lambda x: pl.BlockSpec(x.shape, lambda b, c: (0, 0, 0))
    state_spec = pl.BlockSpec((None, GLA_HEADS, GLA_DK, GLA_DV), lambda b, c: (b, 0, 0, 0))
    state_shape = jax.ShapeDtypeStruct((bsz, GLA_HEADS, GLA_DK, GLA_DV), F32)
    state_sc = lambda: pltpu.VMEM((GLA_HEADS, GLA_DK, GLA_DV), F32)
    z_sc = lambda: pltpu.VMEM((mask_f.shape[0] - 1, chunk, GLA_HEADS * GLA_DK), BF16)
    return pl.pallas_call(
        _gla_kernel,
        grid=(bsz, nc),
        in_specs=[pl.BlockSpec((chunk, qkvr.shape[1]), fwd_block), pl.BlockSpec((chunk, gfeat.shape[1]), fwd_block),
                  pl.BlockSpec((chunk, qkvr.shape[1]), bwd_block), pl.BlockSpec((chunk, gfeat.shape[1]), bwd_block),
                  const2(wup_f), const2(wup_b), const2(bup_f), const2(bup_b), const2(a_f), const2(a_b),
                  const3(mask_f), const3(mask_b), state_spec, state_spec],
        out_specs=[pl.BlockSpec((chunk, dv), fwd_block), pl.BlockSpec((chunk, dv), bwd_block), state_spec, state_spec],
        out_shape=[jax.ShapeDtypeStruct((rows, dv), F32), jax.ShapeDtypeStruct((rows, dv), F32),
                   state_shape, state_shape],
        scratch_shapes=[state_sc(), state_sc(), z_sc(), z_sc()],
        compiler_params=_cparams("arbitrary", "arbitrary"),
        name="gla_scan",
    )(qkvr, gfeat, qkvr, gfeat, wup_f, wup_b, bup_f, bup_b, a_f, a_b, mask_f, mask_b, s0_f, s0_b)


LRU_HALO = 16
CONV_SHIFTS = tuple(j - CONV_LEFT for j in range(CONV_WIDTH) if j != CONV_LEFT)
F32_TINY = float(np.finfo(np.float32).tiny)


def _lru_shift_matrix(tt):
    sel = np.zeros((len(CONV_SHIFTS) * tt, tt + 2 * LRU_HALO), np.float32)
    for i, off in enumerate(CONV_SHIFTS):
        sel[i * tt + np.arange(tt), LRU_HALO + np.arange(tt) + off] = 1.0
    return jnp.asarray(sel, BF16)


def _lru_kernel(x_ref, prev_ref, next_ref, shift_ref, cw_ref, cb_ref, wcat_ref, ba_ref, bx_ref, lam_ref, h0_ref,
                hs_ref, hfin_ref, xs_sc, xc_sc, a_sc, u_sc, hs_sc, h_sc, *, reverse):
    i = pl.program_id(0)
    n = pl.num_programs(0)
    tidx = n - 1 - i if reverse else i
    bsz, tt, width = x_ref.shape
    pitch = a_sc.shape[1] // bsz
    bd = RNN_BLOCK_DIM

    @pl.when(i == 0)
    def _():
        h_sc[...] = h0_ref[...]

    xs_sc[:, 0:LRU_HALO, :] = jnp.where(tidx > 0, prev_ref[...], jnp.zeros_like(prev_ref))
    xs_sc[:, LRU_HALO:LRU_HALO + tt, :] = x_ref[...]
    xs_sc[:, LRU_HALO + tt:2 * LRU_HALO + tt, :] = jnp.where(tidx < n - 1, next_ref[...], jnp.zeros_like(next_ref))
    for b in range(bsz):
        shifted = _dot(shift_ref[...], xs_sc[b])
        xc = cb_ref[...] + cw_ref[CONV_LEFT:CONV_LEFT + 1, :] * x_ref[b].astype(F32)
        for k, off in enumerate(CONV_SHIFTS):
            j = off + CONV_LEFT
            xc = xc + cw_ref[j:j + 1, :] * shifted[k * tt:(k + 1) * tt]
        xc_sc[b] = xc

    lam = lam_ref[...]
    neg4sp = -0.5 * LRU_C * (jnp.maximum(-lam, 0.0) + jnp.log1p(jnp.exp(-jnp.abs(lam))))
    for nb in range(RNN_BLOCKS):
        sl = slice(nb * bd, (nb + 1) * bd)
        xc2 = xc_sc[:, :, sl].reshape(bsz * tt, bd)
        ri = _dot(xc2.astype(BF16), wcat_ref[nb])
        log_a = neg4sp[:, sl] + neg4sp[:, sl] * jnp.tanh(0.5 * (ri[:, :bd] + ba_ref[:, sl]))
        ig = 0.5 + 0.5 * jnp.tanh(0.5 * (ri[:, bd:] + bx_ref[:, sl]))
        th = jnp.tanh(log_a)
        one_minus_a2 = (-2.0 * th) / (1.0 - th)
        root = one_minus_a2 * lax.rsqrt(jnp.maximum(one_minus_a2, F32_TINY))
        a = jnp.exp(log_a)
        u = root * (ig * xc2)
        for b in range(bsz):
            a_sc[nb, b * pitch:b * pitch + tt, :] = a[b * tt:(b + 1) * tt]
            u_sc[nb, b * pitch:b * pitch + tt, :] = u[b * tt:(b + 1) * tt]

    def step(k, hs):
        t = tt - 1 - k if reverse else k
        rows = pl.ds(t, bsz, stride=pitch)
        new = []
        for nb in range(RNN_BLOCKS):
            h = a_sc[nb, rows, :] * hs[nb] + u_sc[nb, rows, :]
            hs_sc[nb, rows, :] = h
            new.append(h)
        return tuple(new)

    h_init = tuple(h_sc[:, nb * bd:(nb + 1) * bd] for nb in range(RNN_BLOCKS))
    h_last = lax.fori_loop(0, tt, step, h_init, unroll=4)
    for nb in range(RNN_BLOCKS):
        sl = slice(nb * bd, (nb + 1) * bd)
        h_sc[:, sl] = h_last[nb]
        for b in range(bsz):
            hs_ref[b, :, sl] = hs_sc[nb, b * pitch:b * pitch + tt, :]

    @pl.when(i == n - 1)
    def _():
        hfin_ref[...] = h_sc[...]


def _lru_scan(p3, conv_w, conv_b, wcat, ba, bx, lam, h0, tt, reverse):
    bsz, seq, two_w = p3.shape
    width = two_w // 2
    n = seq // tt
    hb = tt // LRU_HALO
    pitch = tt + 8
    assert tt % LRU_HALO == 0 and width == RNN_BLOCKS * RNN_BLOCK_DIM
    shift = _lru_shift_matrix(tt)

    def tix(i):
        return n - 1 - i if reverse else i

    vec = lambda: pl.BlockSpec((1, width), lambda i: (0, 0))
    slab = lambda: pltpu.VMEM((RNN_BLOCKS, bsz * pitch, RNN_BLOCK_DIM), F32)
    return pl.pallas_call(
        functools.partial(_lru_kernel, reverse=reverse),
        grid=(n,),
        in_specs=[pl.BlockSpec((bsz, tt, width), lambda i: (0, tix(i), 1)),
                  pl.BlockSpec((bsz, LRU_HALO, width), lambda i: (0, jnp.maximum(tix(i) * hb - 1, 0), 1)),
                  pl.BlockSpec((bsz, LRU_HALO, width), lambda i: (0, jnp.minimum((tix(i) + 1) * hb, seq // LRU_HALO - 1), 1)),
                  pl.BlockSpec(shift.shape, lambda i: (0, 0)),
                  pl.BlockSpec((CONV_WIDTH, width), lambda i: (0, 0)), vec(),
                  pl.BlockSpec(wcat.shape, lambda i: (0, 0, 0)), vec(), vec(), vec(),
                  pl.BlockSpec((bsz, width), lambda i: (0, 0))],
        out_specs=[pl.BlockSpec((bsz, tt, width), lambda i: (0, tix(i), 0)),
                   pl.BlockSpec((bsz, width), lambda i: (0, 0))],
        out_shape=[jax.ShapeDtypeStruct((bsz, seq, width), F32),
                   jax.ShapeDtypeStruct((bsz, width), F32)],
        scratch_shapes=[pltpu.VMEM((bsz, tt + 2 * LRU_HALO, width), BF16), pltpu.VMEM((bsz, tt, width), F32),
                        slab(), slab(), slab(), pltpu.VMEM((bsz, width), F32)],
        compiler_params=_cparams("arbitrary"),
        name="lru_scan_bwd" if reverse else "lru_scan_fwd",
    )(p3, p3, p3, shift, conv_w, conv_b.reshape(1, width), wcat, ba.reshape(1, width),
      bx.reshape(1, width), lam.reshape(1, width), h0)


def _rope(x, cos, sin):
    return x * cos + pltpu.roll(x, HEAD_DIM // 2, axis=1) * sin


KV_OUT_WIDTH = 3 * KV_HEADS * HEAD_DIM
PV_SPLITS = 4


def _attn_in_kernel(x_ref, mod_ref, g_ref, wq_ref, wkv_ref, qg_ref, kg_ref, *rest, rope):
    if rope:
        cos_ref, sin_ref, q_ref, kv_ref = rest
    else:
        q_ref, kv_ref = rest
    h = _adaln(x_ref[...], g_ref[...], mod_ref[0:1, :], mod_ref[1:2, :]).astype(BF16)

    def head(xh, gain, scale):
        xh = _rms(xh, gain)
        if rope:
            xh = _rope(xh, cos_ref[...], sin_ref[...])
        return (xh * scale).astype(BF16)

    group_w = GROUP * HEAD_DIM
    for n0 in range(0, Q_HEADS * HEAD_DIM, group_w):
        res = _dot(h, wq_ref[:, n0:n0 + group_w])
        for j in range(GROUP):
            sl = slice(j * HEAD_DIM, (j + 1) * HEAD_DIM)
            q_ref[:, n0 + j * HEAD_DIM:n0 + (j + 1) * HEAD_DIM] = head(res[:, sl], qg_ref[...], HEAD_DIM ** -0.5)
    kv = _dot(h, wkv_ref[...])
    kw = KV_HEADS * HEAD_DIM
    ones = jnp.ones((kv.shape[0], HEAD_DIM), BF16)
    for j in range(KV_HEADS):
        sl = slice(j * HEAD_DIM, (j + 1) * HEAD_DIM)
        kv_ref[:, sl] = head(kv[:, sl], kg_ref[...], 1.0)
        v0 = kw + 2 * j * HEAD_DIM
        kv_ref[:, v0:v0 + HEAD_DIM] = kv[:, kw + j * HEAD_DIM:kw + (j + 1) * HEAD_DIM].astype(BF16)
        kv_ref[:, v0 + HEAD_DIM:v0 + 2 * HEAD_DIM] = ones


def _attn_in(x2d, mod, layer, mod_index, gain, w_q, w_kv, q_g, k_g, tm, rope_tables, seq, name):
    rows, d = x2d.shape
    vec = lambda: pl.BlockSpec((1, HEAD_DIM), lambda i: (0, 0))
    in_specs = [pl.BlockSpec((tm, d), lambda i: (i, 0)), _mod_spec(layer, d, mod_index),
                pl.BlockSpec((1, d), lambda i: (0, 0)),
                pl.BlockSpec(w_q.shape, lambda i: (0, 0)), pl.BlockSpec(w_kv.shape, lambda i: (0, 0)), vec(), vec()]
    args = [x2d, mod, gain.reshape(1, d), w_q, w_kv, q_g.reshape(1, HEAD_DIM), k_g.reshape(1, HEAD_DIM)]
    if rope_tables is not None:
        tpb = seq // tm
        in_specs += [pl.BlockSpec((tm, HEAD_DIM), lambda i: (i % tpb, 0))] * 2
        args += list(rope_tables)
    return pl.pallas_call(
        functools.partial(_attn_in_kernel, rope=rope_tables is not None),
        grid=(rows // tm,),
        in_specs=in_specs,
        out_specs=[pl.BlockSpec((tm, w_q.shape[1]), lambda i: (i, 0)),
                   pl.BlockSpec((tm, KV_OUT_WIDTH), lambda i: (i, 0))],
        out_shape=[jax.ShapeDtypeStruct((rows, w_q.shape[1]), BF16),
                   jax.ShapeDtypeStruct((rows, KV_OUT_WIDTH), BF16)],
        compiler_params=_cparams("arbitrary"),
        name=name,
    )(*args)


def _attn_kernel(q_ref, kvc_ref, *rest, has_latent, tk):
    if has_latent:
        kvl_ref, o0_ref, o1_ref, s0_sc, s1_sc, p0_sc, p1_sc, m1_sc = rest
    else:
        o0_ref, o1_ref, s0_sc, s1_sc, p0_sc, p1_sc, m1_sc = rest
    tq = q_ref.shape[0]
    rows = GROUP * tq
    kw = KV_HEADS * HEAD_DIM
    lanes = HEAD_DIM
    n_ctx = kvc_ref.shape[0]
    n_lat = kvl_ref.shape[0] if has_latent else 0

    @pl.when((pl.program_id(0) == 0) & (pl.program_id(1) == 0))
    def _():
        s1_sc[...] = jnp.zeros_like(s1_sc)
        m1_sc[...] = jnp.zeros_like(m1_sc)

    def lane_groups(s):
        return [s[:, c * lanes:(c + 1) * lanes] for c in range(s.shape[1] // lanes)]

    tkc = min(tk, n_ctx)
    tiles = [(kvc_ref, r0, tkc, r0) for r0 in range(0, n_ctx, tkc)]
    tiles += [(kvl_ref, r0, tk, n_ctx + r0) for r0 in range(0, n_lat, tk)]
    pv_parts = [(kvc_ref, 0, n_ctx, 0)]
    pv_parts += [(kvl_ref, r0, n_lat // PV_SPLITS, n_ctx + r0) for r0 in range(0, n_lat, max(n_lat // PV_SPLITS, 1))]

    def stage(hk_new, s_new_sc, hk_old, s_old_sc, m_old, p_sc, o_ref):
        ksl = slice(hk_new * HEAD_DIM, (hk_new + 1) * HEAD_DIM)
        vsl = slice(kw + 2 * hk_old * HEAD_DIM, kw + 2 * (hk_old + 1) * HEAD_DIM)
        qcat = jnp.concatenate([q_ref[:, (hk_new * GROUP + gi) * HEAD_DIM:(hk_new * GROUP + gi + 1) * HEAD_DIM]
                                for gi in range(GROUP)], axis=0)
        mrun = jnp.full((rows, lanes), -jnp.inf, F32)
        partial, pending = [], list(pv_parts)
        for kv_ref, r0, w, c0 in tiles:
            s = _dot_nt(qcat, kv_ref[r0:r0 + w, ksl])
            s_new_sc[:, c0:c0 + w] = s
            mrun = functools.reduce(jnp.maximum, [mrun] + lane_groups(s))
            p = [jnp.exp(sg - m_old) for sg in lane_groups(s_old_sc[:, c0:c0 + w])]
            p_sc[:, c0:c0 + w] = jnp.concatenate(p, axis=1).astype(BF16)
            while pending and pending[0][3] + pending[0][2] <= c0 + w:
                pv_ref, pr0, pw, pc0 = pending.pop(0)
                partial.append(_dot(p_sc[:, pc0:pc0 + pw], pv_ref[pr0:pr0 + pw, vsl]))
        acc = functools.reduce(jnp.add, partial)
        o = acc[:, 0:HEAD_DIM] / acc[:, HEAD_DIM:HEAD_DIM + 1]
        for gi in range(GROUP):
            o_ref[:, gi * HEAD_DIM:(gi + 1) * HEAD_DIM] = o[gi * tq:(gi + 1) * tq].astype(o_ref.dtype)
        return jnp.broadcast_to(jnp.max(mrun, axis=-1, keepdims=True), (rows, lanes))

    m0 = stage(0, s0_sc, 1, s1_sc, m1_sc[...], p1_sc, o1_ref)
    m1_sc[...] = stage(1, s1_sc, 0, s0_sc, m0, p0_sc, o0_ref)


def _attention(q, kv_ctx, kv_lat, bsz, tq, tk):
    rows = q.shape[0]
    nq = rows // bsz // tq
    n_ctx = kv_ctx.shape[0] // bsz
    has_latent = kv_lat is not None
    n_keys = n_ctx + (kv_lat.shape[0] // bsz if has_latent else 0)
    assert n_ctx % min(tk, n_ctx) == 0 and (n_keys - n_ctx) % (tk * PV_SPLITS) == 0
    in_specs = [pl.BlockSpec((tq, q.shape[1]), lambda b, j: (b * nq + jnp.minimum(j, nq - 1), 0)),
                pl.BlockSpec((n_ctx, kv_ctx.shape[1]), lambda b, j: (b, 0))]
    args = [q, kv_ctx]
    if has_latent:
        in_specs.append(pl.BlockSpec((n_keys - n_ctx, kv_lat.shape[1]), lambda b, j: (b, 0)))
        args.append(kv_lat)
    group_w = GROUP * HEAD_DIM
    scores = lambda: pltpu.VMEM((GROUP * tq, n_keys), F32)
    probs = lambda: pltpu.VMEM((GROUP * tq, n_keys), BF16)
    return pl.pallas_call(
        functools.partial(_attn_kernel, has_latent=has_latent, tk=tk),
        grid=(bsz, nq + 1),
        in_specs=in_specs,
        out_specs=[pl.BlockSpec((tq, group_w), lambda b, j: (b * nq + jnp.minimum(j, nq - 1), 0)),
                   pl.BlockSpec((tq, group_w), lambda b, j: (b * nq + jnp.maximum(j - 1, 0), 0))],
        out_shape=[jax.ShapeDtypeStruct((rows, group_w), BF16), jax.ShapeDtypeStruct((rows, group_w), BF16)],
        scratch_shapes=[scores(), scores(), probs(), probs(), pltpu.VMEM((GROUP * tq, HEAD_DIM), F32)],
        compiler_params=_cparams("arbitrary", "arbitrary"),
        name="attention" if has_latent else "attention_ctx",
    )(*args)


def _rope_tables(seq):
    n_rows = seq // GRID_W
    row = jnp.repeat(jnp.arange(n_rows), GRID_W)
    col = jnp.tile(jnp.arange(GRID_W), n_rows)
    n_pairs_axis = HEAD_DIM // 4
    inv_freq = ROPE_THETA ** (-jnp.arange(n_pairs_axis, dtype=F32) / n_pairs_axis)
    ang = jnp.concatenate([row[:, None] * inv_freq, col[:, None] * inv_freq], axis=-1)
    cos, sin = jnp.cos(ang), jnp.sin(ang)
    return jnp.concatenate([cos, cos], axis=-1), jnp.concatenate([-sin, sin], axis=-1)


def _tile(n, pref):
    t = min(n, pref)
    assert n % t == 0, (n, pref)
    return t


def kernel(x, c, ctx, c_ctx, norm_mix_g, norm_mlp_g, w_mod, b_mod, w_mlp1, w_mlp2, gla_w_in, gla_w_up_f, gla_b_f, gla_w_up_b, gla_b_b, gla_norm_g, gla_w_o, lru_w_in, lru_conv_w, lru_conv_b, lru_wa_f, lru_ba_f, lru_wx_f, lru_bx_f, lru_lam_f, lru_wa_b, lru_ba_b, lru_wx_b, lru_bx_b, lru_lam_b, lru_w_o, attn_w_in, attn_q_g, attn_k_g, attn_w_o, final_g):
    bsz, seq, d = x.shape
    n_ctx = ctx.shape[1]
    depth = w_mod.shape[0]
    assert bsz + 1 <= MOD_ROWS and d == GLA_HEADS * GLA_DV == Q_HEADS * HEAD_DIM

    mod = _modulation(c, c_ctx, w_mod, b_mod)
    xl = x.reshape(bsz * seq, d)
    xc = ctx.reshape(bsz * n_ctx, d)

    tm_l = _tile(seq, 512)
    tm_c = _tile(n_ctx, 256)
    lat_idx = lambda i: 1 + (i * tm_l) // seq
    ctx_idx = lambda i: 0

    for layer in range(depth):
        need_ctx = layer < depth - 1
        kind, j = layer % N_MIXERS, layer // N_MIXERS
        gmix = norm_mix_g[layer]

        if kind == 0:
            dq = GLA_HEADS * GLA_DK
            main_w = gla_w_in[j][:, :2 * dq + 2 * d].astype(BF16)
            gate_w = jnp.pad(gla_w_in[j][:, 2 * dq + 2 * d:], ((0, 0), (0, GLA_GATE_PAD - 2 * GLA_GATE_RANK))).astype(BF16)
            zpad = jnp.zeros((GLA_GATE_PAD - 2 * GLA_GATE_RANK, dq), F32)
            wup_f = jnp.concatenate([gla_w_up_f[j], jnp.zeros_like(gla_w_up_b[j]), zpad], axis=0).astype(BF16)
            wup_b = jnp.concatenate([jnp.zeros_like(gla_w_up_f[j]), gla_w_up_b[j], zpad], axis=0).astype(BF16)
            bup_f, bup_b = gla_b_f[j].reshape(1, dq), gla_b_b[j].reshape(1, dq)
            w_o = gla_w_o[j].astype(BF16)
            proj = {}
            for name, (xs, midx, tm, slen) in zip(("lat", "ctx"), [(xl, lat_idx, tm_l, seq), (xc, ctx_idx, tm_c, n_ctx)]):
                proj[name] = _ln_matmul(xs, mod, layer, midx, gmix, [main_w, gate_w], [BF16, BF16], tm,
                                        name="gla_in_" + name)
            s0 = jnp.zeros((bsz, GLA_HEADS, GLA_DK, GLA_DV), F32)
            ch_c, ch_l = _tile(n_ctx, 128), _tile(seq, 128)
            ocf, ocb, s_cf, s_cb = _gla_scan(*proj["ctx"], wup_f, wup_b, bup_f, bup_b, s0, s0, bsz, ch_c)
            olf, olb, _, _ = _gla_scan(*proj["lat"], wup_f, wup_b, bup_f, bup_b, s_cf, s_cb, bsz, ch_l)
            ng = gla_norm_g[j].reshape(1, GLA_DV)

            def gla_out(of, ob, qkvr, xs, midx, tm):
                blk = lambda i: (i, 0)
                return _proj_res("gla", [of, ob, qkvr], [pl.BlockSpec((tm, d), blk), pl.BlockSpec((tm, d), blk),
                                                        pl.BlockSpec((tm, d), lambda i: (i, (2 * dq + d) // d))],
                                 w_o, xs, mod, layer, midx, tm, extra=[ng],
                                 extra_specs=[pl.BlockSpec((1, GLA_DV), lambda i: (0, 0))])

            xl = gla_out(olf, olb, proj["lat"][0], xl, lat_idx, tm_l)
            if need_ctx:
                xc = gla_out(ocf, ocb, proj["ctx"][0], xc, ctx_idx, tm_c)

        elif kind == 1:
            w_in = lru_w_in[j].astype(BF16)
            w_o = lru_w_o[j].astype(BF16)
            wcat_f = jnp.concatenate([lru_wa_f[j], lru_wx_f[j]], axis=-1).astype(BF16)
            wcat_b = jnp.concatenate([lru_wa_b[j], lru_wx_b[j]], axis=-1).astype(BF16)
            p_l, = _ln_matmul(xl, mod, layer, lat_idx, gmix, [w_in], [BF16], tm_l, name="lru_in_lat")
            p_c, = _ln_matmul(xc, mod, layer, ctx_idx, gmix, [w_in], [BF16], tm_c, name="lru_in_ctx")
            h0 = jnp.zeros((bsz, d), F32)
            fwd = (lru_conv_w[j], lru_conv_b[j], wcat_f, lru_ba_f[j], lru_bx_f[j], lru_lam_f[j])
            bwd = (lru_conv_w[j], lru_conv_b[j], wcat_b, lru_ba_b[j], lru_bx_b[j], lru_lam_b[j])
            tt_c, tt_l = _tile(n_ctx, 128), _tile(seq, 128)
            p_c3, p_l3 = p_c.reshape(bsz, n_ctx, 2 * d), p_l.reshape(bsz, seq, 2 * d)
            hcf, s_cf = _lru_scan(p_c3, *fwd, h0, tt_c, False)
            hcb, s_cb = _lru_scan(p_c3, *bwd, h0, tt_c, True)
            hlf, _ = _lru_scan(p_l3, *fwd, s_cf, tt_l, False)
            hlb, _ = _lru_scan(p_l3, *bwd, s_cb, tt_l, True)

            def lru_out(hf, hb, p2d, xs, midx, tm):
                blk = lambda i: (i, 0)
                spec = lambda: pl.BlockSpec((tm, d), blk)
                return _proj_res("lru", [hf.reshape(-1, d), hb.reshape(-1, d), p2d], [spec(), spec(), spec()],
                                 w_o, xs, mod, layer, midx, tm)

            xl = lru_out(hlf, hlb, p_l, xl, lat_idx, tm_l)
            if need_ctx:
                xc = lru_out(hcf, hcb, p_c, xc, ctx_idx, tm_c)

        else:
            qw = Q_HEADS * HEAD_DIM
            w_q = attn_w_in[j][:, :qw].astype(BF16)
            w_kv = attn_w_in[j][:, qw:].astype(BF16)
            w_o = attn_w_o[j].astype(BF16)
            qg, kg = attn_q_g[j], attn_k_g[j]
            q_l, kv_l = _attn_in(xl, mod, layer, lat_idx, gmix, w_q, w_kv, qg, kg, tm_l, _rope_tables(seq), seq,
                                 "attn_in_lat")
            q_c, kv_c = _attn_in(xc, mod, layer, ctx_idx, gmix, w_q, w_kv, qg, kg, tm_c, None, n_ctx, "attn_in_ctx")
            tk = _tile(seq // PV_SPLITS, 512)
            o_l = _attention(q_l, kv_c, kv_l, bsz, _tile(seq, 128), tk)
            half = lambda tm: [pl.BlockSpec((tm, d // KV_HEADS), lambda i: (i, 0))] * KV_HEADS
            xl = _proj_res("plain", o_l, half(tm_l), w_o, xl, mod, layer, lat_idx, tm_l)
            if need_ctx:
                o_c = _attention(q_c, kv_c, None, bsz, _tile(n_ctx, 128), tk)
                xc = _proj_res("plain", o_c, half(tm_c), w_o, xc, mod, layer, ctx_idx, tm_c)

        w1, w2 = w_mlp1[layer].astype(BF16), w_mlp2[layer].astype(BF16)
        tf = _tile(w1.shape[1], 1024)
        tmm_l = _tile(seq, 1024)
        xl = _mlp(xl, mod, layer, lambda i: 1 + (i * tmm_l) // seq, norm_mlp_g[layer], w1, w2, tmm_l, tf,
                  final_gain=None if need_ctx else final_g)
        if need_ctx:
            xc = _mlp(xc, mod, layer, ctx_idx, norm_mlp_g[layer], w1, w2, _tile(bsz * n_ctx, 1024), tf)

    return xl.reshape(bsz, seq, d)
```

```python
import functools

import numpy as np
import jax
import jax.numpy as jnp
from jax import lax
from jax.experimental import pallas as pl
from jax.experimental.pallas import tpu as pltpu

F32 = jnp.float32
BF16 = jnp.bfloat16

EPS = 1e-6
N_MOD = 6
N_MIXERS = 3
MOD_ROWS = 16

GLA_HEADS = 4
GLA_DK = 128
GLA_DV = 256
GLA_GATE_RANK = 16
GLA_GATE_TAU = 16.0
GLA_GATE_PAD = 128

RNN_BLOCKS = 8
RNN_BLOCK_DIM = 128
CONV_WIDTH = 4
CONV_LEFT = 2
LRU_C = 8.0

HEAD_DIM = 128
Q_HEADS = 8
KV_HEADS = 2
GROUP = Q_HEADS // KV_HEADS
GRID_W = 64
ROPE_THETA = 10000.0

VMEM_LIMIT_BYTES = 56 * 1024 * 1024


def _cparams(*sem):
    return pltpu.CompilerParams(dimension_semantics=sem, vmem_limit_bytes=VMEM_LIMIT_BYTES)


def _dot(a, b):
    return jnp.dot(a, b, preferred_element_type=F32)


def _dot_nt(a, b):
    return lax.dot_general(a, b, (((1,), (1,)), ((), ())), preferred_element_type=F32)


def _dot_tn(a, b):
    return lax.dot_general(a, b, (((0,), (0,)), ((), ())), preferred_element_type=F32)


def _rms(x, g):
    return x * lax.rsqrt(jnp.mean(x * x, axis=-1, keepdims=True) + EPS) * g


def _adaln(x, g, shift, scale):
    return _rms(x, g) * (1.0 + scale) + shift


def _mod_kernel(s_ref, w_ref, b_ref, o_ref):
    s = s_ref[...]
    s = s * jax.nn.sigmoid(s)
    o_ref[...] = _dot(s.astype(BF16), w_ref[...].astype(BF16)) + b_ref[...]


def _modulation(c, c_ctx, w_mod, b_mod):
    depth, d, nd = w_mod.shape
    bsz = c.shape[0]
    rows = jnp.concatenate([c_ctx[None, :], c, jnp.zeros((MOD_ROWS - 1 - bsz, d), F32)], axis=0)
    tn = min(nd, 1536)
    out = pl.pallas_call(
        _mod_kernel,
        grid=(depth, nd // tn),
        in_specs=[pl.BlockSpec((MOD_ROWS, d), lambda l, j: (0, 0)),
                  pl.BlockSpec((None, d, tn), lambda l, j: (l, 0, j)),
                  pl.BlockSpec((None, 1, tn), lambda l, j: (l, 0, j))],
        out_specs=pl.BlockSpec((None, MOD_ROWS, tn), lambda l, j: (l, 0, j)),
        out_shape=jax.ShapeDtypeStruct((depth, MOD_ROWS, nd), F32),
        compiler_params=_cparams("arbitrary", "arbitrary"),
        name="modulation",
    )(rows, w_mod, b_mod.reshape(depth, 1, nd))
    return out.reshape(depth, MOD_ROWS, N_MOD, d)


def _mod_spec(layer, d, mod_index):
    return pl.BlockSpec((None, None, N_MOD, d), lambda i, *_: (layer, mod_index(i), 0, 0))


def _ln_matmul_kernel(x_ref, mod_ref, g_ref, *refs, n_out, mod_off, col_chunk):
    w_refs, o_refs = refs[:n_out], refs[n_out:]
    h = _adaln(x_ref[...], g_ref[...], mod_ref[mod_off:mod_off + 1, :],
               mod_ref[mod_off + 1:mod_off + 2, :]).astype(BF16)
    for w_ref, o_ref in zip(w_refs, o_refs):
        n = w_ref.shape[1]
        for n0 in range(0, n, col_chunk):
            n1 = min(n, n0 + col_chunk)
            o_ref[:, n0:n1] = _dot(h, w_ref[:, n0:n1]).astype(o_ref.dtype)


def _ln_matmul(x2d, mod, layer, mod_index, gain, ws, out_dtypes, tm, name="ln_matmul"):
    rows, d = x2d.shape
    in_specs = [pl.BlockSpec((tm, d), lambda i: (i, 0)), _mod_spec(layer, d, mod_index),
                pl.BlockSpec((1, d), lambda i: (0, 0))]
    in_specs += [pl.BlockSpec(w.shape, lambda i: (0, 0)) for w in ws]
    return pl.pallas_call(
        functools.partial(_ln_matmul_kernel, n_out=len(ws), mod_off=0, col_chunk=512),
        grid=(rows // tm,),
        in_specs=in_specs,
        out_specs=[pl.BlockSpec((tm, w.shape[1]), lambda i: (i, 0)) for w in ws],
        out_shape=[jax.ShapeDtypeStruct((rows, w.shape[1]), dt) for w, dt in zip(ws, out_dtypes)],
        compiler_params=_cparams("arbitrary"),
        name=name,
    )(x2d, mod, gain.reshape(1, d), *ws)


def _proj_res_kernel(y0_ref, y1_ref, w_ref, x_ref, mod_ref, o_ref):
    y = jnp.concatenate([y0_ref[...], y1_ref[...]], axis=1)
    o_ref[...] = x_ref[...] + mod_ref[2:3, :] * _dot(y, w_ref[...])


def _gla_proj_res_kernel(of_ref, ob_ref, r_ref, ng_ref, w_ref, x_ref, mod_ref, o_ref):
    o = of_ref[...] + ob_ref[...]
    r = r_ref[...].astype(F32)
    ys = []
    for h in range(GLA_HEADS):
        sl = slice(h * GLA_DV, (h + 1) * GLA_DV)
        rh = r[:, sl]
        ys.append(_rms(o[:, sl], ng_ref[...]) * (rh * jax.nn.sigmoid(rh)))
    y = jnp.concatenate(ys, axis=1).astype(BF16)
    o_ref[...] = x_ref[...] + mod_ref[2:3, :] * _dot(y, w_ref[...])


def _lru_proj_res_kernel(hf_ref, hb_ref, gate_ref, w_ref, x_ref, mod_ref, o_ref):
    y = ((hf_ref[...] + hb_ref[...]) * jax.nn.gelu(gate_ref[...].astype(F32))).astype(BF16)
    o_ref[...] = x_ref[...] + mod_ref[2:3, :] * _dot(y, w_ref[...])


def _proj_res(kind, operands, operand_specs, w, x2d, mod, layer, mod_index, tm, extra=(), extra_specs=()):
    rows, d = x2d.shape
    body = {"plain": _proj_res_kernel, "gla": _gla_proj_res_kernel, "lru": _lru_proj_res_kernel}[kind]
    n_in = len(operands) + len(extra) + 1
    return pl.pallas_call(
        body,
        grid=(rows // tm,),
        in_specs=[*operand_specs, *extra_specs, pl.BlockSpec(w.shape, lambda i: (0, 0)),
                  pl.BlockSpec((tm, d), lambda i: (i, 0)), _mod_spec(layer, d, mod_index)],
        out_specs=pl.BlockSpec((tm, d), lambda i: (i, 0)),
        out_shape=jax.ShapeDtypeStruct((rows, d), F32),
        input_output_aliases={n_in: 0} if layer > 0 else {},
        compiler_params=_cparams("arbitrary"),
        name=kind + "_proj_res",
    )(*operands, *extra, w, x2d, mod)


def _mlp_kernel(x_ref, mod_ref, g_ref, w1_ref, w2_ref, *rest, final_norm):
    if final_norm:
        fg_ref, o_ref, h_sc, acc_sc = rest
    else:
        o_ref, h_sc, acc_sc = rest
    j = pl.program_id(1)

    @pl.when(j == 0)
    def _():
        h_sc[...] = _adaln(x_ref[...], g_ref[...], mod_ref[3:4, :], mod_ref[4:5, :]).astype(BF16)
        acc_sc[...] = jnp.zeros_like(acc_sc)

    a = jnp.maximum(_dot(h_sc[...], w1_ref[...]), 0.0)
    acc_sc[...] += _dot((a * a).astype(BF16), w2_ref[...])

    @pl.when(j == pl.num_programs(1) - 1)
    def _():
        y = x_ref[...] + mod_ref[5:6, :] * acc_sc[...]
        o_ref[...] = _rms(y, fg_ref[...]) if final_norm else y


def _mlp(x2d, mod, layer, mod_index, gain, w1, w2, tm, tf, final_gain=None):
    rows, d = x2d.shape
    dff = w1.shape[1]
    vec = lambda: pl.BlockSpec((1, d), lambda i, j: (0, 0))
    in_specs = [pl.BlockSpec((tm, d), lambda i, j: (i, 0)), _mod_spec(layer, d, mod_index), vec(),
                pl.BlockSpec((d, tf), lambda i, j: (0, j)),
                pl.BlockSpec((tf, d), lambda i, j: (j, 0))]
    args = [x2d, mod, gain.reshape(1, d), w1, w2]
    if final_gain is not None:
        in_specs.append(vec())
        args.append(final_gain.reshape(1, d))
    return pl.pallas_call(
        functools.partial(_mlp_kernel, final_norm=final_gain is not None),
        grid=(rows // tm, dff // tf),
        in_specs=in_specs,
        out_specs=pl.BlockSpec((tm, d), lambda i, j: (i, 0)),
        out_shape=jax.ShapeDtypeStruct((rows, d), F32),
        scratch_shapes=[pltpu.VMEM((tm, d), BF16), pltpu.VMEM((tm, d), F32)],
        input_output_aliases={0: 0},
        compiler_params=_cparams("arbitrary", "arbitrary"),
        name="mlp",
    )(*args)


GLA_MATMUL_LEVEL_ROWS = 8
GLA_FAST_MAX_DECAY = 60.0


def _gla_tables(chunk, reverse):
    levels = int(np.log2(chunk))
    idx = np.arange(chunk)
    t, r = idx[:, None], idx[None, :]
    if not reverse:
        q_incl = (r <= t)
        k_rest = (r > t)
    else:
        q_incl = (r >= t)
        k_rest = (r < t)
    mats = [q_incl, k_rest]
    masks = [np.eye(chunk, dtype=bool)]
    for lv in range(1, levels + 1):
        m = 1 << (lv - 1)
        mid = (idx // (2 * m)) * (2 * m) + m
        upper = idx >= mid
        midc = mid[:, None]
        if not reverse:
            pat = np.where(upper[:, None], (r >= midc) & (r <= t), (r > t) & (r < midc))
            mask = (upper[:, None] & ~upper[None, :])
        else:
            pat = np.where(upper[:, None], (r >= midc) & (r < t), (r >= t) & (r < midc))
            mask = (~upper[:, None] & upper[None, :])
        mask = mask & ((idx[:, None] // (2 * m)) == (idx[None, :] // (2 * m)))
        if 2 * m <= GLA_MATMUL_LEVEL_ROWS:
            mats.append(pat)
        masks.append(mask)
    a = np.concatenate(mats, axis=0).astype(np.float32)
    a2 = np.concatenate([a, a], axis=1)
    return jnp.asarray(a2, BF16), jnp.asarray(np.stack(masks).astype(np.float32), F32)


def _gla_gates(qkvr_ref, gf_ref, wup_ref, bup_ref, a_ref, reverse):
    chunk = qkvr_ref.shape[0]
    dq = GLA_HEADS * GLA_DK
    graw = _dot(gf_ref[...], wup_ref[...]) + bup_ref[...]
    g = (jnp.minimum(graw, 0.0) - jnp.log(1.0 + jnp.exp(-jnp.abs(graw)))) * (1.0 / GLA_GATE_TAU)
    g_hi = g.astype(BF16)
    g_cat = jnp.concatenate([g_hi, (g - g_hi.astype(F32)).astype(BF16)], axis=0)

    def partial_sums(block):
        return _dot(a_ref[block * chunk:(block + 1) * chunk, :], g_cat)

    st = dict(partial_sums=partial_sums, bq=partial_sums(0), bk=partial_sums(1), reverse=reverse, chunk=chunk,
              q_all=qkvr_ref[:, 0:dq].astype(F32) * (GLA_DK ** -0.5), k_all=qkvr_ref[:, dq:2 * dq].astype(F32))
    st["qd"] = (st["q_all"] * jnp.exp(st["bq"])).astype(BF16)
    st["kd"] = (st["k_all"] * jnp.exp(st["bk"])).astype(BF16)
    st["btot"] = st["bq"][0:1, :] if reverse else st["bq"][chunk - 1:chunk, :]
    return st


def _gla_levels(st, z_sc, lv_range):
    chunk, reverse, bq, bk = st["chunk"], st["reverse"], st["bq"], st["bk"]
    q_all, k_all = st["q_all"], st["k_all"]
    row = lax.broadcasted_iota(jnp.int32, (chunk, 1), 0)
    for lv in lv_range:
        m = 1 << (lv - 1)
        if 2 * m <= GLA_MATMUL_LEVEL_ROWS:
            use_q = ((row // m) % 2) == (0 if reverse else 1)
            z = jnp.where(use_q, q_all, k_all) * jnp.exp(st["partial_sums"](1 + lv))
        else:
            pieces = []
            for lo in range(0, chunk, 2 * m):
                mid, hi = lo + m, lo + 2 * m
                if not reverse:
                    pieces.append(k_all[lo:mid] * jnp.exp(bk[lo:mid] - bk[mid - 1:mid]))
                    pieces.append(q_all[mid:hi] * jnp.exp(bq[mid:hi] - bq[mid - 1:mid]))
                else:
                    pieces.append(q_all[lo:mid] * jnp.exp(bq[lo:mid] - bq[mid:mid + 1]))
                    pieces.append(k_all[mid:hi] * jnp.exp(bk[mid:hi] - bk[mid:mid + 1]))
            z = jnp.concatenate(pieces, axis=0)
        z_sc[lv - 1] = z.astype(BF16)


def _gla_scores_by_level(st, h, mask_ref, z_sc, levels):
    sl = slice(h * GLA_DK, (h + 1) * GLA_DK)
    chunk, reverse = st["chunk"], st["reverse"]
    sub = 8
    diag = mask_ref[0] * jnp.sum(st["q_all"][:, sl] * st["k_all"][:, sl], axis=-1, keepdims=True)
    tiles = [diag[r:r + sub] for r in range(0, chunk, sub)]
    for lv in range(1, levels + 1):
        m = 1 << (lv - 1)
        if 2 * m <= GLA_MATMUL_LEVEL_ROWS:
            q_rows = [(0, chunk)]
        else:
            q_rows = [(lo, lo + m) if reverse else (lo + m, lo + 2 * m) for lo in range(0, chunk, 2 * m)]
        zq = jnp.concatenate([z_sc[lv - 1, lo:hi, sl] for lo, hi in q_rows], axis=0)
        mq = jnp.concatenate([mask_ref[lv, lo:hi, :] for lo, hi in q_rows], axis=0)
        part = mq * _dot_nt(zq, z_sc[lv - 1, :, sl])
        src = 0
        for lo, hi in q_rows:
            for r in range(lo, hi, sub):
                tiles[r // sub] = tiles[r // sub] + part[src:src + sub]
                src += sub
    return jnp.concatenate(tiles, axis=0)


def _gla_head_update(st, h, scores, qkvr_ref, s_sc, o_ref):
    dq = GLA_HEADS * GLA_DK
    sl = slice(h * GLA_DK, (h + 1) * GLA_DK)
    vh = qkvr_ref[:, 2 * dq + h * GLA_DV:2 * dq + (h + 1) * GLA_DV]
    s_old = s_sc[h]
    o_ref[:, h * GLA_DV:(h + 1) * GLA_DV] = (_dot(scores.astype(BF16), vh)
                                            + _dot(st["qd"][:, sl], s_old.astype(BF16)))
    etot = jnp.exp(st["btot"][:, sl])
    ecol = jnp.transpose(jnp.broadcast_to(etot, (GLA_DK, GLA_DK)))
    s_sc[h] = s_old * jnp.concatenate([ecol] * (GLA_DV // GLA_DK), axis=1) + _dot_tn(st["kd"][:, sl], vh)


def _gla_kernel(qkvr_f, gf_f, qkvr_b, gf_b, wup_f, wup_b, bup_f, bup_b, a_f, a_b, mask_f, mask_b, pair_f, pair_b,
                s0_f, s0_b, o_f, o_b, sfin_f, sfin_b, s_sc_f, s_sc_b, z_sc_f, z_sc_b):
    c = pl.program_id(1)
    levels = mask_f.shape[0] - 1
    dirs = ((qkvr_f, mask_f, pair_f, z_sc_f, s_sc_f, o_f), (qkvr_b, mask_b, pair_b, z_sc_b, s_sc_b, o_b))

    @pl.when(c == 0)
    def _():
        s_sc_f[...] = s0_f[...]
        s_sc_b[...] = s0_b[...]

    sts = (_gla_gates(qkvr_f, gf_f, wup_f, bup_f, a_f, False), _gla_gates(qkvr_b, gf_b, wup_b, bup_b, a_b, True))
    worst = jnp.maximum(jnp.max(jnp.abs(sts[0]["btot"])), jnp.max(jnp.abs(sts[1]["btot"])))
    small_decay = worst <= GLA_FAST_MAX_DECAY

    @pl.when(small_decay)
    def _():
        kinv = [(st["k_all"] * jnp.exp(-st["bq"])).astype(BF16) for st in sts]
        for h in range(GLA_HEADS):
            sl = slice(h * GLA_DK, (h + 1) * GLA_DK)
            for st, kv, (qkvr, _, pair, _, s_sc, o_ref) in zip(sts, kinv, dirs):
                scores = pair[...] * _dot_nt(st["qd"][:, sl], kv[:, sl])
                _gla_head_update(st, h, scores, qkvr, s_sc, o_ref)

    @pl.when(jnp.logical_not(small_decay))
    def _():
        for st, (_, _, _, z_sc, _, _) in zip(sts, dirs):
            _gla_levels(st, z_sc, range(1, levels + 1))
        for h in range(GLA_HEADS):
            for st, (qkvr, mask, _, z_sc, s_sc, o_ref) in zip(sts, dirs):
                _gla_head_update(st, h, _gla_scores_by_level(st, h, mask, z_sc, levels), qkvr, s_sc, o_ref)

    @pl.when(c == pl.num_programs(1) - 1)
    def _():
        sfin_f[...] = s_sc_f[...]
        sfin_b[...] = s_sc_b[...]


def _gla_scan(qkvr, gfeat, wup_f, wup_b, bup_f, bup_b, s0_f, s0_b, bsz, chunk):
    rows = qkvr.shape[0]
    nc = rows // bsz // chunk
    a_f, mask_f = _gla_tables(chunk, False)
    a_b, mask_b = _gla_tables(chunk, True)
    tri = np.tril(np.ones((chunk, chunk), np.float32))
    pair_f, pair_b = jnp.asarray(tri), jnp.asarray(tri.T)
    dv = GLA_HEADS * GLA_DV
    fwd_block = lambda b, c: (b * nc + c, 0)
    bwd_block = lambda b, c: (b * nc + nc - 1 - c, 0)
    const2 = lambda x: pl.BlockSpec(x.shape, lambda b, c: (0, 0))
    const3 = lambda x: pl.BlockSpec(x.shape, lambda b, c: (0, 0, 0))
    state_spec = pl.BlockSpec((None, GLA_HEADS, GLA_DK, GLA_DV), lambda b, c: (b, 0, 0, 0))
    state_shape = jax.ShapeDtypeStruct((bsz, GLA_HEADS, GLA_DK, GLA_DV), F32)
    state_sc = lambda: pltpu.VMEM((GLA_HEADS, GLA_DK, GLA_DV), F32)
    z_sc = lambda: pltpu.VMEM((mask_f.shape[0] - 1, chunk, GLA_HEADS * GLA_DK), BF16)
    return pl.pallas_call(
        _gla_kernel,
        grid=(bsz, nc),
        in_specs=[pl.BlockSpec((chunk, qkvr.shape[1]), fwd_block), pl.BlockSpec((chunk, gfeat.shape[1]), fwd_block),
                  pl.BlockSpec((chunk, qkvr.shape[1]), bwd_block), pl.BlockSpec((chunk, gfeat.shape[1]), bwd_block),
                  const2(wup_f), const2(wup_b), const2(bup_f), const2(bup_b), const2(a_f), const2(a_b),
                  const3(mask_f), const3(mask_b), const2(pair_f), const2(pair_b), state_spec, state_spec],
        out_specs=[pl.BlockSpec((chunk, dv), fwd_block), pl.BlockSpec((chunk, dv), bwd_block), state_spec, state_spec],
        out_shape=[jax.ShapeDtypeStruct((rows, dv), F32), jax.ShapeDtypeStruct((rows, dv), F32),
                   state_shape, state_shape],
        scratch_shapes=[state_sc(), state_sc(), z_sc(), z_sc()],
        compiler_params=_cparams("arbitrary", "arbitrary"),
        name="gla_scan",
    )(qkvr, gfeat, qkvr, gfeat, wup_f, wup_b, bup_f, bup_b, a_f, a_b, mask_f, mask_b, pair_f, pair_b, s0_f, s0_b)


LRU_HALO = 16
CONV_SHIFTS = tuple(j - CONV_LEFT for j in range(CONV_WIDTH) if j != CONV_LEFT)
F32_TINY = float(np.finfo(np.float32).tiny)


def _lru_shift_matrix(tt):
    sel = np.zeros((len(CONV_SHIFTS) * tt, tt + 2 * LRU_HALO), np.float32)
    for i, off in enumerate(CONV_SHIFTS):
        sel[i * tt + np.arange(tt), LRU_HALO + np.arange(tt) + off] = 1.0
    return jnp.asarray(sel, BF16)


def _lru_kernel(x_ref, prev_ref, next_ref, shift_ref, cw_ref, cb_ref, wcat_ref, ba_ref, bx_ref, lam_ref, h0_ref,
                hs_ref, hfin_ref, xs_sc, xc_sc, a_sc, u_sc, hs_sc, h_sc, *, reverse):
    i = pl.program_id(0)
    n = pl.num_programs(0)
    tidx = n - 1 - i if reverse else i
    bsz, tt, width = x_ref.shape
    pitch = a_sc.shape[1] // bsz
    bd = RNN_BLOCK_DIM

    @pl.when(i == 0)
    def _():
        h_sc[...] = h0_ref[...]

    xs_sc[:, 0:LRU_HALO, :] = jnp.where(tidx > 0, prev_ref[...], jnp.zeros_like(prev_ref))
    xs_sc[:, LRU_HALO:LRU_HALO + tt, :] = x_ref[...]
    xs_sc[:, LRU_HALO + tt:2 * LRU_HALO + tt, :] = jnp.where(tidx < n - 1, next_ref[...], jnp.zeros_like(next_ref))
    for b in range(bsz):
        shifted = _dot(shift_ref[...], xs_sc[b])
        xc = cb_ref[...] + cw_ref[CONV_LEFT:CONV_LEFT + 1, :] * x_ref[b].astype(F32)
        for k, off in enumerate(CONV_SHIFTS):
            j = off + CONV_LEFT
            xc = xc + cw_ref[j:j + 1, :] * shifted[k * tt:(k + 1) * tt]
        xc_sc[b] = xc

    lam = lam_ref[...]
    neg4sp = -0.5 * LRU_C * (jnp.maximum(-lam, 0.0) + jnp.log1p(jnp.exp(-jnp.abs(lam))))
    for nb in range(RNN_BLOCKS):
        sl = slice(nb * bd, (nb + 1) * bd)
        xc2 = xc_sc[:, :, sl].reshape(bsz * tt, bd)
        ri = _dot(xc2.astype(BF16), wcat_ref[nb])
        log_a = neg4sp[:, sl] + neg4sp[:, sl] * jnp.tanh(0.5 * (ri[:, :bd] + ba_ref[:, sl]))
        ig = 0.5 + 0.5 * jnp.tanh(0.5 * (ri[:, bd:] + bx_ref[:, sl]))
        th = jnp.tanh(log_a)
        one_minus_a2 = (-2.0 * th) / (1.0 - th)
        root = one_minus_a2 * lax.rsqrt(jnp.maximum(one_minus_a2, F32_TINY))
        a = jnp.exp(log_a)
        u = root * (ig * xc2)
        for b in range(bsz):
            a_sc[nb, b * pitch:b * pitch + tt, :] = a[b * tt:(b + 1) * tt]
            u_sc[nb, b * pitch:b * pitch + tt, :] = u[b * tt:(b + 1) * tt]

    def step(k, hs):
        t = tt - 1 - k if reverse else k
        rows = pl.ds(t, bsz, stride=pitch)
        new = []
        for nb in range(RNN_BLOCKS):
            h = a_sc[nb, rows, :] * hs[nb] + u_sc[nb, rows, :]
            hs_sc[nb, rows, :] = h
            new.append(h)
        return tuple(new)

    h_init = tuple(h_sc[:, nb * bd:(nb + 1) * bd] for nb in range(RNN_BLOCKS))
    h_last = lax.fori_loop(0, tt, step, h_init, unroll=4)
    for nb in range(RNN_BLOCKS):
        sl = slice(nb * bd, (nb + 1) * bd)
        h_sc[:, sl] = h_last[nb]
        for b in range(bsz):
            hs_ref[b, :, sl] = hs_sc[nb, b * pitch:b * pitch + tt, :]

    @pl.when(i == n - 1)
    def _():
        hfin_ref[...] = h_sc[...]


def _lru_scan(p3, conv_w, conv_b, wcat, ba, bx, lam, h0, tt, reverse):
    bsz, seq, two_w = p3.shape
    width = two_w // 2
    n = seq // tt
    hb = tt // LRU_HALO
    pitch = tt + 8
    assert tt % LRU_HALO == 0 and width == RNN_BLOCKS * RNN_BLOCK_DIM
    shift = _lru_shift_matrix(tt)

    def tix(i):
        return n - 1 - i if reverse else i

    vec = lambda: pl.BlockSpec((1, width), lambda i: (0, 0))
    slab = lambda: pltpu.VMEM((RNN_BLOCKS, bsz * pitch, RNN_BLOCK_DIM), F32)
    return pl.pallas_call(
        functools.partial(_lru_kernel, reverse=reverse),
        grid=(n,),
        in_specs=[pl.BlockSpec((bsz, tt, width), lambda i: (0, tix(i), 1)),
                  pl.BlockSpec((bsz, LRU_HALO, width), lambda i: (0, jnp.maximum(tix(i) * hb - 1, 0), 1)),
                  pl.BlockSpec((bsz, LRU_HALO, width), lambda i: (0, jnp.minimum((tix(i) + 1) * hb, seq // LRU_HALO - 1), 1)),
                  pl.BlockSpec(shift.shape, lambda i: (0, 0)),
                  pl.BlockSpec((CONV_WIDTH, width), lambda i: (0, 0)), vec(),
                  pl.BlockSpec(wcat.shape, lambda i: (0, 0, 0)), vec(), vec(), vec(),
                  pl.BlockSpec((bsz, width), lambda i: (0, 0))],
        out_specs=[pl.BlockSpec((bsz, tt, width), lambda i: (0, tix(i), 0)),
                   pl.BlockSpec((bsz, width), lambda i: (0, 0))],
        out_shape=[jax.ShapeDtypeStruct((bsz, seq, width), F32),
                   jax.ShapeDtypeStruct((bsz, width), F32)],
        scratch_shapes=[pltpu.VMEM((bsz, tt + 2 * LRU_HALO, width), BF16), pltpu.VMEM((bsz, tt, width), F32),
                        slab(), slab(), slab(), pltpu.VMEM((bsz, width), F32)],
        compiler_params=_cparams("arbitrary"),
        name="lru_scan_bwd" if reverse else "lru_scan_fwd",
    )(p3, p3, p3, shift, conv_w, conv_b.reshape(1, width), wcat, ba.reshape(1, width),
      bx.reshape(1, width), lam.reshape(1, width), h0)


def _rope(x, cos, sin):
    return x * cos + pltpu.roll(x, HEAD_DIM // 2, axis=1) * sin


KV_OUT_WIDTH = 3 * KV_HEADS * HEAD_DIM
PV_SPLITS = 4


def _attn_in_kernel(x_ref, mod_ref, g_ref, wq_ref, wkv_ref, qg_ref, kg_ref, *rest, rope):
    if rope:
        cos_ref, sin_ref, q_ref, kv_ref = rest
    else:
        q_ref, kv_ref = rest
    h = _adaln(x_ref[...], g_ref[...], mod_ref[0:1, :], mod_ref[1:2, :]).astype(BF16)

    def head(xh, gain, scale):
        xh = _rms(xh, gain)
        if rope:
            xh = _rope(xh, cos_ref[...], sin_ref[...])
        return (xh * scale).astype(BF16)

    group_w = GROUP * HEAD_DIM
    for n0 in range(0, Q_HEADS * HEAD_DIM, group_w):
        res = _dot(h, wq_ref[:, n0:n0 + group_w])
        for j in range(GROUP):
            sl = slice(j * HEAD_DIM, (j + 1) * HEAD_DIM)
            q_ref[:, n0 + j * HEAD_DIM:n0 + (j + 1) * HEAD_DIM] = head(res[:, sl], qg_ref[...], HEAD_DIM ** -0.5)
    kv = _dot(h, wkv_ref[...])
    kw = KV_HEADS * HEAD_DIM
    ones = jnp.ones((kv.shape[0], HEAD_DIM), BF16)
    for j in range(KV_HEADS):
        sl = slice(j * HEAD_DIM, (j + 1) * HEAD_DIM)
        kv_ref[:, sl] = head(kv[:, sl], kg_ref[...], 1.0)
        v0 = kw + 2 * j * HEAD_DIM
        kv_ref[:, v0:v0 + HEAD_DIM] = kv[:, kw + j * HEAD_DIM:kw + (j + 1) * HEAD_DIM].astype(BF16)
        kv_ref[:, v0 + HEAD_DIM:v0 + 2 * HEAD_DIM] = ones


def _attn_in(x2d, mod, layer, mod_index, gain, w_q, w_kv, q_g, k_g, tm, rope_tables, seq, name):
    rows, d = x2d.shape
    vec = lambda: pl.BlockSpec((1, HEAD_DIM), lambda i: (0, 0))
    in_specs = [pl.BlockSpec((tm, d), lambda i: (i, 0)), _mod_spec(layer, d, mod_index),
                pl.BlockSpec((1, d), lambda i: (0, 0)),
                pl.BlockSpec(w_q.shape, lambda i: (0, 0)), pl.BlockSpec(w_kv.shape, lambda i: (0, 0)), vec(), vec()]
    args = [x2d, mod, gain.reshape(1, d), w_q, w_kv, q_g.reshape(1, HEAD_DIM), k_g.reshape(1, HEAD_DIM)]
    if rope_tables is not None:
        tpb = seq // tm
        in_specs += [pl.BlockSpec((tm, HEAD_DIM), lambda i: (i % tpb, 0))] * 2
        args += list(rope_tables)
    return pl.pallas_call(
        functools.partial(_attn_in_kernel, rope=rope_tables is not None),
        grid=(rows // tm,),
        in_specs=in_specs,
        out_specs=[pl.BlockSpec((tm, w_q.shape[1]), lambda i: (i, 0)),
                   pl.BlockSpec((tm, KV_OUT_WIDTH), lambda i: (i, 0))],
        out_shape=[jax.ShapeDtypeStruct((rows, w_q.shape[1]), BF16),
                   jax.ShapeDtypeStruct((rows, KV_OUT_WIDTH), BF16)],
        compiler_params=_cparams("arbitrary"),
        name=name,
    )(*args)


def _attn_kernel(q_ref, kvc_ref, *rest, has_latent, tk):
    if has_latent:
        kvl_ref, o0_ref, o1_ref, s0_sc, s1_sc, p0_sc, p1_sc, m1_sc = rest
    else:
        o0_ref, o1_ref, s0_sc, s1_sc, p0_sc, p1_sc, m1_sc = rest
    tq = q_ref.shape[0]
    rows = GROUP * tq
    kw = KV_HEADS * HEAD_DIM
    lanes = HEAD_DIM
    n_ctx = kvc_ref.shape[0]
    n_lat = kvl_ref.shape[0] if has_latent else 0

    @pl.when((pl.program_id(0) == 0) & (pl.program_id(1) == 0))
    def _():
        s1_sc[...] = jnp.zeros_like(s1_sc)
        m1_sc[...] = jnp.zeros_like(m1_sc)

    def lane_groups(s):
        return [s[:, c * lanes:(c + 1) * lanes] for c in range(s.shape[1] // lanes)]

    tkc = min(tk, n_ctx)
    tiles = [(kvc_ref, r0, tkc, r0) for r0 in range(0, n_ctx, tkc)]
    tiles += [(kvl_ref, r0, tk, n_ctx + r0) for r0 in range(0, n_lat, tk)]
    pv_parts = [(kvc_ref, 0, n_ctx, 0)]
    pv_parts += [(kvl_ref, r0, n_lat // PV_SPLITS, n_ctx + r0) for r0 in range(0, n_lat, max(n_lat // PV_SPLITS, 1))]

    def stage(hk_new, s_new_sc, hk_old, s_old_sc, m_old, p_sc, o_ref):
        ksl = slice(hk_new * HEAD_DIM, (hk_new + 1) * HEAD_DIM)
        vsl = slice(kw + 2 * hk_old * HEAD_DIM, kw + 2 * (hk_old + 1) * HEAD_DIM)
        qcat = jnp.concatenate([q_ref[:, (hk_new * GROUP + gi) * HEAD_DIM:(hk_new * GROUP + gi + 1) * HEAD_DIM]
                                for gi in range(GROUP)], axis=0)
        mrun = jnp.full((rows, lanes), -jnp.inf, F32)
        partial, pending = [], list(pv_parts)
        for kv_ref, r0, w, c0 in tiles:
            s = _dot_nt(qcat, kv_ref[r0:r0 + w, ksl])
            s_new_sc[:, c0:c0 + w] = s
            mrun = functools.reduce(jnp.maximum, [mrun] + lane_groups(s))
            p = [jnp.exp(sg - m_old) for sg in lane_groups(s_old_sc[:, c0:c0 + w])]
            p_sc[:, c0:c0 + w] = jnp.concatenate(p, axis=1).astype(BF16)
            while pending and pending[0][3] + pending[0][2] <= c0 + w:
                pv_ref, pr0, pw, pc0 = pending.pop(0)
                partial.append(_dot(p_sc[:, pc0:pc0 + pw], pv_ref[pr0:pr0 + pw, vsl]))
        acc = functools.reduce(jnp.add, partial)
        o = acc[:, 0:HEAD_DIM] / acc[:, HEAD_DIM:HEAD_DIM + 1]
        for gi in range(GROUP):
            o_ref[:, gi * HEAD_DIM:(gi + 1) * HEAD_DIM] = o[gi * tq:(gi + 1) * tq].astype(o_ref.dtype)
        return jnp.broadcast_to(jnp.max(mrun, axis=-1, keepdims=True), (rows, lanes))

    m0 = stage(0, s0_sc, 1, s1_sc, m1_sc[...], p1_sc, o1_ref)
    m1_sc[...] = stage(1, s1_sc, 0, s0_sc, m0, p0_sc, o0_ref)


def _attention(q, kv_ctx, kv_lat, bsz, tq, tk):
    rows = q.shape[0]
    nq = rows // bsz // tq
    n_ctx = kv_ctx.shape[0] // bsz
    has_latent = kv_lat is not None
    n_keys = n_ctx + (kv_lat.shape[0] // bsz if has_latent else 0)
    assert n_ctx % min(tk, n_ctx) == 0 and (n_keys - n_ctx) % (tk * PV_SPLITS) == 0
    in_specs = [pl.BlockSpec((tq, q.shape[1]), lambda b, j: (b * nq + jnp.minimum(j, nq - 1), 0)),
                pl.BlockSpec((n_ctx, kv_ctx.shape[1]), lambda b, j: (b, 0))]
    args = [q, kv_ctx]
    if has_latent:
        in_specs.append(pl.BlockSpec((n_keys - n_ctx, kv_lat.shape[1]), lambda b, j: (b, 0)))
        args.append(kv_lat)
    group_w = GROUP * HEAD_DIM
    scores = lambda: pltpu.VMEM((GROUP * tq, n_keys), F32)
    probs = lambda: pltpu.VMEM((GROUP * tq, n_keys), BF16)
    return pl.pallas_call(
        functools.partial(_attn_kernel, has_latent=has_latent, tk=tk),
        grid=(bsz, nq + 1),
        in_specs=in_specs,
        out_specs=[pl.BlockSpec((tq, group_w), lambda b, j: (b * nq + jnp.minimum(j, nq - 1), 0)),
                   pl.BlockSpec((tq, group_w), lambda b, j: (b * nq + jnp.maximum(j - 1, 0), 0))],
        out_shape=[jax.ShapeDtypeStruct((rows, group_w), BF16), jax.ShapeDtypeStruct((rows, group_w), BF16)],
        scratch_shapes=[scores(), scores(), probs(), probs(), pltpu.VMEM((GROUP * tq, HEAD_DIM), F32)],
        compiler_params=_cparams("arbitrary", "arbitrary"),
        name="attention" if has_latent else "attention_ctx",
    )(*args)


def _rope_tables(seq):
    n_rows = seq // GRID_W
    row = jnp.repeat(jnp.arange(n_rows), GRID_W)
    col = jnp.tile(jnp.arange(GRID_W), n_rows)
    n_pairs_axis = HEAD_DIM // 4
    inv_freq = ROPE_THETA ** (-jnp.arange(n_pairs_axis, dtype=F32) / n_pairs_axis)
    ang = jnp.concatenate([row[:, None] * inv_freq, col[:, None] * inv_freq], axis=-1)
    cos, sin = jnp.cos(ang), jnp.sin(ang)
    return jnp.concatenate([cos, cos], axis=-1), jnp.concatenate([-sin, sin], axis=-1)


def _tile(n, pref):
    t = min(n, pref)
    assert n % t == 0, (n, pref)
    return t


def kernel(x, c, ctx, c_ctx, norm_mix_g, norm_mlp_g, w_mod, b_mod, w_mlp1, w_mlp2, gla_w_in, gla_w_up_f, gla_b_f, gla_w_up_b, gla_b_b, gla_norm_g, gla_w_o, lru_w_in, lru_conv_w, lru_conv_b, lru_wa_f, lru_ba_f, lru_wx_f, lru_bx_f, lru_lam_f, lru_wa_b, lru_ba_b, lru_wx_b, lru_bx_b, lru_lam_b, lru_w_o, attn_w_in, attn_q_g, attn_k_g, attn_w_o, final_g):
    bsz, seq, d = x.shape
    n_ctx = ctx.shape[1]
    depth = w_mod.shape[0]
    assert bsz + 1 <= MOD_ROWS and d == GLA_HEADS * GLA_DV == Q_HEADS * HEAD_DIM

    mod = _modulation(c, c_ctx, w_mod, b_mod)
    xl = x.reshape(bsz * seq, d)
    xc = ctx.reshape(bsz * n_ctx, d)

    tm_l = _tile(seq, 512)
    tm_c = _tile(n_ctx, 256)
    lat_idx = lambda i: 1 + (i * tm_l) // seq
    ctx_idx = lambda i: 0

    for layer in range(depth):
        need_ctx = layer < depth - 1
        kind, j = layer % N_MIXERS, layer // N_MIXERS
        gmix = norm_mix_g[layer]

        if kind == 0:
            dq = GLA_HEADS * GLA_DK
            main_w = gla_w_in[j][:, :2 * dq + 2 * d].astype(BF16)
            gate_w = jnp.pad(gla_w_in[j][:, 2 * dq + 2 * d:], ((0, 0), (0, GLA_GATE_PAD - 2 * GLA_GATE_RANK))).astype(BF16)
            zpad = jnp.zeros((GLA_GATE_PAD - 2 * GLA_GATE_RANK, dq), F32)
            wup_f = jnp.concatenate([gla_w_up_f[j], jnp.zeros_like(gla_w_up_b[j]), zpad], axis=0).astype(BF16)
            wup_b = jnp.concatenate([jnp.zeros_like(gla_w_up_f[j]), gla_w_up_b[j], zpad], axis=0).astype(BF16)
            bup_f, bup_b = gla_b_f[j].reshape(1, dq), gla_b_b[j].reshape(1, dq)
            w_o = gla_w_o[j].astype(BF16)
            proj = {}
            for name, (xs, midx, tm, slen) in zip(("lat", "ctx"), [(xl, lat_idx, tm_l, seq), (xc, ctx_idx, tm_c, n_ctx)]):
                proj[name] = _ln_matmul(xs, mod, layer, midx, gmix, [main_w, gate_w], [BF16, BF16], tm,
                                        name="gla_in_" + name)
            s0 = jnp.zeros((bsz, GLA_HEADS, GLA_DK, GLA_DV), F32)
            ch_c, ch_l = _tile(n_ctx, 128), _tile(seq, 128)
            ocf, ocb, s_cf, s_cb = _gla_scan(*proj["ctx"], wup_f, wup_b, bup_f, bup_b, s0, s0, bsz, ch_c)
            olf, olb, _, _ = _gla_scan(*proj["lat"], wup_f, wup_b, bup_f, bup_b, s_cf, s_cb, bsz, ch_l)
            ng = gla_norm_g[j].reshape(1, GLA_DV)

            def gla_out(of, ob, qkvr, xs, midx, tm):
                blk = lambda i: (i, 0)
                return _proj_res("gla", [of, ob, qkvr], [pl.BlockSpec((tm, d), blk), pl.BlockSpec((tm, d), blk),
                                                        pl.BlockSpec((tm, d), lambda i: (i, (2 * dq + d) // d))],
                                 w_o, xs, mod, layer, midx, tm, extra=[ng],
                                 extra_specs=[pl.BlockSpec((1, GLA_DV), lambda i: (0, 0))])

            xl = gla_out(olf, olb, proj["lat"][0], xl, lat_idx, tm_l)
            if need_ctx:
                xc = gla_out(ocf, ocb, proj["ctx"][0], xc, ctx_idx, tm_c)

        elif kind == 1:
            w_in = lru_w_in[j].astype(BF16)
            w_o = lru_w_o[j].astype(BF16)
            wcat_f = jnp.concatenate([lru_wa_f[j], lru_wx_f[j]], axis=-1).astype(BF16)
            wcat_b = jnp.concatenate([lru_wa_b[j], lru_wx_b[j]], axis=-1).astype(BF16)
            p_l, = _ln_matmul(xl, mod, layer, lat_idx, gmix, [w_in], [BF16], tm_l, name="lru_in_lat")
            p_c, = _ln_matmul(xc, mod, layer, ctx_idx, gmix, [w_in], [BF16], tm_c, name="lru_in_ctx")
            h0 = jnp.zeros((bsz, d), F32)
            fwd = (lru_conv_w[j], lru_conv_b[j], wcat_f, lru_ba_f[j], lru_bx_f[j], lru_lam_f[j])
            bwd = (lru_conv_w[j], lru_conv_b[j], wcat_b, lru_ba_b[j], lru_bx_b[j], lru_lam_b[j])
            tt_c, tt_l = _tile(n_ctx, 128), _tile(seq, 128)
            p_c3, p_l3 = p_c.reshape(bsz, n_ctx, 2 * d), p_l.reshape(bsz, seq, 2 * d)
            hcf, s_cf = _lru_scan(p_c3, *fwd, h0, tt_c, False)
            hcb, s_cb = _lru_scan(p_c3, *bwd, h0, tt_c, True)
            hlf, _ = _lru_scan(p_l3, *fwd, s_cf, tt_l, False)
            hlb, _ = _lru_scan(p_l3, *bwd, s_cb, tt_l, True)

            def lru_out(hf, hb, p2d, xs, midx, tm):
                blk = lambda i: (i, 0)
                spec = lambda: pl.BlockSpec((tm, d), blk)
                return _proj_res("lru", [hf.reshape(-1, d), hb.reshape(-1, d), p2d], [spec(), spec(), spec()],
                                 w_o, xs, mod, layer, midx, tm)

            xl = lru_out(hlf, hlb, p_l, xl, lat_idx, tm_l)
            if need_ctx:
                xc = lru_out(hcf, hcb, p_c, xc, ctx_idx, tm_c)

        else:
            qw = Q_HEADS * HEAD_DIM
            w_q = attn_w_in[j][:, :qw].astype(BF16)
            w_kv = attn_w_in[j][:, qw:].astype(BF16)
            w_o = attn_w_o[j].astype(BF16)
            qg, kg = attn_q_g[j], attn_k_g[j]
            q_l, kv_l = _attn_in(xl, mod, layer, lat_idx, gmix, w_q, w_kv, qg, kg, tm_l, _rope_tables(seq), seq,
                                 "attn_in_lat")
            q_c, kv_c = _attn_in(xc, mod, layer, ctx_idx, gmix, w_q, w_kv, qg, kg, tm_c, None, n_ctx, "attn_in_ctx")
            tk = _tile(seq // PV_SPLITS, 512)
            o_l = _attention(q_l, kv_c, kv_l, bsz, _tile(seq, 128), tk)
            half = lambda tm: [pl.BlockSpec((tm, d // KV_HEADS), lambda i: (i, 0))] * KV_HEADS
            xl = _proj_res("plain", o_l, half(tm_l), w_o, xl, mod, layer, lat_idx, tm_l)
            if need_ctx:
                o_c = _attention(q_c, kv_c, None, bsz, _tile(n_ctx, 128), tk)
                xc = _proj_res("plain", o_c, half(tm_c), w_o, xc, mod, layer, ctx_idx, tm_c)

        w1, w2 = w_mlp1[layer].astype(BF16), w_mlp2[layer].astype(BF16)
        tf = _tile(w1.shape[1], 1024)
        tmm_l = _tile(seq, 1024)
        xl = _mlp(xl, mod, layer, lambda i: 1 + (i * tmm_l) // seq, norm_mlp_g[layer], w1, w2, tmm_l, tf,
                  final_gain=None if need_ctx else final_g)
        if need_ctx:
            xc = _mlp(xc, mod, layer, ctx_idx, norm_mlp_g[layer], w1, w2, _tile(bsz * n_ctx, 1024), tf)

    return xl.reshape(bsz, seq, d)
```

```python
import functools

import numpy as np
import jax
import jax.numpy as jnp
from jax import lax
from jax.experimental import pallas as pl
from jax.experimental.pallas import tpu as pltpu

F32 = jnp.float32
BF16 = jnp.bfloat16

EPS = 1e-6
N_MOD = 6
N_MIXERS = 3
MOD_ROWS = 16

GLA_HEADS = 4
GLA_DK = 128
GLA_DV = 256
GLA_GATE_RANK = 16
GLA_GATE_TAU = 16.0
GLA_GATE_PAD = 128

RNN_BLOCKS = 8
RNN_BLOCK_DIM = 128
CONV_WIDTH = 4
CONV_LEFT = 2
LRU_C = 8.0

HEAD_DIM = 128
Q_HEADS = 8
KV_HEADS = 2
GROUP = Q_HEADS // KV_HEADS
GRID_W = 64
ROPE_THETA = 10000.0

VMEM_LIMIT_BYTES = 56 * 1024 * 1024


def _cparams(*sem):
    return pltpu.CompilerParams(dimension_semantics=sem, vmem_limit_bytes=VMEM_LIMIT_BYTES)


def _dot(a, b):
    return jnp.dot(a, b, preferred_element_type=F32)


def _dot_nt(a, b):
    return lax.dot_general(a, b, (((1,), (1,)), ((), ())), preferred_element_type=F32)


def _dot_tn(a, b):
    return lax.dot_general(a, b, (((0,), (0,)), ((), ())), preferred_element_type=F32)


def _rms(x, g):
    return x * lax.rsqrt(jnp.mean(x * x, axis=-1, keepdims=True) + EPS) * g


def _adaln(x, g, shift, scale):
    return _rms(x, g) * (1.0 + scale) + shift


def _mod_kernel(s_ref, w_ref, b_ref, o_ref):
    s = s_ref[...]
    s = s * jax.nn.sigmoid(s)
    o_ref[...] = _dot(s.astype(BF16), w_ref[...].astype(BF16)) + b_ref[...]


def _modulation(c, c_ctx, w_mod, b_mod):
    depth, d, nd = w_mod.shape
    bsz = c.shape[0]
    rows = jnp.concatenate([c_ctx[None, :], c, jnp.zeros((MOD_ROWS - 1 - bsz, d), F32)], axis=0)
    tn = min(nd, 1536)
    out = pl.pallas_call(
        _mod_kernel,
        grid=(depth, nd // tn),
        in_specs=[pl.BlockSpec((MOD_ROWS, d), lambda l, j: (0, 0)),
                  pl.BlockSpec((None, d, tn), lambda l, j: (l, 0, j)),
                  pl.BlockSpec((None, 1, tn), lambda l, j: (l, 0, j))],
        out_specs=pl.BlockSpec((None, MOD_ROWS, tn), lambda l, j: (l, 0, j)),
        out_shape=jax.ShapeDtypeStruct((depth, MOD_ROWS, nd), F32),
        compiler_params=_cparams("arbitrary", "arbitrary"),
        name="modulation",
    )(rows, w_mod, b_mod.reshape(depth, 1, nd))
    return out.reshape(depth, MOD_ROWS, N_MOD, d)


def _mod_spec(layer, d, mod_index):
    return pl.BlockSpec((None, None, N_MOD, d), lambda i, *_: (layer, mod_index(i), 0, 0))


def _ln_matmul_kernel(x_ref, mod_ref, g_ref, *refs, n_out, mod_off, col_chunk):
    w_refs, o_refs = refs[:n_out], refs[n_out:]
    h = _adaln(x_ref[...], g_ref[...], mod_ref[mod_off:mod_off + 1, :],
               mod_ref[mod_off + 1:mod_off + 2, :]).astype(BF16)
    for w_ref, o_ref in zip(w_refs, o_refs):
        n = w_ref.shape[1]
        for n0 in range(0, n, col_chunk):
            n1 = min(n, n0 + col_chunk)
            o_ref[:, n0:n1] = _dot(h, w_ref[:, n0:n1]).astype(o_ref.dtype)


def _ln_matmul(x2d, mod, layer, mod_index, gain, ws, out_dtypes, tm, name="ln_matmul"):
    rows, d = x2d.shape
    in_specs = [pl.BlockSpec((tm, d), lambda i: (i, 0)), _mod_spec(layer, d, mod_index),
                pl.BlockSpec((1, d), lambda i: (0, 0))]
    in_specs += [pl.BlockSpec(w.shape, lambda i: (0, 0)) for w in ws]
    return pl.pallas_call(
        functools.partial(_ln_matmul_kernel, n_out=len(ws), mod_off=0, col_chunk=512),
        grid=(rows // tm,),
        in_specs=in_specs,
        out_specs=[pl.BlockSpec((tm, w.shape[1]), lambda i: (i, 0)) for w in ws],
        out_shape=[jax.ShapeDtypeStruct((rows, w.shape[1]), dt) for w, dt in zip(ws, out_dtypes)],
        compiler_params=_cparams("arbitrary"),
        name=name,
    )(x2d, mod, gain.reshape(1, d), *ws)


def _proj_res_kernel(y0_ref, y1_ref, w_ref, x_ref, mod_ref, o_ref):
    y = jnp.concatenate([y0_ref[...], y1_ref[...]], axis=1)
    o_ref[...] = x_ref[...] + mod_ref[2:3, :] * _dot(y, w_ref[...])


def _gla_proj_res_kernel(of_ref, ob_ref, r_ref, ng_ref, w_ref, x_ref, mod_ref, o_ref):
    o = of_ref[...].astype(F32) + ob_ref[...].astype(F32)
    r = r_ref[...].astype(F32)
    ys = []
    for h in range(GLA_HEADS):
        sl = slice(h * GLA_DV, (h + 1) * GLA_DV)
        rh = r[:, sl]
        ys.append(_rms(o[:, sl], ng_ref[...]) * (rh * jax.nn.sigmoid(rh)))
    y = jnp.concatenate(ys, axis=1).astype(BF16)
    o_ref[...] = x_ref[...] + mod_ref[2:3, :] * _dot(y, w_ref[...])


def _lru_proj_res_kernel(hf_ref, hb_ref, gate_ref, w_ref, x_ref, mod_ref, o_ref):
    h = hf_ref[...].astype(F32) + hb_ref[...].astype(F32)
    y = (h * jax.nn.gelu(gate_ref[...].astype(F32))).astype(BF16)
    o_ref[...] = x_ref[...] + mod_ref[2:3, :] * _dot(y, w_ref[...])


def _proj_res(kind, operands, operand_specs, w, x2d, mod, layer, mod_index, tm, extra=(), extra_specs=()):
    rows, d = x2d.shape
    body = {"plain": _proj_res_kernel, "gla": _gla_proj_res_kernel, "lru": _lru_proj_res_kernel}[kind]
    n_in = len(operands) + len(extra) + 1
    return pl.pallas_call(
        body,
        grid=(rows // tm,),
        in_specs=[*operand_specs, *extra_specs, pl.BlockSpec(w.shape, lambda i: (0, 0)),
                  pl.BlockSpec((tm, d), lambda i: (i, 0)), _mod_spec(layer, d, mod_index)],
        out_specs=pl.BlockSpec((tm, d), lambda i: (i, 0)),
        out_shape=jax.ShapeDtypeStruct((rows, d), F32),
        input_output_aliases={n_in: 0} if layer > 0 else {},
        compiler_params=_cparams("arbitrary"),
        name=kind + "_proj_res",
    )(*operands, *extra, w, x2d, mod)


def _mlp_kernel(x_ref, mod_ref, g_ref, w1_ref, w2_ref, *rest, final_norm):
    if final_norm:
        fg_ref, o_ref, h_sc, acc_sc = rest
    else:
        o_ref, h_sc, acc_sc = rest
    j = pl.program_id(1)

    @pl.when(j == 0)
    def _():
        h_sc[...] = _adaln(x_ref[...], g_ref[...], mod_ref[3:4, :], mod_ref[4:5, :]).astype(BF16)
        acc_sc[...] = jnp.zeros_like(acc_sc)

    a = jnp.maximum(_dot(h_sc[...], w1_ref[...]), 0.0)
    acc_sc[...] += _dot((a * a).astype(BF16), w2_ref[...])

    @pl.when(j == pl.num_programs(1) - 1)
    def _():
        y = x_ref[...] + mod_ref[5:6, :] * acc_sc[...]
        o_ref[...] = _rms(y, fg_ref[...]) if final_norm else y


def _mlp(x2d, mod, layer, mod_index, gain, w1, w2, tm, tf, final_gain=None):
    rows, d = x2d.shape
    dff = w1.shape[1]
    vec = lambda: pl.BlockSpec((1, d), lambda i, j: (0, 0))
    in_specs = [pl.BlockSpec((tm, d), lambda i, j: (i, 0)), _mod_spec(layer, d, mod_index), vec(),
                pl.BlockSpec((d, tf), lambda i, j: (0, j)),
                pl.BlockSpec((tf, d), lambda i, j: (j, 0))]
    args = [x2d, mod, gain.reshape(1, d), w1, w2]
    if final_gain is not None:
        in_specs.append(vec())
        args.append(final_gain.reshape(1, d))
    return pl.pallas_call(
        functools.partial(_mlp_kernel, final_norm=final_gain is not None),
        grid=(rows // tm, dff // tf),
        in_specs=in_specs,
        out_specs=pl.BlockSpec((tm, d), lambda i, j: (i, 0)),
        out_shape=jax.ShapeDtypeStruct((rows, d), F32),
        scratch_shapes=[pltpu.VMEM((tm, d), BF16), pltpu.VMEM((tm, d), F32)],
        input_output_aliases={0: 0},
        compiler_params=_cparams("arbitrary", "arbitrary"),
        name="mlp",
    )(*args)


GLA_MATMUL_LEVEL_ROWS = 8
GLA_FAST_MAX_DECAY = 60.0


def _gla_tables(chunk, reverse):
    levels = int(np.log2(chunk))
    idx = np.arange(chunk)
    t, r = idx[:, None], idx[None, :]
    if not reverse:
        q_incl = (r <= t)
        k_rest = (r > t)
    else:
        q_incl = (r >= t)
        k_rest = (r < t)
    mats = [q_incl, k_rest]
    masks = [np.eye(chunk, dtype=bool)]
    for lv in range(1, levels + 1):
        m = 1 << (lv - 1)
        mid = (idx // (2 * m)) * (2 * m) + m
        upper = idx >= mid
        midc = mid[:, None]
        if not reverse:
            pat = np.where(upper[:, None], (r >= midc) & (r <= t), (r > t) & (r < midc))
            mask = (upper[:, None] & ~upper[None, :])
        else:
            pat = np.where(upper[:, None], (r >= midc) & (r < t), (r >= t) & (r < midc))
            mask = (~upper[:, None] & upper[None, :])
        mask = mask & ((idx[:, None] // (2 * m)) == (idx[None, :] // (2 * m)))
        if 2 * m <= GLA_MATMUL_LEVEL_ROWS:
            mats.append(pat)
        masks.append(mask)
    a = np.concatenate(mats, axis=0).astype(np.float32)
    a2 = np.concatenate([a, a], axis=1)
    return jnp.asarray(a2, BF16), jnp.asarray(np.stack(masks).astype(np.float32), F32)


def _gla_gates(qkvr_ref, gf_ref, wup_ref, bup_ref, a_ref, reverse):
    chunk = qkvr_ref.shape[0]
    dq = GLA_HEADS * GLA_DK
    graw = _dot(gf_ref[...], wup_ref[...]) + bup_ref[...]
    g = (jnp.minimum(graw, 0.0) - jnp.log(1.0 + jnp.exp(-jnp.abs(graw)))) * (1.0 / GLA_GATE_TAU)
    g_hi = g.astype(BF16)
    g_cat = jnp.concatenate([g_hi, (g - g_hi.astype(F32)).astype(BF16)], axis=0)

    def partial_sums(block):
        return _dot(a_ref[block * chunk:(block + 1) * chunk, :], g_cat)

    st = dict(partial_sums=partial_sums, bq=partial_sums(0), bk=partial_sums(1), reverse=reverse, chunk=chunk,
              q_all=qkvr_ref[:, 0:dq].astype(F32) * (GLA_DK ** -0.5), k_all=qkvr_ref[:, dq:2 * dq].astype(F32))
    st["qd"] = (st["q_all"] * jnp.exp(st["bq"])).astype(BF16)
    st["kd"] = (st["k_all"] * jnp.exp(st["bk"])).astype(BF16)
    st["btot"] = st["bq"][0:1, :] if reverse else st["bq"][chunk - 1:chunk, :]
    return st


def _gla_levels(st, z_sc, lv_range):
    chunk, reverse, bq, bk = st["chunk"], st["reverse"], st["bq"], st["bk"]
    q_all, k_all = st["q_all"], st["k_all"]
    row = lax.broadcasted_iota(jnp.int32, (chunk, 1), 0)
    for lv in lv_range:
        m = 1 << (lv - 1)
        if 2 * m <= GLA_MATMUL_LEVEL_ROWS:
            use_q = ((row // m) % 2) == (0 if reverse else 1)
            z = jnp.where(use_q, q_all, k_all) * jnp.exp(st["partial_sums"](1 + lv))
        else:
            pieces = []
            for lo in range(0, chunk, 2 * m):
                mid, hi = lo + m, lo + 2 * m
                if not reverse:
                    pieces.append(k_all[lo:mid] * jnp.exp(bk[lo:mid] - bk[mid - 1:mid]))
                    pieces.append(q_all[mid:hi] * jnp.exp(bq[mid:hi] - bq[mid - 1:mid]))
                else:
                    pieces.append(q_all[lo:mid] * jnp.exp(bq[lo:mid] - bq[mid:mid + 1]))
                    pieces.append(k_all[mid:hi] * jnp.exp(bk[mid:hi] - bk[mid:mid + 1]))
            z = jnp.concatenate(pieces, axis=0)
        z_sc[lv - 1] = z.astype(BF16)


def _gla_scores_by_level(st, h, mask_ref, z_sc, levels):
    sl = slice(h * GLA_DK, (h + 1) * GLA_DK)
    chunk, reverse = st["chunk"], st["reverse"]
    sub = 8
    diag = mask_ref[0] * jnp.sum(st["q_all"][:, sl] * st["k_all"][:, sl], axis=-1, keepdims=True)
    tiles = [diag[r:r + sub] for r in range(0, chunk, sub)]
    for lv in range(1, levels + 1):
        m = 1 << (lv - 1)
        if 2 * m <= GLA_MATMUL_LEVEL_ROWS:
            q_rows = [(0, chunk)]
        else:
            q_rows = [(lo, lo + m) if reverse else (lo + m, lo + 2 * m) for lo in range(0, chunk, 2 * m)]
        zq = jnp.concatenate([z_sc[lv - 1, lo:hi, sl] for lo, hi in q_rows], axis=0)
        mq = jnp.concatenate([mask_ref[lv, lo:hi, :] for lo, hi in q_rows], axis=0)
        part = mq * _dot_nt(zq, z_sc[lv - 1, :, sl])
        src = 0
        for lo, hi in q_rows:
            for r in range(lo, hi, sub):
                tiles[r // sub] = tiles[r // sub] + part[src:src + sub]
                src += sub
    return jnp.concatenate(tiles, axis=0)


def _gla_head_update(st, h, scores, qkvr_ref, s_sc, o_ref):
    dq = GLA_HEADS * GLA_DK
    sl = slice(h * GLA_DK, (h + 1) * GLA_DK)
    vh = qkvr_ref[:, 2 * dq + h * GLA_DV:2 * dq + (h + 1) * GLA_DV]
    s_old = s_sc[h]
    o_ref[:, h * GLA_DV:(h + 1) * GLA_DV] = (_dot(scores.astype(BF16), vh)
                                            + _dot(st["qd"][:, sl], s_old.astype(BF16))).astype(o_ref.dtype)
    etot = jnp.exp(st["btot"][:, sl])
    ecol = jnp.transpose(jnp.broadcast_to(etot, (GLA_DK, GLA_DK)))
    s_sc[h] = s_old * jnp.concatenate([ecol] * (GLA_DV // GLA_DK), axis=1) + _dot_tn(st["kd"][:, sl], vh)


def _gla_kernel(qkvr_f, gf_f, qkvr_b, gf_b, wup_f, wup_b, bup_f, bup_b, a_f, a_b, mask_f, mask_b, pair_f, pair_b,
                s0_f, s0_b, o_f, o_b, sfin_f, sfin_b, s_sc_f, s_sc_b, z_sc_f, z_sc_b):
    c = pl.program_id(1)
    levels = mask_f.shape[0] - 1
    dirs = ((qkvr_f, mask_f, pair_f, z_sc_f, s_sc_f, o_f), (qkvr_b, mask_b, pair_b, z_sc_b, s_sc_b, o_b))

    @pl.when(c == 0)
    def _():
        s_sc_f[...] = s0_f[...]
        s_sc_b[...] = s0_b[...]

    sts = (_gla_gates(qkvr_f, gf_f, wup_f, bup_f, a_f, False), _gla_gates(qkvr_b, gf_b, wup_b, bup_b, a_b, True))
    worst = jnp.maximum(jnp.max(jnp.abs(sts[0]["btot"])), jnp.max(jnp.abs(sts[1]["btot"])))
    small_decay = worst <= GLA_FAST_MAX_DECAY

    @pl.when(small_decay)
    def _():
        kinv = [(st["k_all"] * jnp.exp(-st["bq"])).astype(BF16) for st in sts]
        for h in range(GLA_HEADS):
            sl = slice(h * GLA_DK, (h + 1) * GLA_DK)
            for st, kv, (qkvr, _, pair, _, s_sc, o_ref) in zip(sts, kinv, dirs):
                scores = pair[...] * _dot_nt(st["qd"][:, sl], kv[:, sl])
                _gla_head_update(st, h, scores, qkvr, s_sc, o_ref)

    @pl.when(jnp.logical_not(small_decay))
    def _():
        for st, (_, _, _, z_sc, _, _) in zip(sts, dirs):
            _gla_levels(st, z_sc, range(1, levels + 1))
        for h in range(GLA_HEADS):
            for st, (qkvr, mask, _, z_sc, s_sc, o_ref) in zip(sts, dirs):
                _gla_head_update(st, h, _gla_scores_by_level(st, h, mask, z_sc, levels), qkvr, s_sc, o_ref)

    @pl.when(c == pl.num_programs(1) - 1)
    def _():
        sfin_f[...] = s_sc_f[...]
        sfin_b[...] = s_sc_b[...]


def _gla_scan(qkvr, gfeat, wup_f, wup_b, bup_f, bup_b, s0_f, s0_b, bsz, chunk):
    rows = qkvr.shape[0]
    nc = rows // bsz // chunk
    a_f, mask_f = _gla_tables(chunk, False)
    a_b, mask_b = _gla_tables(chunk, True)
    tri = np.tril(np.ones((chunk, chunk), np.float32))
    pair_f, pair_b = jnp.asarray(tri), jnp.asarray(tri.T)
    dv = GLA_HEADS * GLA_DV
    fwd_block = lambda b, c: (b * nc + c, 0)
    bwd_block = lambda b, c: (b * nc + nc - 1 - c, 0)
    const2 = lambda x: pl.BlockSpec(x.shape, lambda b, c: (0, 0))
    const3 = lambda x: pl.BlockSpec(x.shape, lambda b, c: (0, 0, 0))
    state_spec = pl.BlockSpec((None, GLA_HEADS, GLA_DK, GLA_DV), lambda b, c: (b, 0, 0, 0))
    state_shape = jax.ShapeDtypeStruct((bsz, GLA_HEADS, GLA_DK, GLA_DV), F32)
    state_sc = lambda: pltpu.VMEM((GLA_HEADS, GLA_DK, GLA_DV), F32)
    z_sc = lambda: pltpu.VMEM((mask_f.shape[0] - 1, chunk, GLA_HEADS * GLA_DK), BF16)
    return pl.pallas_call(
        _gla_kernel,
        grid=(bsz, nc),
        in_specs=[pl.BlockSpec((chunk, qkvr.shape[1]), fwd_block), pl.BlockSpec((chunk, gfeat.shape[1]), fwd_block),
                  pl.BlockSpec((chunk, qkvr.shape[1]), bwd_block), pl.BlockSpec((chunk, gfeat.shape[1]), bwd_block),
                  const2(wup_f), const2(wup_b), const2(bup_f), const2(bup_b), const2(a_f), const2(a_b),
                  const3(mask_f), const3(mask_b), const2(pair_f), const2(pair_b), state_spec, state_spec],
        out_specs=[pl.BlockSpec((chunk, dv), fwd_block), pl.BlockSpec((chunk, dv), bwd_block), state_spec, state_spec],
        out_shape=[jax.ShapeDtypeStruct((rows, dv), BF16), jax.ShapeDtypeStruct((rows, dv), BF16),
                   state_shape, state_shape],
        scratch_shapes=[state_sc(), state_sc(), z_sc(), z_sc()],
        compiler_params=_cparams("arbitrary", "arbitrary"),
        name="gla_scan",
    )(qkvr, gfeat, qkvr, gfeat, wup_f, wup_b, bup_f, bup_b, a_f, a_b, mask_f, mask_b, pair_f, pair_b, s0_f, s0_b)


LRU_HALO = 16
CONV_SHIFTS = tuple(j - CONV_LEFT for j in range(CONV_WIDTH) if j != CONV_LEFT)
F32_TINY = float(np.finfo(np.float32).tiny)


def _lru_shift_matrix(tt):
    sel = np.zeros((len(CONV_SHIFTS) * tt, tt + 2 * LRU_HALO), np.float32)
    for i, off in enumerate(CONV_SHIFTS):
        sel[i * tt + np.arange(tt), LRU_HALO + np.arange(tt) + off] = 1.0
    return jnp.asarray(sel, BF16)


def _lru_kernel(x_ref, prev_ref, next_ref, shift_ref, cw_ref, cb_ref, wcat_ref, ba_ref, bx_ref, lam_ref, h0_ref,
                hs_ref, hfin_ref, xs_sc, xc_sc, a_sc, u_sc, hs_sc, h_sc, *, reverse):
    i = pl.program_id(0)
    n = pl.num_programs(0)
    tidx = n - 1 - i if reverse else i
    bsz, tt, width = x_ref.shape
    pitch = a_sc.shape[1] // bsz
    bd = RNN_BLOCK_DIM

    @pl.when(i == 0)
    def _():
        h_sc[...] = h0_ref[...]

    xs_sc[:, 0:LRU_HALO, :] = jnp.where(tidx > 0, prev_ref[...], jnp.zeros_like(prev_ref))
    xs_sc[:, LRU_HALO:LRU_HALO + tt, :] = x_ref[...]
    xs_sc[:, LRU_HALO + tt:2 * LRU_HALO + tt, :] = jnp.where(tidx < n - 1, next_ref[...], jnp.zeros_like(next_ref))
    for b in range(bsz):
        shifted = _dot(shift_ref[...], xs_sc[b])
        xc = cb_ref[...] + cw_ref[CONV_LEFT:CONV_LEFT + 1, :] * x_ref[b].astype(F32)
        for k, off in enumerate(CONV_SHIFTS):
            j = off + CONV_LEFT
            xc = xc + cw_ref[j:j + 1, :] * shifted[k * tt:(k + 1) * tt]
        xc_sc[b] = xc

    lam = lam_ref[...]
    neg4sp = -0.5 * LRU_C * (jnp.maximum(-lam, 0.0) + jnp.log1p(jnp.exp(-jnp.abs(lam))))
    for nb in range(RNN_BLOCKS):
        sl = slice(nb * bd, (nb + 1) * bd)
        xc2 = xc_sc[:, :, sl].reshape(bsz * tt, bd)
        ri = _dot(xc2.astype(BF16), wcat_ref[nb])
        log_a = neg4sp[:, sl] + neg4sp[:, sl] * jnp.tanh(0.5 * (ri[:, :bd] + ba_ref[:, sl]))
        ig = 0.5 + 0.5 * jnp.tanh(0.5 * (ri[:, bd:] + bx_ref[:, sl]))
        th = jnp.tanh(log_a)
        one_minus_a2 = (-2.0 * th) / (1.0 - th)
        root = one_minus_a2 * lax.rsqrt(jnp.maximum(one_minus_a2, F32_TINY))
        a = jnp.exp(log_a)
        u = root * (ig * xc2)
        for b in range(bsz):
            a_sc[nb, b * pitch:b * pitch + tt, :] = a[b * tt:(b + 1) * tt]
            u_sc[nb, b * pitch:b * pitch + tt, :] = u[b * tt:(b + 1) * tt]

    def step(k, hs):
        t = tt - 1 - k if reverse else k
        rows = pl.ds(t, bsz, stride=pitch)
        new = []
        for nb in range(RNN_BLOCKS):
            h = a_sc[nb, rows, :] * hs[nb] + u_sc[nb, rows, :]
            hs_sc[nb, rows, :] = h
            new.append(h)
        return tuple(new)

    h_init = tuple(h_sc[:, nb * bd:(nb + 1) * bd] for nb in range(RNN_BLOCKS))
    h_last = lax.fori_loop(0, tt, step, h_init, unroll=4)
    for nb in range(RNN_BLOCKS):
        sl = slice(nb * bd, (nb + 1) * bd)
        h_sc[:, sl] = h_last[nb]
        for b in range(bsz):
            hs_ref[b, :, sl] = hs_sc[nb, b * pitch:b * pitch + tt, :].astype(hs_ref.dtype)

    @pl.when(i == n - 1)
    def _():
        hfin_ref[...] = h_sc[...]


def _lru_scan(p3, conv_w, conv_b, wcat, ba, bx, lam, h0, tt, reverse):
    bsz, seq, two_w = p3.shape
    width = two_w // 2
    n = seq // tt
    hb = tt // LRU_HALO
    pitch = tt + 8
    assert tt % LRU_HALO == 0 and width == RNN_BLOCKS * RNN_BLOCK_DIM
    shift = _lru_shift_matrix(tt)

    def tix(i):
        return n - 1 - i if reverse else i

    vec = lambda: pl.BlockSpec((1, width), lambda i: (0, 0))
    slab = lambda: pltpu.VMEM((RNN_BLOCKS, bsz * pitch, RNN_BLOCK_DIM), F32)
    return pl.pallas_call(
        functools.partial(_lru_kernel, reverse=reverse),
        grid=(n,),
        in_specs=[pl.BlockSpec((bsz, tt, width), lambda i: (0, tix(i), 1)),
                  pl.BlockSpec((bsz, LRU_HALO, width), lambda i: (0, jnp.maximum(tix(i) * hb - 1, 0), 1)),
                  pl.BlockSpec((bsz, LRU_HALO, width), lambda i: (0, jnp.minimum((tix(i) + 1) * hb, seq // LRU_HALO - 1), 1)),
                  pl.BlockSpec(shift.shape, lambda i: (0, 0)),
                  pl.BlockSpec((CONV_WIDTH, width), lambda i: (0, 0)), vec(),
                  pl.BlockSpec(wcat.shape, lambda i: (0, 0, 0)), vec(), vec(), vec(),
                  pl.BlockSpec((bsz, width), lambda i: (0, 0))],
        out_specs=[pl.BlockSpec((bsz, tt, width), lambda i: (0, tix(i), 0)),
                   pl.BlockSpec((bsz, width), lambda i: (0, 0))],
        out_shape=[jax.ShapeDtypeStruct((bsz, seq, width), BF16),
                   jax.ShapeDtypeStruct((bsz, width), F32)],
        scratch_shapes=[pltpu.VMEM((bsz, tt + 2 * LRU_HALO, width), BF16), pltpu.VMEM((bsz, tt, width), F32),
                        slab(), slab(), slab(), pltpu.VMEM((bsz, width), F32)],
        compiler_params=_cparams("arbitrary"),
        name="lru_scan_bwd" if reverse else "lru_scan_fwd",
    )(p3, p3, p3, shift, conv_w, conv_b.reshape(1, width), wcat, ba.reshape(1, width),
      bx.reshape(1, width), lam.reshape(1, width), h0)


def _rope(x, cos, sin):
    return x * cos + pltpu.roll(x, HEAD_DIM // 2, axis=1) * sin


KV_OUT_WIDTH = 3 * KV_HEADS * HEAD_DIM
PV_SPLITS = 4


def _attn_in_kernel(x_ref, mod_ref, g_ref, wq_ref, wkv_ref, qg_ref, kg_ref, seg_ref, *rest, rope):
    if rope:
        cos_ref, sin_ref, q_ref, kv_ref = rest
    else:
        q_ref, kv_ref = rest
    h = _adaln(x_ref[...], g_ref[...], mod_ref[0:1, :], mod_ref[1:2, :]).astype(BF16)
    pair_w = 2 * HEAD_DIM

    def head_pair(x2, gain, scale):
        sq = x2 * x2
        sq_hi = sq.astype(BF16)
        ss = _dot(sq_hi, seg_ref[...]) + _dot((sq - sq_hi.astype(F32)).astype(BF16), seg_ref[...])
        g2 = jnp.concatenate([gain, gain], axis=1)
        xn = x2 * lax.rsqrt(ss * (1.0 / HEAD_DIM) + EPS) * g2
        if rope:
            xn = jnp.concatenate([_rope(xn[:, i * HEAD_DIM:(i + 1) * HEAD_DIM], cos_ref[...], sin_ref[...])
                                  for i in range(2)], axis=1)
        return (xn * scale).astype(BF16) if scale != 1.0 else xn.astype(BF16)

    group_w = GROUP * HEAD_DIM
    q_groups = [_dot(h, wq_ref[:, n0:n0 + group_w]) for n0 in range(0, Q_HEADS * HEAD_DIM, group_w)]
    kv = _dot(h, wkv_ref[...])
    for gi, res in enumerate(q_groups):
        for n0 in range(0, group_w, pair_w):
            q_ref[:, gi * group_w + n0:gi * group_w + n0 + pair_w] = head_pair(res[:, n0:n0 + pair_w], qg_ref[...],
                                                                               HEAD_DIM ** -0.5)
    kw = KV_HEADS * HEAD_DIM
    assert kw == pair_w
    kv_ref[:, 0:kw] = head_pair(kv[:, 0:kw], kg_ref[...], 1.0)
    ones = jnp.ones((kv.shape[0], HEAD_DIM), BF16)
    for j in range(KV_HEADS):
        v0 = kw + 2 * j * HEAD_DIM
        kv_ref[:, v0:v0 + HEAD_DIM] = kv[:, kw + j * HEAD_DIM:kw + (j + 1) * HEAD_DIM].astype(BF16)
        kv_ref[:, v0 + HEAD_DIM:v0 + 2 * HEAD_DIM] = ones


def _attn_in(x2d, mod, layer, mod_index, gain, w_q, w_kv, q_g, k_g, tm, rope_tables, seq, name):
    rows, d = x2d.shape
    vec = lambda: pl.BlockSpec((1, HEAD_DIM), lambda i: (0, 0))
    in_specs = [pl.BlockSpec((tm, d), lambda i: (i, 0)), _mod_spec(layer, d, mod_index),
                pl.BlockSpec((1, d), lambda i: (0, 0)),
                pl.BlockSpec(w_q.shape, lambda i: (0, 0)), pl.BlockSpec(w_kv.shape, lambda i: (0, 0)), vec(), vec(),
                pl.BlockSpec((2 * HEAD_DIM, 2 * HEAD_DIM), lambda i: (0, 0))]
    seg = jnp.asarray(np.kron(np.eye(2, dtype=np.float32), np.ones((HEAD_DIM, HEAD_DIM), np.float32)), BF16)
    args = [x2d, mod, gain.reshape(1, d), w_q, w_kv, q_g.reshape(1, HEAD_DIM), k_g.reshape(1, HEAD_DIM), seg]
    if rope_tables is not None:
        tpb = seq // tm
        in_specs += [pl.BlockSpec((tm, HEAD_DIM), lambda i: (i % tpb, 0))] * 2
        args += list(rope_tables)
    return pl.pallas_call(
        functools.partial(_attn_in_kernel, rope=rope_tables is not None),
        grid=(rows // tm,),
        in_specs=in_specs,
        out_specs=[pl.BlockSpec((tm, w_q.shape[1]), lambda i: (i, 0)),
                   pl.BlockSpec((tm, KV_OUT_WIDTH), lambda i: (i, 0))],
        out_shape=[jax.ShapeDtypeStruct((rows, w_q.shape[1]), BF16),
                   jax.ShapeDtypeStruct((rows, KV_OUT_WIDTH), BF16)],
        compiler_params=_cparams("arbitrary"),
        name=name,
    )(*args)


def _attn_kernel(q_ref, kvc_ref, *rest, has_latent, tk):
    if has_latent:
        kvl_ref, o0_ref, o1_ref, s0_sc, s1_sc, p0_sc, p1_sc, m1_sc = rest
    else:
        o0_ref, o1_ref, s0_sc, s1_sc, p0_sc, p1_sc, m1_sc = rest
    tq = q_ref.shape[0]
    rows = GROUP * tq
    kw = KV_HEADS * HEAD_DIM
    lanes = HEAD_DIM
    n_ctx = kvc_ref.shape[0]
    n_lat = kvl_ref.shape[0] if has_latent else 0

    @pl.when((pl.program_id(0) == 0) & (pl.program_id(1) == 0))
    def _():
        s1_sc[...] = jnp.zeros_like(s1_sc)
        m1_sc[...] = jnp.zeros_like(m1_sc)

    def lane_groups(s):
        return [s[:, c * lanes:(c + 1) * lanes] for c in range(s.shape[1] // lanes)]

    tkc = min(tk, n_ctx)
    tiles = [(kvc_ref, r0, tkc, r0) for r0 in range(0, n_ctx, tkc)]
    tiles += [(kvl_ref, r0, tk, n_ctx + r0) for r0 in range(0, n_lat, tk)]
    pv_parts = [(kvc_ref, 0, n_ctx, 0)]
    pv_parts += [(kvl_ref, r0, n_lat // PV_SPLITS, n_ctx + r0) for r0 in range(0, n_lat, max(n_lat // PV_SPLITS, 1))]

    def stage(hk_new, s_new_sc, hk_old, s_old_sc, m_old, p_sc, o_ref):
        ksl = slice(hk_new * HEAD_DIM, (hk_new + 1) * HEAD_DIM)
        vsl = slice(kw + 2 * hk_old * HEAD_DIM, kw + 2 * (hk_old + 1) * HEAD_DIM)
        qcat = jnp.concatenate([q_ref[:, (hk_new * GROUP + gi) * HEAD_DIM:(hk_new * GROUP + gi + 1) * HEAD_DIM]
                                for gi in range(GROUP)], axis=0)
        mrun = jnp.full((rows, lanes), -jnp.inf, F32)
        partial, pending = [], list(pv_parts)
        for kv_ref, r0, w, c0 in tiles:
            s = _dot_nt(qcat, kv_ref[r0:r0 + w, ksl])
            s_new_sc[:, c0:c0 + w] = s
            mrun = functools.reduce(jnp.maximum, [mrun] + lane_groups(s))
            p = [jnp.exp((sg - m_old).astype(BF16)) for sg in lane_groups(s_old_sc[:, c0:c0 + w])]
            p_sc[:, c0:c0 + w] = jnp.concatenate(p, axis=1)
            while pending and pending[0][3] + pending[0][2] <= c0 + w:
                pv_ref, pr0, pw, pc0 = pending.pop(0)
                partial.append(_dot(p_sc[:, pc0:pc0 + pw], pv_ref[pr0:pr0 + pw, vsl]))
        acc = functools.reduce(jnp.add, partial)
        o = acc[:, 0:HEAD_DIM] / acc[:, HEAD_DIM:HEAD_DIM + 1]
        for gi in range(GROUP):
            o_ref[:, gi * HEAD_DIM:(gi + 1) * HEAD_DIM] = o[gi * tq:(gi + 1) * tq].astype(o_ref.dtype)
        return jnp.broadcast_to(jnp.max(mrun, axis=-1, keepdims=True), (rows, lanes))

    m0 = stage(0, s0_sc, 1, s1_sc, m1_sc[...], p1_sc, o1_ref)
    m1_sc[...] = stage(1, s1_sc, 0, s0_sc, m0, p0_sc, o0_ref)


def _attention(q, kv_ctx, kv_lat, bsz, tq, tk):
    rows = q.shape[0]
    nq = rows // bsz // tq
    n_ctx = kv_ctx.shape[0] // bsz
    has_latent = kv_lat is not None
    n_keys = n_ctx + (kv_lat.shape[0] // bsz if has_latent else 0)
    assert n_ctx % min(tk, n_ctx) == 0 and (n_keys - n_ctx) % (tk * PV_SPLITS) == 0
    in_specs = [pl.BlockSpec((tq, q.shape[1]), lambda b, j: (b * nq + jnp.minimum(j, nq - 1), 0)),
                pl.BlockSpec((n_ctx, kv_ctx.shape[1]), lambda b, j: (b, 0))]
    args = [q, kv_ctx]
    if has_latent:
        in_specs.append(pl.BlockSpec((n_keys - n_ctx, kv_lat.shape[1]), lambda b, j: (b, 0)))
        args.append(kv_lat)
    group_w = GROUP * HEAD_DIM
    scores = lambda: pltpu.VMEM((GROUP * tq, n_keys), F32)
    probs = lambda: pltpu.VMEM((GROUP * tq, n_keys), BF16)
    return pl.pallas_call(
        functools.partial(_attn_kernel, has_latent=has_latent, tk=tk),
        grid=(bsz, nq + 1),
        in_specs=in_specs,
        out_specs=[pl.BlockSpec((tq, group_w), lambda b, j: (b * nq + jnp.minimum(j, nq - 1), 0)),
                   pl.BlockSpec((tq, group_w), lambda b, j: (b * nq + jnp.maximum(j - 1, 0), 0))],
        out_shape=[jax.ShapeDtypeStruct((rows, group_w), BF16), jax.ShapeDtypeStruct((rows, group_w), BF16)],
        scratch_shapes=[scores(), scores(), probs(), probs(), pltpu.VMEM((GROUP * tq, HEAD_DIM), F32)],
        compiler_params=_cparams("arbitrary", "arbitrary"),
        name="attention" if has_latent else "attention_ctx",
    )(*args)


def _rope_tables(seq):
    n_rows = seq // GRID_W
    row = jnp.repeat(jnp.arange(n_rows), GRID_W)
    col = jnp.tile(jnp.arange(GRID_W), n_rows)
    n_pairs_axis = HEAD_DIM // 4
    inv_freq = ROPE_THETA ** (-jnp.arange(n_pairs_axis, dtype=F32) / n_pairs_axis)
    ang = jnp.concatenate([row[:, None] * inv_freq, col[:, None] * inv_freq], axis=-1)
    cos, sin = jnp.cos(ang), jnp.sin(ang)
    return jnp.concatenate([cos, cos], axis=-1), jnp.concatenate([-sin, sin], axis=-1)


def _tile(n, pref):
    t = min(n, pref)
    assert n % t == 0, (n, pref)
    return t


def kernel(x, c, ctx, c_ctx, norm_mix_g, norm_mlp_g, w_mod, b_mod, w_mlp1, w_mlp2, gla_w_in, gla_w_up_f, gla_b_f, gla_w_up_b, gla_b_b, gla_norm_g, gla_w_o, lru_w_in, lru_conv_w, lru_conv_b, lru_wa_f, lru_ba_f, lru_wx_f, lru_bx_f, lru_lam_f, lru_wa_b, lru_ba_b, lru_wx_b, lru_bx_b, lru_lam_b, lru_w_o, attn_w_in, attn_q_g, attn_k_g, attn_w_o, final_g):
    bsz, seq, d = x.shape
    n_ctx = ctx.shape[1]
    depth = w_mod.shape[0]
    assert bsz + 1 <= MOD_ROWS and d == GLA_HEADS * GLA_DV == Q_HEADS * HEAD_DIM

    mod = _modulation(c, c_ctx, w_mod, b_mod)
    xl = x.reshape(bsz * seq, d)
    xc = ctx.reshape(bsz * n_ctx, d)

    tm_l = _tile(seq, 512)
    tm_c = _tile(n_ctx, 256)
    lat_idx = lambda i: 1 + (i * tm_l) // seq
    ctx_idx = lambda i: 0

    for layer in range(depth):
        need_ctx = layer < depth - 1
        kind, j = layer % N_MIXERS, layer // N_MIXERS
        gmix = norm_mix_g[layer]

        if kind == 0:
            dq = GLA_HEADS * GLA_DK
            main_w = gla_w_in[j][:, :2 * dq + 2 * d].astype(BF16)
            gate_w = jnp.pad(gla_w_in[j][:, 2 * dq + 2 * d:], ((0, 0), (0, GLA_GATE_PAD - 2 * GLA_GATE_RANK))).astype(BF16)
            zpad = jnp.zeros((GLA_GATE_PAD - 2 * GLA_GATE_RANK, dq), F32)
            wup_f = jnp.concatenate([gla_w_up_f[j], jnp.zeros_like(gla_w_up_b[j]), zpad], axis=0).astype(BF16)
            wup_b = jnp.concatenate([jnp.zeros_like(gla_w_up_f[j]), gla_w_up_b[j], zpad], axis=0).astype(BF16)
            bup_f, bup_b = gla_b_f[j].reshape(1, dq), gla_b_b[j].reshape(1, dq)
            w_o = gla_w_o[j].astype(BF16)
            proj = {}
            for name, (xs, midx, tm, slen) in zip(("lat", "ctx"), [(xl, lat_idx, tm_l, seq), (xc, ctx_idx, tm_c, n_ctx)]):
                proj[name] = _ln_matmul(xs, mod, layer, midx, gmix, [main_w, gate_w], [BF16, BF16], tm,
                                        name="gla_in_" + name)
            s0 = jnp.zeros((bsz, GLA_HEADS, GLA_DK, GLA_DV), F32)
            ch_c, ch_l = _tile(n_ctx, 128), _tile(seq, 128)
            ocf, ocb, s_cf, s_cb = _gla_scan(*proj["ctx"], wup_f, wup_b, bup_f, bup_b, s0, s0, bsz, ch_c)
            olf, olb, _, _ = _gla_scan(*proj["lat"], wup_f, wup_b, bup_f, bup_b, s_cf, s_cb, bsz, ch_l)
            ng = gla_norm_g[j].reshape(1, GLA_DV)

            def gla_out(of, ob, qkvr, xs, midx, tm):
                blk = lambda i: (i, 0)
                return _proj_res("gla", [of, ob, qkvr], [pl.BlockSpec((tm, d), blk), pl.BlockSpec((tm, d), blk),
                                                        pl.BlockSpec((tm, d), lambda i: (i, (2 * dq + d) // d))],
                                 w_o, xs, mod, layer, midx, tm, extra=[ng],
                                 extra_specs=[pl.BlockSpec((1, GLA_DV), lambda i: (0, 0))])

            xl = gla_out(olf, olb, proj["lat"][0], xl, lat_idx, tm_l)
            if need_ctx:
                xc = gla_out(ocf, ocb, proj["ctx"][0], xc, ctx_idx, tm_c)

        elif kind == 1:
            w_in = lru_w_in[j].astype(BF16)
            w_o = lru_w_o[j].astype(BF16)
            wcat_f = jnp.concatenate([lru_wa_f[j], lru_wx_f[j]], axis=-1).astype(BF16)
            wcat_b = jnp.concatenate([lru_wa_b[j], lru_wx_b[j]], axis=-1).astype(BF16)
            p_l, = _ln_matmul(xl, mod, layer, lat_idx, gmix, [w_in], [BF16], tm_l, name="lru_in_lat")
            p_c, = _ln_matmul(xc, mod, layer, ctx_idx, gmix, [w_in], [BF16], tm_c, name="lru_in_ctx")
            h0 = jnp.zeros((bsz, d), F32)
            fwd = (lru_conv_w[j], lru_conv_b[j], wcat_f, lru_ba_f[j], lru_bx_f[j], lru_lam_f[j])
            bwd = (lru_conv_w[j], lru_conv_b[j], wcat_b, lru_ba_b[j], lru_bx_b[j], lru_lam_b[j])
            tt_c, tt_l = _tile(n_ctx, 128), _tile(seq, 128)
            p_c3, p_l3 = p_c.reshape(bsz, n_ctx, 2 * d), p_l.reshape(bsz, seq, 2 * d)
            hcf, s_cf = _lru_scan(p_c3, *fwd, h0, tt_c, False)
            hcb, s_cb = _lru_scan(p_c3, *bwd, h0, tt_c, True)
            hlf, _ = _lru_scan(p_l3, *fwd, s_cf, tt_l, False)
            hlb, _ = _lru_scan(p_l3, *bwd, s_cb, tt_l, True)

            def lru_out(hf, hb, p2d, xs, midx, tm):
                blk = lambda i: (i, 0)
                spec = lambda: pl.BlockSpec((tm, d), blk)
                return _proj_res("lru", [hf.reshape(-1, d), hb.reshape(-1, d), p2d], [spec(), spec(), spec()],
                                 w_o, xs, mod, layer, midx, tm)

            xl = lru_out(hlf, hlb, p_l, xl, lat_idx, tm_l)
            if need_ctx:
                xc = lru_out(hcf, hcb, p_c, xc, ctx_idx, tm_c)

        else:
            qw = Q_HEADS * HEAD_DIM
            w_q = attn_w_in[j][:, :qw].astype(BF16)
            w_kv = attn_w_in[j][:, qw:].astype(BF16)
            w_o = attn_w_o[j].astype(BF16)
            qg, kg = attn_q_g[j], attn_k_g[j]
            q_l, kv_l = _attn_in(xl, mod, layer, lat_idx, gmix, w_q, w_kv, qg, kg, tm_l, _rope_tables(seq), seq,
                                 "attn_in_lat")
            q_c, kv_c = _attn_in(xc, mod, layer, ctx_idx, gmix, w_q, w_kv, qg, kg, tm_c, None, n_ctx, "attn_in_ctx")
            tk = _tile(seq // PV_SPLITS, 512)
            o_l = _attention(q_l, kv_c, kv_l, bsz, _tile(seq, 128), tk)
            half = lambda tm: [pl.BlockSpec((tm, d // KV_HEADS), lambda i: (i, 0))] * KV_HEADS
            xl = _proj_res("plain", o_l, half(tm_l), w_o, xl, mod, layer, lat_idx, tm_l)
            if need_ctx:
                o_c = _attention(q_c, kv_c, None, bsz, _tile(n_ctx, 128), tk)
                xc = _proj_res("plain", o_c, half(tm_c), w_o, xc, mod, layer, ctx_idx, tm_c)

        w1, w2 = w_mlp1[layer].astype(BF16), w_mlp2[layer].astype(BF16)
        tf = _tile(w1.shape[1], 1024)
        tmm_l = _tile(seq, 1024)
        xl = _mlp(xl, mod, layer, lambda i: 1 + (i * tmm_l) // seq, norm_mlp_g[layer], w1, w2, tmm_l, tf,
                  final_gain=None if need_ctx else final_g)
        if need_ctx:
            xc = _mlp(xc, mod, layer, ctx_idx, norm_mlp_g[layer], w1, w2, _tile(bsz * n_ctx, 1024), tf)

    return xl.reshape(bsz, seq, d)
```

```python
import functools

import numpy as np
import jax
import jax.numpy as jnp
from jax import lax
from jax.experimental import pallas as pl
from jax.experimental.pallas import tpu as pltpu

F32 = jnp.float32
BF16 = jnp.bfloat16

EPS = 1e-6
N_MOD = 6
N_MIXERS = 3
MOD_ROWS = 16

GLA_HEADS = 4
GLA_DK = 128
GLA_DV = 256
GLA_GATE_RANK = 16
GLA_GATE_TAU = 16.0
GLA_GATE_PAD = 128

RNN_BLOCKS = 8
RNN_BLOCK_DIM = 128
CONV_WIDTH = 4
CONV_LEFT = 2
LRU_C = 8.0

HEAD_DIM = 128
Q_HEADS = 8
KV_HEADS = 2
GROUP = Q_HEADS // KV_HEADS
GRID_W = 64
ROPE_THETA = 10000.0

VMEM_LIMIT_BYTES = 56 * 1024 * 1024


def _cparams(*sem):
    return pltpu.CompilerParams(dimension_semantics=sem, vmem_limit_bytes=VMEM_LIMIT_BYTES)


def _dot(a, b):
    return jnp.dot(a, b, preferred_element_type=F32)


def _dot_nt(a, b):
    return lax.dot_general(a, b, (((1,), (1,)), ((), ())), preferred_element_type=F32)


def _dot_tn(a, b):
    return lax.dot_general(a, b, (((0,), (0,)), ((), ())), preferred_element_type=F32)


def _rms(x, g):
    return x * lax.rsqrt(jnp.mean(x * x, axis=-1, keepdims=True) + EPS) * g


def _adaln(x, g, shift, scale):
    return _rms(x, g) * (1.0 + scale) + shift


def _mod_kernel(s_ref, w_ref, b_ref, o_ref):
    s = s_ref[...]
    s = s * jax.nn.sigmoid(s)
    o_ref[...] = _dot(s.astype(BF16), w_ref[...].astype(BF16)) + b_ref[...]


def _modulation(c, c_ctx, w_mod, b_mod):
    depth, d, nd = w_mod.shape
    bsz = c.shape[0]
    rows = jnp.concatenate([c_ctx[None, :], c, jnp.zeros((MOD_ROWS - 1 - bsz, d), F32)], axis=0)
    tn = min(nd, 1536)
    out = pl.pallas_call(
        _mod_kernel,
        grid=(depth, nd // tn),
        in_specs=[pl.BlockSpec((MOD_ROWS, d), lambda l, j: (0, 0)),
                  pl.BlockSpec((None, d, tn), lambda l, j: (l, 0, j)),
                  pl.BlockSpec((None, 1, tn), lambda l, j: (l, 0, j))],
        out_specs=pl.BlockSpec((None, MOD_ROWS, tn), lambda l, j: (l, 0, j)),
        out_shape=jax.ShapeDtypeStruct((depth, MOD_ROWS, nd), F32),
        compiler_params=_cparams("arbitrary", "arbitrary"),
        name="modulation",
    )(rows, w_mod, b_mod.reshape(depth, 1, nd))
    return out.reshape(depth, MOD_ROWS, N_MOD, d)


def _mod_spec(layer, d, mod_index):
    return pl.BlockSpec((None, None, N_MOD, d), lambda i, *_: (layer, mod_index(i), 0, 0))


def _ln_matmul_kernel(x_ref, mod_ref, g_ref, *refs, n_out, mod_off, col_chunk):
    w_refs, o_refs = refs[:n_out], refs[n_out:]
    h = _adaln(x_ref[...], g_ref[...], mod_ref[mod_off:mod_off + 1, :],
               mod_ref[mod_off + 1:mod_off + 2, :]).astype(BF16)
    for w_ref, o_ref in zip(w_refs, o_refs):
        n = w_ref.shape[1]
        for n0 in range(0, n, col_chunk):
            n1 = min(n, n0 + col_chunk)
            o_ref[:, n0:n1] = _dot(h, w_ref[:, n0:n1]).astype(o_ref.dtype)


def _ln_matmul(x2d, mod, layer, mod_index, gain, ws, out_dtypes, tm, name="ln_matmul"):
    rows, d = x2d.shape
    in_specs = [pl.BlockSpec((tm, d), lambda i: (i, 0)), _mod_spec(layer, d, mod_index),
                pl.BlockSpec((1, d), lambda i: (0, 0))]
    in_specs += [pl.BlockSpec(w.shape, lambda i: (0, 0)) for w in ws]
    return pl.pallas_call(
        functools.partial(_ln_matmul_kernel, n_out=len(ws), mod_off=0, col_chunk=512),
        grid=(rows // tm,),
        in_specs=in_specs,
        out_specs=[pl.BlockSpec((tm, w.shape[1]), lambda i: (i, 0)) for w in ws],
        out_shape=[jax.ShapeDtypeStruct((rows, w.shape[1]), dt) for w, dt in zip(ws, out_dtypes)],
        compiler_params=_cparams("arbitrary"),
        name=name,
    )(x2d, mod, gain.reshape(1, d), *ws)


def _proj_res_kernel(y0_ref, y1_ref, w_ref, x_ref, mod_ref, o_ref):
    y = jnp.concatenate([y0_ref[...], y1_ref[...]], axis=1)
    o_ref[...] = x_ref[...] + mod_ref[2:3, :] * _dot(y, w_ref[...])


def _gla_proj_res_kernel(of_ref, ob_ref, r_ref, ng_ref, w_ref, x_ref, mod_ref, o_ref):
    o = of_ref[...].astype(F32) + ob_ref[...].astype(F32)
    r = r_ref[...].astype(F32)
    ys = []
    for h in range(GLA_HEADS):
        sl = slice(h * GLA_DV, (h + 1) * GLA_DV)
        rh = r[:, sl]
        ys.append(_rms(o[:, sl], ng_ref[...]) * (rh * jax.nn.sigmoid(rh)))
    y = jnp.concatenate(ys, axis=1).astype(BF16)
    o_ref[...] = x_ref[...] + mod_ref[2:3, :] * _dot(y, w_ref[...])


def _lru_proj_res_kernel(hf_ref, hb_ref, gate_ref, w_ref, x_ref, mod_ref, o_ref):
    h = hf_ref[...].astype(F32) + hb_ref[...].astype(F32)
    y = (h * jax.nn.gelu(gate_ref[...].astype(F32))).astype(BF16)
    o_ref[...] = x_ref[...] + mod_ref[2:3, :] * _dot(y, w_ref[...])


def _proj_res(kind, operands, operand_specs, w, x2d, mod, layer, mod_index, tm, extra=(), extra_specs=()):
    rows, d = x2d.shape
    body = {"plain": _proj_res_kernel, "gla": _gla_proj_res_kernel, "lru": _lru_proj_res_kernel}[kind]
    n_in = len(operands) + len(extra) + 1
    return pl.pallas_call(
        body,
        grid=(rows // tm,),
        in_specs=[*operand_specs, *extra_specs, pl.BlockSpec(w.shape, lambda i: (0, 0)),
                  pl.BlockSpec((tm, d), lambda i: (i, 0)), _mod_spec(layer, d, mod_index)],
        out_specs=pl.BlockSpec((tm, d), lambda i: (i, 0)),
        out_shape=jax.ShapeDtypeStruct((rows, d), F32),
        input_output_aliases={n_in: 0} if layer > 0 else {},
        compiler_params=_cparams("arbitrary"),
        name=kind + "_proj_res",
    )(*operands, *extra, w, x2d, mod)


def _mlp_kernel(x_ref, mod_ref, g_ref, w1_ref, w2_ref, *rest, final_norm):
    if final_norm:
        fg_ref, o_ref, h_sc, acc_sc = rest
    else:
        o_ref, h_sc, acc_sc = rest
    j = pl.program_id(1)

    @pl.when(j == 0)
    def _():
        h_sc[...] = _adaln(x_ref[...], g_ref[...], mod_ref[3:4, :], mod_ref[4:5, :]).astype(BF16)
        acc_sc[...] = jnp.zeros_like(acc_sc)

    a = jnp.maximum(_dot(h_sc[...], w1_ref[...]), 0.0)
    acc_sc[...] += _dot((a * a).astype(BF16), w2_ref[...])

    @pl.when(j == pl.num_programs(1) - 1)
    def _():
        y = x_ref[...] + mod_ref[5:6, :] * acc_sc[...]
        o_ref[...] = _rms(y, fg_ref[...]) if final_norm else y


def _mlp(x2d, mod, layer, mod_index, gain, w1, w2, tm, tf, final_gain=None):
    rows, d = x2d.shape
    dff = w1.shape[1]
    vec = lambda: pl.BlockSpec((1, d), lambda i, j: (0, 0))
    in_specs = [pl.BlockSpec((tm, d), lambda i, j: (i, 0)), _mod_spec(layer, d, mod_index), vec(),
                pl.BlockSpec((d, tf), lambda i, j: (0, j)),
                pl.BlockSpec((tf, d), lambda i, j: (j, 0))]
    args = [x2d, mod, gain.reshape(1, d), w1, w2]
    if final_gain is not None:
        in_specs.append(vec())
        args.append(final_gain.reshape(1, d))
    return pl.pallas_call(
        functools.partial(_mlp_kernel, final_norm=final_gain is not None),
        grid=(rows // tm, dff // tf),
        in_specs=in_specs,
        out_specs=pl.BlockSpec((tm, d), lambda i, j: (i, 0)),
        out_shape=jax.ShapeDtypeStruct((rows, d), F32),
        scratch_shapes=[pltpu.VMEM((tm, d), BF16), pltpu.VMEM((tm, d), F32)],
        input_output_aliases={0: 0},
        compiler_params=_cparams("arbitrary", "arbitrary"),
        name="mlp",
    )(*args)


GLA_CHUNK = 256
GLA_MATMUL_LEVEL_ROWS = 8
GLA_FAST_MAX_DECAY = 60.0


def _gla_tables(chunk, reverse):
    levels = int(np.log2(chunk))
    idx = np.arange(chunk)
    t, r = idx[:, None], idx[None, :]
    if not reverse:
        q_incl = (r <= t)
        k_rest = (r > t)
    else:
        q_incl = (r >= t)
        k_rest = (r < t)
    mats = [q_incl, k_rest]
    masks = [np.eye(chunk, dtype=bool)]
    for lv in range(1, levels + 1):
        m = 1 << (lv - 1)
        mid = (idx // (2 * m)) * (2 * m) + m
        upper = idx >= mid
        midc = mid[:, None]
        if not reverse:
            pat = np.where(upper[:, None], (r >= midc) & (r <= t), (r > t) & (r < midc))
            mask = (upper[:, None] & ~upper[None, :])
        else:
            pat = np.where(upper[:, None], (r >= midc) & (r < t), (r >= t) & (r < midc))
            mask = (~upper[:, None] & upper[None, :])
        mask = mask & ((idx[:, None] // (2 * m)) == (idx[None, :] // (2 * m)))
        if 2 * m <= GLA_MATMUL_LEVEL_ROWS:
            mats.append(pat)
        masks.append(mask)
    a = np.concatenate(mats, axis=0).astype(np.float32)
    a2 = np.concatenate([a, a], axis=1)
    return jnp.asarray(a2, BF16), jnp.asarray(np.stack(masks).astype(np.float32), F32)


def _gla_gates(qkvr_ref, gf_ref, wup_ref, bup_ref, a_ref, reverse):
    chunk = qkvr_ref.shape[0]
    dq = GLA_HEADS * GLA_DK
    graw = _dot(gf_ref[...], wup_ref[...]) + bup_ref[...]
    g = (jnp.minimum(graw, 0.0) - jnp.log(1.0 + jnp.exp(-jnp.abs(graw)))) * (1.0 / GLA_GATE_TAU)
    g_hi = g.astype(BF16)
    g_cat = jnp.concatenate([g_hi, (g - g_hi.astype(F32)).astype(BF16)], axis=0)

    def partial_sums(block):
        return _dot(a_ref[block * chunk:(block + 1) * chunk, :], g_cat)

    st = dict(partial_sums=partial_sums, bq=partial_sums(0), bk=partial_sums(1), reverse=reverse, chunk=chunk,
              q_all=qkvr_ref[:, 0:dq].astype(F32) * (GLA_DK ** -0.5), k_all=qkvr_ref[:, dq:2 * dq].astype(F32))
    st["qd"] = (st["q_all"] * jnp.exp(st["bq"])).astype(BF16)
    st["kd"] = (st["k_all"] * jnp.exp(st["bk"])).astype(BF16)
    st["btot"] = st["bq"][0:1, :] if reverse else st["bq"][chunk - 1:chunk, :]
    return st


def _gla_levels(st, z_sc, lv_range):
    chunk, reverse, bq, bk = st["chunk"], st["reverse"], st["bq"], st["bk"]
    q_all, k_all = st["q_all"], st["k_all"]
    row = lax.broadcasted_iota(jnp.int32, (chunk, 1), 0)
    for lv in lv_range:
        m = 1 << (lv - 1)
        if 2 * m <= GLA_MATMUL_LEVEL_ROWS:
            use_q = ((row // m) % 2) == (0 if reverse else 1)
            z = jnp.where(use_q, q_all, k_all) * jnp.exp(st["partial_sums"](1 + lv))
        else:
            pieces = []
            for lo in range(0, chunk, 2 * m):
                mid, hi = lo + m, lo + 2 * m
                if not reverse:
                    pieces.append(k_all[lo:mid] * jnp.exp(bk[lo:mid] - bk[mid - 1:mid]))
                    pieces.append(q_all[mid:hi] * jnp.exp(bq[mid:hi] - bq[mid - 1:mid]))
                else:
                    pieces.append(q_all[lo:mid] * jnp.exp(bq[lo:mid] - bq[mid:mid + 1]))
                    pieces.append(k_all[mid:hi] * jnp.exp(bk[mid:hi] - bk[mid:mid + 1]))
            z = jnp.concatenate(pieces, axis=0)
        z_sc[lv - 1] = z.astype(BF16)


def _gla_scores_by_level(st, h, mask_ref, z_sc, levels):
    sl = slice(h * GLA_DK, (h + 1) * GLA_DK)
    chunk, reverse = st["chunk"], st["reverse"]
    sub = 8
    diag = mask_ref[0] * jnp.sum(st["q_all"][:, sl] * st["k_all"][:, sl], axis=-1, keepdims=True)
    tiles = [diag[r:r + sub] for r in range(0, chunk, sub)]
    for lv in range(1, levels + 1):
        m = 1 << (lv - 1)
        if 2 * m <= GLA_MATMUL_LEVEL_ROWS:
            q_rows = [(0, chunk)]
        else:
            q_rows = [(lo, lo + m) if reverse else (lo + m, lo + 2 * m) for lo in range(0, chunk, 2 * m)]
        zq = jnp.concatenate([z_sc[lv - 1, lo:hi, sl] for lo, hi in q_rows], axis=0)
        mq = jnp.concatenate([mask_ref[lv, lo:hi, :] for lo, hi in q_rows], axis=0)
        part = mq * _dot_nt(zq, z_sc[lv - 1, :, sl])
        src = 0
        for lo, hi in q_rows:
            for r in range(lo, hi, sub):
                tiles[r // sub] = tiles[r // sub] + part[src:src + sub]
                src += sub
    return jnp.concatenate(tiles, axis=0)


def _gla_head_update(st, h, scores, qkvr_ref, s_sc, o_ref):
    dq = GLA_HEADS * GLA_DK
    sl = slice(h * GLA_DK, (h + 1) * GLA_DK)
    vh = qkvr_ref[:, 2 * dq + h * GLA_DV:2 * dq + (h + 1) * GLA_DV]
    s_old = s_sc[h]
    o_ref[:, h * GLA_DV:(h + 1) * GLA_DV] = (_dot(scores.astype(BF16), vh)
                                            + _dot(st["qd"][:, sl], s_old.astype(BF16))).astype(o_ref.dtype)
    etot = jnp.exp(st["btot"][:, sl])
    ecol = jnp.transpose(jnp.broadcast_to(etot, (GLA_DK, GLA_DK)))
    s_sc[h] = s_old * jnp.concatenate([ecol] * (GLA_DV // GLA_DK), axis=1) + _dot_tn(st["kd"][:, sl], vh)


def _gla_kernel(qkvr_f, gf_f, qkvr_b, gf_b, wup_f, wup_b, bup_f, bup_b, a_f, a_b, mask_f, mask_b, pair_f, pair_b,
                s0_f, s0_b, o_f, o_b, sfin_f, sfin_b, s_sc_f, s_sc_b, z_sc_f, z_sc_b):
    c = pl.program_id(1)
    levels = mask_f.shape[0] - 1
    dirs = ((qkvr_f, mask_f, pair_f, z_sc_f, s_sc_f, o_f), (qkvr_b, mask_b, pair_b, z_sc_b, s_sc_b, o_b))

    @pl.when(c == 0)
    def _():
        s_sc_f[...] = s0_f[...]
        s_sc_b[...] = s0_b[...]

    sts = (_gla_gates(qkvr_f, gf_f, wup_f, bup_f, a_f, False), _gla_gates(qkvr_b, gf_b, wup_b, bup_b, a_b, True))
    worst = jnp.maximum(jnp.max(jnp.abs(sts[0]["btot"])), jnp.max(jnp.abs(sts[1]["btot"])))
    small_decay = worst <= GLA_FAST_MAX_DECAY

    @pl.when(small_decay)
    def _():
        kinv = [(st["k_all"] * jnp.exp(-st["bq"])).astype(BF16) for st in sts]
        for h in range(GLA_HEADS):
            sl = slice(h * GLA_DK, (h + 1) * GLA_DK)
            for st, kv, (qkvr, _, pair, _, s_sc, o_ref) in zip(sts, kinv, dirs):
                scores = pair[...] * _dot_nt(st["qd"][:, sl], kv[:, sl])
                _gla_head_update(st, h, scores, qkvr, s_sc, o_ref)

    @pl.when(jnp.logical_not(small_decay))
    def _():
        for st, (_, _, _, z_sc, _, _) in zip(sts, dirs):
            _gla_levels(st, z_sc, range(1, levels + 1))
        for h in range(GLA_HEADS):
            for st, (qkvr, mask, _, z_sc, s_sc, o_ref) in zip(sts, dirs):
                _gla_head_update(st, h, _gla_scores_by_level(st, h, mask, z_sc, levels), qkvr, s_sc, o_ref)

    @pl.when(c == pl.num_programs(1) - 1)
    def _():
        sfin_f[...] = s_sc_f[...]
        sfin_b[...] = s_sc_b[...]


def _gla_scan(qkvr, gfeat, wup_f, wup_b, bup_f, bup_b, s0_f, s0_b, bsz, chunk):
    rows = qkvr.shape[0]
    nc = rows // bsz // chunk
    a_f, mask_f = _gla_tables(chunk, False)
    a_b, mask_b = _gla_tables(chunk, True)
    tri = np.tril(np.ones((chunk, chunk), np.float32))
    pair_f, pair_b = jnp.asarray(tri), jnp.asarray(tri.T)
    dv = GLA_HEADS * GLA_DV
    fwd_block = lambda b, c: (b * nc + c, 0)
    bwd_block = lambda b, c: (b * nc + nc - 1 - c, 0)
    const2 = lambda x: pl.BlockSpec(x.shape, lambda b, c: (0, 0))
    const3 = lambda x: pl.BlockSpec(x.shape, lambda b, c: (0, 0, 0))
    state_spec = pl.BlockSpec((None, GLA_HEADS, GLA_DK, GLA_DV), lambda b, c: (b, 0, 0, 0))
    state_shape = jax.ShapeDtypeStruct((bsz, GLA_HEADS, GLA_DK, GLA_DV), F32)
    state_sc = lambda: pltpu.VMEM((GLA_HEADS, GLA_DK, GLA_DV), F32)
    z_sc = lambda: pltpu.VMEM((mask_f.shape[0] - 1, chunk, GLA_HEADS * GLA_DK), BF16)
    return pl.pallas_call(
        _gla_kernel,
        grid=(bsz, nc),
        in_specs=[pl.BlockSpec((chunk, qkvr.shape[1]), fwd_block), pl.BlockSpec((chunk, gfeat.shape[1]), fwd_block),
                  pl.BlockSpec((chunk, qkvr.shape[1]), bwd_block), pl.BlockSpec((chunk, gfeat.shape[1]), bwd_block),
                  const2(wup_f), const2(wup_b), const2(bup_f), const2(bup_b), const2(a_f), const2(a_b),
                  const3(mask_f), const3(mask_b), const2(pair_f), const2(pair_b), state_spec, state_spec],
        out_specs=[pl.BlockSpec((chunk, dv), fwd_block), pl.BlockSpec((chunk, dv), bwd_block), state_spec, state_spec],
        out_shape=[jax.ShapeDtypeStruct((rows, dv), BF16), jax.ShapeDtypeStruct((rows, dv), BF16),
                   state_shape, state_shape],
        scratch_shapes=[state_sc(), state_sc(), z_sc(), z_sc()],
        compiler_params=_cparams("arbitrary", "arbitrary"),
        name="gla_scan",
    )(qkvr, gfeat, qkvr, gfeat, wup_f, wup_b, bup_f, bup_b, a_f, a_b, mask_f, mask_b, pair_f, pair_b, s0_f, s0_b)


LRU_HALO = 16
CONV_SHIFTS = tuple(j - CONV_LEFT for j in range(CONV_WIDTH) if j != CONV_LEFT)
F32_TINY = float(np.finfo(np.float32).tiny)


def _lru_shift_matrix(tt):
    sel = np.zeros((len(CONV_SHIFTS) * tt, tt + 2 * LRU_HALO), np.float32)
    for i, off in enumerate(CONV_SHIFTS):
        sel[i * tt + np.arange(tt), LRU_HALO + np.arange(tt) + off] = 1.0
    return jnp.asarray(sel, BF16)


def _lru_kernel(x_ref, prev_ref, next_ref, shift_ref, cw_ref, cb_ref, wcat_ref, ba_ref, bx_ref, lam_ref, h0_ref,
                hs_ref, hfin_ref, xs_sc, xc_sc, a_sc, u_sc, hs_sc, h_sc, *, reverse):
    i = pl.program_id(0)
    n = pl.num_programs(0)
    tidx = n - 1 - i if reverse else i
    bsz, tt, width = x_ref.shape
    pitch = a_sc.shape[1] // bsz
    bd = RNN_BLOCK_DIM

    @pl.when(i == 0)
    def _():
        h_sc[...] = h0_ref[...]

    xs_sc[:, 0:LRU_HALO, :] = jnp.where(tidx > 0, prev_ref[...], jnp.zeros_like(prev_ref))
    xs_sc[:, LRU_HALO:LRU_HALO + tt, :] = x_ref[...]
    xs_sc[:, LRU_HALO + tt:2 * LRU_HALO + tt, :] = jnp.where(tidx < n - 1, next_ref[...], jnp.zeros_like(next_ref))
    for b in range(bsz):
        shifted = _dot(shift_ref[...], xs_sc[b])
        xc = cb_ref[...] + cw_ref[CONV_LEFT:CONV_LEFT + 1, :] * x_ref[b].astype(F32)
        for k, off in enumerate(CONV_SHIFTS):
            j = off + CONV_LEFT
            xc = xc + cw_ref[j:j + 1, :] * shifted[k * tt:(k + 1) * tt]
        xc_sc[b] = xc

    lam = lam_ref[...]
    neg4sp = -0.5 * LRU_C * (jnp.maximum(-lam, 0.0) + jnp.log1p(jnp.exp(-jnp.abs(lam))))
    for nb in range(RNN_BLOCKS):
        sl = slice(nb * bd, (nb + 1) * bd)
        xc2 = xc_sc[:, :, sl].reshape(bsz * tt, bd)
        ri = _dot(xc2.astype(BF16), wcat_ref[nb])
        log_a = neg4sp[:, sl] + neg4sp[:, sl] * jnp.tanh(ri[:, :bd] + ba_ref[:, sl])
        ig = 0.5 + 0.5 * jnp.tanh(ri[:, bd:] + bx_ref[:, sl])
        th = jnp.tanh(log_a)
        one_minus_a2 = (-2.0 * th) / (1.0 - th)
        root = one_minus_a2 * lax.rsqrt(jnp.maximum(one_minus_a2, F32_TINY))
        a = jnp.exp(log_a)
        u = root * (ig * xc2)
        for b in range(bsz):
            a_sc[nb, b * pitch:b * pitch + tt, :] = a[b * tt:(b + 1) * tt]
            u_sc[nb, b * pitch:b * pitch + tt, :] = u[b * tt:(b + 1) * tt]

    def step(k, hs):
        t = tt - 1 - k if reverse else k
        rows = pl.ds(t, bsz, stride=pitch)
        new = []
        for nb in range(RNN_BLOCKS):
            h = a_sc[nb, rows, :] * hs[nb] + u_sc[nb, rows, :]
            hs_sc[nb, rows, :] = h
            new.append(h)
        return tuple(new)

    h_init = tuple(h_sc[:, nb * bd:(nb + 1) * bd] for nb in range(RNN_BLOCKS))
    h_last = lax.fori_loop(0, tt, step, h_init, unroll=4)
    for nb in range(RNN_BLOCKS):
        sl = slice(nb * bd, (nb + 1) * bd)
        h_sc[:, sl] = h_last[nb]
        for b in range(bsz):
            hs_ref[b, :, sl] = hs_sc[nb, b * pitch:b * pitch + tt, :].astype(hs_ref.dtype)

    @pl.when(i == n - 1)
    def _():
        hfin_ref[...] = h_sc[...]


def _lru_scan(p3, conv_w, conv_b, wcat, ba, bx, lam, h0, tt, reverse):
    bsz, seq, two_w = p3.shape
    width = two_w // 2
    n = seq // tt
    hb = tt // LRU_HALO
    pitch = tt + 8
    assert tt % LRU_HALO == 0 and width == RNN_BLOCKS * RNN_BLOCK_DIM
    shift = _lru_shift_matrix(tt)

    def tix(i):
        return n - 1 - i if reverse else i

    vec = lambda: pl.BlockSpec((1, width), lambda i: (0, 0))
    slab = lambda: pltpu.VMEM((RNN_BLOCKS, bsz * pitch, RNN_BLOCK_DIM), F32)
    return pl.pallas_call(
        functools.partial(_lru_kernel, reverse=reverse),
        grid=(n,),
        in_specs=[pl.BlockSpec((bsz, tt, width), lambda i: (0, tix(i), 1)),
                  pl.BlockSpec((bsz, LRU_HALO, width), lambda i: (0, jnp.maximum(tix(i) * hb - 1, 0), 1)),
                  pl.BlockSpec((bsz, LRU_HALO, width), lambda i: (0, jnp.minimum((tix(i) + 1) * hb, seq // LRU_HALO - 1), 1)),
                  pl.BlockSpec(shift.shape, lambda i: (0, 0)),
                  pl.BlockSpec((CONV_WIDTH, width), lambda i: (0, 0)), vec(),
                  pl.BlockSpec(wcat.shape, lambda i: (0, 0, 0)), vec(), vec(), vec(),
                  pl.BlockSpec((bsz, width), lambda i: (0, 0))],
        out_specs=[pl.BlockSpec((bsz, tt, width), lambda i: (0, tix(i), 0)),
                   pl.BlockSpec((bsz, width), lambda i: (0, 0))],
        out_shape=[jax.ShapeDtypeStruct((bsz, seq, width), BF16),
                   jax.ShapeDtypeStruct((bsz, width), F32)],
        scratch_shapes=[pltpu.VMEM((bsz, tt + 2 * LRU_HALO, width), BF16), pltpu.VMEM((bsz, tt, width), F32),
                        slab(), slab(), slab(), pltpu.VMEM((bsz, width), F32)],
        compiler_params=_cparams("arbitrary"),
        name="lru_scan_bwd" if reverse else "lru_scan_fwd",
    )(p3, p3, p3, shift, conv_w, conv_b.reshape(1, width), wcat, ba.reshape(1, width),
      bx.reshape(1, width), lam.reshape(1, width), h0)


def _rope(x, cos, sin):
    return x * cos + pltpu.roll(x, HEAD_DIM // 2, axis=1) * sin


KV_OUT_WIDTH = 3 * KV_HEADS * HEAD_DIM
PV_SPLITS = 4


def _attn_in_kernel(x_ref, mod_ref, g_ref, wq_ref, wkv_ref, qg_ref, kg_ref, seg_ref, *rest, rope):
    if rope:
        cos_ref, sin_ref, q_ref, kv_ref = rest
    else:
        q_ref, kv_ref = rest
    h = _adaln(x_ref[...], g_ref[...], mod_ref[0:1, :], mod_ref[1:2, :]).astype(BF16)
    pair_w = 2 * HEAD_DIM

    def head_pair(x2, gain, scale):
        sq = x2 * x2
        sq_hi = sq.astype(BF16)
        ss = _dot(sq_hi, seg_ref[...]) + _dot((sq - sq_hi.astype(F32)).astype(BF16), seg_ref[...])
        g2 = jnp.concatenate([gain, gain], axis=1)
        xn = x2 * lax.rsqrt(ss * (1.0 / HEAD_DIM) + EPS) * g2
        if rope:
            xn = jnp.concatenate([_rope(xn[:, i * HEAD_DIM:(i + 1) * HEAD_DIM], cos_ref[...], sin_ref[...])
                                  for i in range(2)], axis=1)
        return (xn * scale).astype(BF16) if scale != 1.0 else xn.astype(BF16)

    group_w = GROUP * HEAD_DIM
    q_groups = [_dot(h, wq_ref[:, n0:n0 + group_w]) for n0 in range(0, Q_HEADS * HEAD_DIM, group_w)]
    kv = _dot(h, wkv_ref[...])
    for gi, res in enumerate(q_groups):
        for n0 in range(0, group_w, pair_w):
            q_ref[:, gi * group_w + n0:gi * group_w + n0 + pair_w] = head_pair(res[:, n0:n0 + pair_w], qg_ref[...],
                                                                               HEAD_DIM ** -0.5)
    kw = KV_HEADS * HEAD_DIM
    assert kw == pair_w
    kv_ref[:, 0:kw] = head_pair(kv[:, 0:kw], kg_ref[...], 1.0)
    ones = jnp.ones((kv.shape[0], HEAD_DIM), BF16)
    for j in range(KV_HEADS):
        v0 = kw + 2 * j * HEAD_DIM
        kv_ref[:, v0:v0 + HEAD_DIM] = kv[:, kw + j * HEAD_DIM:kw + (j + 1) * HEAD_DIM].astype(BF16)
        kv_ref[:, v0 + HEAD_DIM:v0 + 2 * HEAD_DIM] = ones


def _attn_in(x2d, mod, layer, mod_index, gain, w_q, w_kv, q_g, k_g, tm, rope_tables, seq, name):
    rows, d = x2d.shape
    vec = lambda: pl.BlockSpec((1, HEAD_DIM), lambda i: (0, 0))
    in_specs = [pl.BlockSpec((tm, d), lambda i: (i, 0)), _mod_spec(layer, d, mod_index),
                pl.BlockSpec((1, d), lambda i: (0, 0)),
                pl.BlockSpec(w_q.shape, lambda i: (0, 0)), pl.BlockSpec(w_kv.shape, lambda i: (0, 0)), vec(), vec(),
                pl.BlockSpec((2 * HEAD_DIM, 2 * HEAD_DIM), lambda i: (0, 0))]
    seg = jnp.asarray(np.kron(np.eye(2, dtype=np.float32), np.ones((HEAD_DIM, HEAD_DIM), np.float32)), BF16)
    args = [x2d, mod, gain.reshape(1, d), w_q, w_kv, q_g.reshape(1, HEAD_DIM), k_g.reshape(1, HEAD_DIM), seg]
    if rope_tables is not None:
        tpb = seq // tm
        in_specs += [pl.BlockSpec((tm, HEAD_DIM), lambda i: (i % tpb, 0))] * 2
        args += list(rope_tables)
    return pl.pallas_call(
        functools.partial(_attn_in_kernel, rope=rope_tables is not None),
        grid=(rows // tm,),
        in_specs=in_specs,
        out_specs=[pl.BlockSpec((tm, w_q.shape[1]), lambda i: (i, 0)),
                   pl.BlockSpec((tm, KV_OUT_WIDTH), lambda i: (i, 0))],
        out_shape=[jax.ShapeDtypeStruct((rows, w_q.shape[1]), BF16),
                   jax.ShapeDtypeStruct((rows, KV_OUT_WIDTH), BF16)],
        compiler_params=_cparams("arbitrary"),
        name=name,
    )(*args)


def _attn_kernel(q_ref, kvc_ref, *rest, has_latent, tk):
    if has_latent:
        kvl_ref, o0_ref, o1_ref, s0_sc, s1_sc, p0_sc, p1_sc, m1_sc = rest
    else:
        o0_ref, o1_ref, s0_sc, s1_sc, p0_sc, p1_sc, m1_sc = rest
    tq = q_ref.shape[0]
    rows = GROUP * tq
    kw = KV_HEADS * HEAD_DIM
    lanes = HEAD_DIM
    n_ctx = kvc_ref.shape[0]
    n_lat = kvl_ref.shape[0] if has_latent else 0

    @pl.when((pl.program_id(0) == 0) & (pl.program_id(1) == 0))
    def _():
        s1_sc[...] = jnp.zeros_like(s1_sc)
        m1_sc[...] = jnp.zeros_like(m1_sc)

    def lane_groups(s):
        return [s[:, c * lanes:(c + 1) * lanes] for c in range(s.shape[1] // lanes)]

    tkc = min(tk, n_ctx)
    tiles = [(kvc_ref, r0, tkc, r0) for r0 in range(0, n_ctx, tkc)]
    tiles += [(kvl_ref, r0, tk, n_ctx + r0) for r0 in range(0, n_lat, tk)]
    pv_parts = [(kvc_ref, 0, n_ctx, 0)]
    pv_parts += [(kvl_ref, r0, n_lat // PV_SPLITS, n_ctx + r0) for r0 in range(0, n_lat, max(n_lat // PV_SPLITS, 1))]

    def stage(hk_new, s_new_sc, hk_old, s_old_sc, m_old, p_sc, o_ref):
        ksl = slice(hk_new * HEAD_DIM, (hk_new + 1) * HEAD_DIM)
        vsl = slice(kw + 2 * hk_old * HEAD_DIM, kw + 2 * (hk_old + 1) * HEAD_DIM)
        qcat = jnp.concatenate([q_ref[:, (hk_new * GROUP + gi) * HEAD_DIM:(hk_new * GROUP + gi + 1) * HEAD_DIM]
                                for gi in range(GROUP)], axis=0)
        mrun = jnp.full((rows, lanes), -jnp.inf, F32)
        partial, pending = [], list(pv_parts)
        for kv_ref, r0, w, c0 in tiles:
            s = _dot_nt(qcat, kv_ref[r0:r0 + w, ksl])
            s_new_sc[:, c0:c0 + w] = s
            mrun = functools.reduce(jnp.maximum, [mrun] + lane_groups(s))
            p = [jnp.exp((sg - m_old).astype(BF16)) for sg in lane_groups(s_old_sc[:, c0:c0 + w])]
            p_sc[:, c0:c0 + w] = jnp.concatenate(p, axis=1)
            while pending and pending[0][3] + pending[0][2] <= c0 + w:
                pv_ref, pr0, pw, pc0 = pending.pop(0)
                partial.append(_dot(p_sc[:, pc0:pc0 + pw], pv_ref[pr0:pr0 + pw, vsl]))
        acc = functools.reduce(jnp.add, partial)
        o = acc[:, 0:HEAD_DIM] / acc[:, HEAD_DIM:HEAD_DIM + 1]
        for gi in range(GROUP):
            o_ref[:, gi * HEAD_DIM:(gi + 1) * HEAD_DIM] = o[gi * tq:(gi + 1) * tq].astype(o_ref.dtype)
        return jnp.broadcast_to(jnp.max(mrun, axis=-1, keepdims=True), (rows, lanes))

    m0 = stage(0, s0_sc, 1, s1_sc, m1_sc[...], p1_sc, o1_ref)
    m1_sc[...] = stage(1, s1_sc, 0, s0_sc, m0, p0_sc, o0_ref)


def _attention(q, kv_ctx, kv_lat, bsz, tq, tk):
    rows = q.shape[0]
    nq = rows // bsz // tq
    n_ctx = kv_ctx.shape[0] // bsz
    has_latent = kv_lat is not None
    n_keys = n_ctx + (kv_lat.shape[0] // bsz if has_latent else 0)
    assert n_ctx % min(tk, n_ctx) == 0 and (n_keys - n_ctx) % (tk * PV_SPLITS) == 0
    in_specs = [pl.BlockSpec((tq, q.shape[1]), lambda b, j: (b * nq + jnp.minimum(j, nq - 1), 0)),
                pl.BlockSpec((n_ctx, kv_ctx.shape[1]), lambda b, j: (b, 0))]
    args = [q, kv_ctx]
    if has_latent:
        in_specs.append(pl.BlockSpec((n_keys - n_ctx, kv_lat.shape[1]), lambda b, j: (b, 0)))
        args.append(kv_lat)
    group_w = GROUP * HEAD_DIM
    scores = lambda: pltpu.VMEM((GROUP * tq, n_keys), F32)
    probs = lambda: pltpu.VMEM((GROUP * tq, n_keys), BF16)
    return pl.pallas_call(
        functools.partial(_attn_kernel, has_latent=has_latent, tk=tk),
        grid=(bsz, nq + 1),
        in_specs=in_specs,
        out_specs=[pl.BlockSpec((tq, group_w), lambda b, j: (b * nq + jnp.minimum(j, nq - 1), 0)),
                   pl.BlockSpec((tq, group_w), lambda b, j: (b * nq + jnp.maximum(j - 1, 0), 0))],
        out_shape=[jax.ShapeDtypeStruct((rows, group_w), BF16), jax.ShapeDtypeStruct((rows, group_w), BF16)],
        scratch_shapes=[scores(), scores(), probs(), probs(), pltpu.VMEM((GROUP * tq, HEAD_DIM), F32)],
        compiler_params=_cparams("arbitrary", "arbitrary"),
        name="attention" if has_latent else "attention_ctx",
    )(*args)


def _rope_tables(seq):
    n_rows = seq // GRID_W
    row = jnp.repeat(jnp.arange(n_rows), GRID_W)
    col = jnp.tile(jnp.arange(GRID_W), n_rows)
    n_pairs_axis = HEAD_DIM // 4
    inv_freq = ROPE_THETA ** (-jnp.arange(n_pairs_axis, dtype=F32) / n_pairs_axis)
    ang = jnp.concatenate([row[:, None] * inv_freq, col[:, None] * inv_freq], axis=-1)
    cos, sin = jnp.cos(ang), jnp.sin(ang)
    return jnp.concatenate([cos, cos], axis=-1), jnp.concatenate([-sin, sin], axis=-1)


def _tile(n, pref):
    t = min(n, pref)
    assert n % t == 0, (n, pref)
    return t


def kernel(x, c, ctx, c_ctx, norm_mix_g, norm_mlp_g, w_mod, b_mod, w_mlp1, w_mlp2, gla_w_in, gla_w_up_f, gla_b_f, gla_w_up_b, gla_b_b, gla_norm_g, gla_w_o, lru_w_in, lru_conv_w, lru_conv_b, lru_wa_f, lru_ba_f, lru_wx_f, lru_bx_f, lru_lam_f, lru_wa_b, lru_ba_b, lru_wx_b, lru_bx_b, lru_lam_b, lru_w_o, attn_w_in, attn_q_g, attn_k_g, attn_w_o, final_g):
    bsz, seq, d = x.shape
    n_ctx = ctx.shape[1]
    depth = w_mod.shape[0]
    assert bsz + 1 <= MOD_ROWS and d == GLA_HEADS * GLA_DV == Q_HEADS * HEAD_DIM

    mod = _modulation(c, c_ctx, w_mod, b_mod)
    xl = x.reshape(bsz * seq, d)
    xc = ctx.reshape(bsz * n_ctx, d)

    tm_l = _tile(seq, 1024)
    tm_c = _tile(n_ctx, 256)
    lat_idx = lambda i: 1 + (i * tm_l) // seq
    ctx_idx = lambda i: 0

    for layer in range(depth):
        need_ctx = layer < depth - 1
        kind, j = layer % N_MIXERS, layer // N_MIXERS
        gmix = norm_mix_g[layer]

        if kind == 0:
            dq = GLA_HEADS * GLA_DK
            main_w = gla_w_in[j][:, :2 * dq + 2 * d].astype(BF16)
            gate_w = jnp.pad(gla_w_in[j][:, 2 * dq + 2 * d:], ((0, 0), (0, GLA_GATE_PAD - 2 * GLA_GATE_RANK))).astype(BF16)
            zpad = jnp.zeros((GLA_GATE_PAD - 2 * GLA_GATE_RANK, dq), F32)
            wup_f = jnp.concatenate([gla_w_up_f[j], jnp.zeros_like(gla_w_up_b[j]), zpad], axis=0).astype(BF16)
            wup_b = jnp.concatenate([jnp.zeros_like(gla_w_up_f[j]), gla_w_up_b[j], zpad], axis=0).astype(BF16)
            bup_f, bup_b = gla_b_f[j].reshape(1, dq), gla_b_b[j].reshape(1, dq)
            w_o = gla_w_o[j].astype(BF16)
            proj = {}
            for name, (xs, midx, tm, slen) in zip(("lat", "ctx"), [(xl, lat_idx, tm_l, seq), (xc, ctx_idx, tm_c, n_ctx)]):
                proj[name] = _ln_matmul(xs, mod, layer, midx, gmix, [main_w, gate_w], [BF16, BF16], tm,
                                        name="gla_in_" + name)
            s0 = jnp.zeros((bsz, GLA_HEADS, GLA_DK, GLA_DV), F32)
            ch_c, ch_l = _tile(n_ctx, GLA_CHUNK), _tile(seq, GLA_CHUNK)
            ocf, ocb, s_cf, s_cb = _gla_scan(*proj["ctx"], wup_f, wup_b, bup_f, bup_b, s0, s0, bsz, ch_c)
            olf, olb, _, _ = _gla_scan(*proj["lat"], wup_f, wup_b, bup_f, bup_b, s_cf, s_cb, bsz, ch_l)
            ng = gla_norm_g[j].reshape(1, GLA_DV)

            def gla_out(of, ob, qkvr, xs, midx, tm):
                blk = lambda i: (i, 0)
                return _proj_res("gla", [of, ob, qkvr], [pl.BlockSpec((tm, d), blk), pl.BlockSpec((tm, d), blk),
                                                        pl.BlockSpec((tm, d), lambda i: (i, (2 * dq + d) // d))],
                                 w_o, xs, mod, layer, midx, tm, extra=[ng],
                                 extra_specs=[pl.BlockSpec((1, GLA_DV), lambda i: (0, 0))])

            xl = gla_out(olf, olb, proj["lat"][0], xl, lat_idx, tm_l)
            if need_ctx:
                xc = gla_out(ocf, ocb, proj["ctx"][0], xc, ctx_idx, tm_c)

        elif kind == 1:
            w_in = lru_w_in[j].astype(BF16)
            w_o = lru_w_o[j].astype(BF16)
            wcat_f = (0.5 * jnp.concatenate([lru_wa_f[j], lru_wx_f[j]], axis=-1)).astype(BF16)
            wcat_b = (0.5 * jnp.concatenate([lru_wa_b[j], lru_wx_b[j]], axis=-1)).astype(BF16)
            p_l, = _ln_matmul(xl, mod, layer, lat_idx, gmix, [w_in], [BF16], tm_l, name="lru_in_lat")
            p_c, = _ln_matmul(xc, mod, layer, ctx_idx, gmix, [w_in], [BF16], tm_c, name="lru_in_ctx")
            h0 = jnp.zeros((bsz, d), F32)
            fwd = (lru_conv_w[j], lru_conv_b[j], wcat_f, 0.5 * lru_ba_f[j], 0.5 * lru_bx_f[j], lru_lam_f[j])
            bwd = (lru_conv_w[j], lru_conv_b[j], wcat_b, 0.5 * lru_ba_b[j], 0.5 * lru_bx_b[j], lru_lam_b[j])
            tt_c, tt_l = _tile(n_ctx, 128), _tile(seq, 128)
            p_c3, p_l3 = p_c.reshape(bsz, n_ctx, 2 * d), p_l.reshape(bsz, seq, 2 * d)
            hcf, s_cf = _lru_scan(p_c3, *fwd, h0, tt_c, False)
            hcb, s_cb = _lru_scan(p_c3, *bwd, h0, tt_c, True)
            hlf, _ = _lru_scan(p_l3, *fwd, s_cf, tt_l, False)
            hlb, _ = _lru_scan(p_l3, *bwd, s_cb, tt_l, True)

            def lru_out(hf, hb, p2d, xs, midx, tm):
                blk = lambda i: (i, 0)
                spec = lambda: pl.BlockSpec((tm, d), blk)
                return _proj_res("lru", [hf.reshape(-1, d), hb.reshape(-1, d), p2d], [spec(), spec(), spec()],
                                 w_o, xs, mod, layer, midx, tm)

            xl = lru_out(hlf, hlb, p_l, xl, lat_idx, tm_l)
            if need_ctx:
                xc = lru_out(hcf, hcb, p_c, xc, ctx_idx, tm_c)

        else:
            qw = Q_HEADS * HEAD_DIM
            w_q = attn_w_in[j][:, :qw].astype(BF16)
            w_kv = attn_w_in[j][:, qw:].astype(BF16)
            w_o = attn_w_o[j].astype(BF16)
            qg, kg = attn_q_g[j], attn_k_g[j]
            q_l, kv_l = _attn_in(xl, mod, layer, lat_idx, gmix, w_q, w_kv, qg, kg, tm_l, _rope_tables(seq), seq,
                                 "attn_in_lat")
            q_c, kv_c = _attn_in(xc, mod, layer, ctx_idx, gmix, w_q, w_kv, qg, kg, tm_c, None, n_ctx, "attn_in_ctx")
            tk = _tile(seq // PV_SPLITS, 512)
            o_l = _attention(q_l, kv_c, kv_l, bsz, _tile(seq, 128), tk)
            half = lambda tm: [pl.BlockSpec((tm, d // KV_HEADS), lambda i: (i, 0))] * KV_HEADS
            xl = _proj_res("plain", o_l, half(tm_l), w_o, xl, mod, layer, lat_idx, tm_l)
            if need_ctx:
                o_c = _attention(q_c, kv_c, None, bsz, _tile(n_ctx, 128), tk)
                xc = _proj_res("plain", o_c, half(tm_c), w_o, xc, mod, layer, ctx_idx, tm_c)

        w1, w2 = w_mlp1[layer].astype(BF16), w_mlp2[layer].astype(BF16)
        tf = _tile(w1.shape[1], 1024)
        tmm_l = _tile(seq, 1024)
        xl = _mlp(xl, mod, layer, lambda i: 1 + (i * tmm_l) // seq, norm_mlp_g[layer], w1, w2, tmm_l, tf,
                  final_gain=None if need_ctx else final_g)
        if need_ctx:
            xc = _mlp(xc, mod, layer, ctx_idx, norm_mlp_g[layer], w1, w2, _tile(bsz * n_ctx, 1024), tf)

    return xl.reshape(bsz, seq, d)
```

```python
import functools

import numpy as np
import jax
import jax.numpy as jnp
from jax import lax
from jax.experimental import pallas as pl
from jax.experimental.pallas import tpu as pltpu

F32 = jnp.float32
BF16 = jnp.bfloat16

EPS = 1e-6
N_MOD = 6
N_MIXERS = 3
MOD_ROWS = 16

GLA_HEADS = 4
GLA_DK = 128
GLA_DV = 256
GLA_GATE_RANK = 16
GLA_GATE_TAU = 16.0
GLA_GATE_PAD = 128

RNN_BLOCKS = 8
RNN_BLOCK_DIM = 128
CONV_WIDTH = 4
CONV_LEFT = 2
LRU_C = 8.0

HEAD_DIM = 128
Q_HEADS = 8
KV_HEADS = 2
GROUP = Q_HEADS // KV_HEADS
GRID_W = 64
ROPE_THETA = 10000.0

VMEM_LIMIT_BYTES = 56 * 1024 * 1024


def _cparams(*sem):
    return pltpu.CompilerParams(dimension_semantics=sem, vmem_limit_bytes=VMEM_LIMIT_BYTES)


def _dot(a, b):
    return jnp.dot(a, b, preferred_element_type=F32)


def _dot_nt(a, b):
    return lax.dot_general(a, b, (((1,), (1,)), ((), ())), preferred_element_type=F32)


def _dot_tn(a, b):
    return lax.dot_general(a, b, (((0,), (0,)), ((), ())), preferred_element_type=F32)


def _rms(x, g):
    return x * lax.rsqrt(jnp.mean(x * x, axis=-1, keepdims=True) + EPS) * g


def _adaln(x, g, shift, scale):
    return _rms(x, g) * (1.0 + scale) + shift


def _mod_kernel(s_ref, w_ref, b_ref, o_ref):
    s = s_ref[...]
    s = s * jax.nn.sigmoid(s)
    o_ref[...] = _dot(s.astype(BF16), w_ref[...].astype(BF16)) + b_ref[...]


def _modulation(c, c_ctx, w_mod, b_mod):
    depth, d, nd = w_mod.shape
    bsz = c.shape[0]
    rows = jnp.concatenate([c_ctx[None, :], c, jnp.zeros((MOD_ROWS - 1 - bsz, d), F32)], axis=0)
    tn = min(nd, 1536)
    out = pl.pallas_call(
        _mod_kernel,
        grid=(depth, nd // tn),
        in_specs=[pl.BlockSpec((MOD_ROWS, d), lambda l, j: (0, 0)),
                  pl.BlockSpec((None, d, tn), lambda l, j: (l, 0, j)),
                  pl.BlockSpec((None, 1, tn), lambda l, j: (l, 0, j))],
        out_specs=pl.BlockSpec((None, MOD_ROWS, tn), lambda l, j: (l, 0, j)),
        out_shape=jax.ShapeDtypeStruct((depth, MOD_ROWS, nd), F32),
        compiler_params=_cparams("arbitrary", "arbitrary"),
        name="modulation",
    )(rows, w_mod, b_mod.reshape(depth, 1, nd))
    return out.reshape(depth, MOD_ROWS, N_MOD, d)


def _mod_spec(layer, d, mod_index):
    return pl.BlockSpec((None, None, N_MOD, d), lambda i, *_: (layer, mod_index(i), 0, 0))


def _ln_matmul_kernel(x_ref, mod_ref, g_ref, *refs, n_out, mod_off, col_chunk):
    w_refs, o_refs = refs[:n_out], refs[n_out:]
    h = _adaln(x_ref[...], g_ref[...], mod_ref[mod_off:mod_off + 1, :],
               mod_ref[mod_off + 1:mod_off + 2, :]).astype(BF16)
    for w_ref, o_ref in zip(w_refs, o_refs):
        n = w_ref.shape[1]
        for n0 in range(0, n, col_chunk):
            n1 = min(n, n0 + col_chunk)
            o_ref[:, n0:n1] = _dot(h, w_ref[:, n0:n1]).astype(o_ref.dtype)


def _ln_matmul(x2d, mod, layer, mod_index, gain, ws, out_dtypes, tm, name="ln_matmul"):
    rows, d = x2d.shape
    in_specs = [pl.BlockSpec((tm, d), lambda i: (i, 0)), _mod_spec(layer, d, mod_index),
                pl.BlockSpec((1, d), lambda i: (0, 0))]
    in_specs += [pl.BlockSpec(w.shape, lambda i: (0, 0)) for w in ws]
    return pl.pallas_call(
        functools.partial(_ln_matmul_kernel, n_out=len(ws), mod_off=0, col_chunk=512),
        grid=(rows // tm,),
        in_specs=in_specs,
        out_specs=[pl.BlockSpec((tm, w.shape[1]), lambda i: (i, 0)) for w in ws],
        out_shape=[jax.ShapeDtypeStruct((rows, w.shape[1]), dt) for w, dt in zip(ws, out_dtypes)],
        compiler_params=_cparams("arbitrary"),
        name=name,
    )(x2d, mod, gain.reshape(1, d), *ws)


def _proj_res_kernel(y0_ref, y1_ref, w_ref, x_ref, mod_ref, o_ref):
    y = jnp.concatenate([y0_ref[...], y1_ref[...]], axis=1)
    o_ref[...] = x_ref[...] + mod_ref[2:3, :] * _dot(y, w_ref[...])


def _gla_proj_res_kernel(of_ref, ob_ref, r_ref, ng_ref, w_ref, x_ref, mod_ref, o_ref):
    o = of_ref[...].astype(F32) + ob_ref[...].astype(F32)
    r = r_ref[...].astype(F32)
    ys = []
    for h in range(GLA_HEADS):
        sl = slice(h * GLA_DV, (h + 1) * GLA_DV)
        rh = r[:, sl]
        ys.append(_rms(o[:, sl], ng_ref[...]) * (rh * jax.nn.sigmoid(rh)))
    y = jnp.concatenate(ys, axis=1).astype(BF16)
    o_ref[...] = x_ref[...] + mod_ref[2:3, :] * _dot(y, w_ref[...])


def _lru_proj_res_kernel(hf_ref, hb_ref, gate_ref, w_ref, x_ref, mod_ref, o_ref):
    h = hf_ref[...].astype(F32) + hb_ref[...].astype(F32)
    y = (h * jax.nn.gelu(gate_ref[...].astype(F32))).astype(BF16)
    o_ref[...] = x_ref[...] + mod_ref[2:3, :] * _dot(y, w_ref[...])


def _proj_res(kind, operands, operand_specs, w, x2d, mod, layer, mod_index, tm, extra=(), extra_specs=()):
    rows, d = x2d.shape
    body = {"plain": _proj_res_kernel, "gla": _gla_proj_res_kernel, "lru": _lru_proj_res_kernel}[kind]
    n_in = len(operands) + len(extra) + 1
    return pl.pallas_call(
        body,
        grid=(rows // tm,),
        in_specs=[*operand_specs, *extra_specs, pl.BlockSpec(w.shape, lambda i: (0, 0)),
                  pl.BlockSpec((tm, d), lambda i: (i, 0)), _mod_spec(layer, d, mod_index)],
        out_specs=pl.BlockSpec((tm, d), lambda i: (i, 0)),
        out_shape=jax.ShapeDtypeStruct((rows, d), F32),
        input_output_aliases={n_in: 0} if layer > 0 else {},
        compiler_params=_cparams("arbitrary"),
        name=kind + "_proj_res",
    )(*operands, *extra, w, x2d, mod)


def _mlp_kernel(x_ref, mod_ref, g_ref, w1_ref, w2_ref, *rest, final_norm):
    if final_norm:
        fg_ref, o_ref, h_sc, acc_sc = rest
    else:
        o_ref, h_sc, acc_sc = rest
    j = pl.program_id(1)

    @pl.when(j == 0)
    def _():
        h_sc[...] = _adaln(x_ref[...], g_ref[...], mod_ref[3:4, :], mod_ref[4:5, :]).astype(BF16)
        acc_sc[...] = jnp.zeros_like(acc_sc)

    a = jnp.maximum(_dot(h_sc[...], w1_ref[...]), 0.0)
    acc_sc[...] += _dot((a * a).astype(BF16), w2_ref[...])

    @pl.when(j == pl.num_programs(1) - 1)
    def _():
        y = x_ref[...] + mod_ref[5:6, :] * acc_sc[...]
        o_ref[...] = _rms(y, fg_ref[...]) if final_norm else y


def _mlp(x2d, mod, layer, mod_index, gain, w1, w2, tm, tf, final_gain=None):
    rows, d = x2d.shape
    dff = w1.shape[1]
    vec = lambda: pl.BlockSpec((1, d), lambda i, j: (0, 0))
    in_specs = [pl.BlockSpec((tm, d), lambda i, j: (i, 0)), _mod_spec(layer, d, mod_index), vec(),
                pl.BlockSpec((d, tf), lambda i, j: (0, j)),
                pl.BlockSpec((tf, d), lambda i, j: (j, 0))]
    args = [x2d, mod, gain.reshape(1, d), w1, w2]
    if final_gain is not None:
        in_specs.append(vec())
        args.append(final_gain.reshape(1, d))
    return pl.pallas_call(
        functools.partial(_mlp_kernel, final_norm=final_gain is not None),
        grid=(rows // tm, dff // tf),
        in_specs=in_specs,
        out_specs=pl.BlockSpec((tm, d), lambda i, j: (i, 0)),
        out_shape=jax.ShapeDtypeStruct((rows, d), F32),
        scratch_shapes=[pltpu.VMEM((tm, d), BF16), pltpu.VMEM((tm, d), F32)],
        input_output_aliases={0: 0},
        compiler_params=_cparams("arbitrary", "arbitrary"),
        name="mlp",
    )(*args)


GLA_CHUNK = 256
GLA_MATMUL_LEVEL_ROWS = 8
GLA_FAST_MAX_DECAY = 60.0


def _gla_tables(chunk, reverse):
    levels = int(np.log2(chunk))
    idx = np.arange(chunk)
    t, r = idx[:, None], idx[None, :]
    if not reverse:
        q_incl = (r <= t)
        k_rest = (r > t)
    else:
        q_incl = (r >= t)
        k_rest = (r < t)
    mats = [q_incl, k_rest]
    masks = [np.eye(chunk, dtype=bool)]
    for lv in range(1, levels + 1):
        m = 1 << (lv - 1)
        mid = (idx // (2 * m)) * (2 * m) + m
        upper = idx >= mid
        midc = mid[:, None]
        if not reverse:
            pat = np.where(upper[:, None], (r >= midc) & (r <= t), (r > t) & (r < midc))
            mask = (upper[:, None] & ~upper[None, :])
        else:
            pat = np.where(upper[:, None], (r >= midc) & (r < t), (r >= t) & (r < midc))
            mask = (~upper[:, None] & upper[None, :])
        mask = mask & ((idx[:, None] // (2 * m)) == (idx[None, :] // (2 * m)))
        if 2 * m <= GLA_MATMUL_LEVEL_ROWS:
            mats.append(pat)
        masks.append(mask)
    a = np.concatenate(mats, axis=0).astype(np.float32)
    a2 = np.concatenate([a, a], axis=1)
    return jnp.asarray(a2, BF16), jnp.asarray(np.stack(masks).astype(np.float32), F32)


def _gla_gates(qkvr_ref, gf_ref, wup_ref, bup_ref, a_ref, reverse):
    chunk = qkvr_ref.shape[0]
    dq = GLA_HEADS * GLA_DK
    graw = _dot(gf_ref[...], wup_ref[...]) + bup_ref[...]
    g = (jnp.minimum(graw, 0.0) - jnp.log(1.0 + jnp.exp(-jnp.abs(graw)))) * (1.0 / GLA_GATE_TAU)
    g_hi = g.astype(BF16)
    g_cat = jnp.concatenate([g_hi, (g - g_hi.astype(F32)).astype(BF16)], axis=0)

    def partial_sums(block):
        return _dot(a_ref[block * chunk:(block + 1) * chunk, :], g_cat)

    st = dict(partial_sums=partial_sums, bq=partial_sums(0), reverse=reverse, chunk=chunk,
              q_all=qkvr_ref[:, 0:dq].astype(F32) * (GLA_DK ** -0.5), k_all=qkvr_ref[:, dq:2 * dq].astype(F32))
    st["qd"] = (st["q_all"] * jnp.exp(st["bq"])).astype(BF16)
    st["btot"] = st["bq"][0:1, :] if reverse else st["bq"][chunk - 1:chunk, :]
    return st


def _gla_levels(st, z_sc, lv_range):
    chunk, reverse, bq, bk = st["chunk"], st["reverse"], st["bq"], st["bk"]
    q_all, k_all = st["q_all"], st["k_all"]
    row = lax.broadcasted_iota(jnp.int32, (chunk, 1), 0)
    for lv in lv_range:
        m = 1 << (lv - 1)
        if 2 * m <= GLA_MATMUL_LEVEL_ROWS:
            use_q = ((row // m) % 2) == (0 if reverse else 1)
            z = jnp.where(use_q, q_all, k_all) * jnp.exp(st["partial_sums"](1 + lv))
        else:
            pieces = []
            for lo in range(0, chunk, 2 * m):
                mid, hi = lo + m, lo + 2 * m
                if not reverse:
                    pieces.append(k_all[lo:mid] * jnp.exp(bk[lo:mid] - bk[mid - 1:mid]))
                    pieces.append(q_all[mid:hi] * jnp.exp(bq[mid:hi] - bq[mid - 1:mid]))
                else:
                    pieces.append(q_all[lo:mid] * jnp.exp(bq[lo:mid] - bq[mid:mid + 1]))
                    pieces.append(k_all[mid:hi] * jnp.exp(bk[mid:hi] - bk[mid:mid + 1]))
            z = jnp.concatenate(pieces, axis=0)
        z_sc[lv - 1] = z.astype(BF16)


def _gla_scores_by_level(st, h, mask_ref, z_sc, levels):
    sl = slice(h * GLA_DK, (h + 1) * GLA_DK)
    chunk, reverse = st["chunk"], st["reverse"]
    sub = 8
    diag = mask_ref[0] * jnp.sum(st["q_all"][:, sl] * st["k_all"][:, sl], axis=-1, keepdims=True)
    tiles = [diag[r:r + sub] for r in range(0, chunk, sub)]
    for lv in range(1, levels + 1):
        m = 1 << (lv - 1)
        if 2 * m <= GLA_MATMUL_LEVEL_ROWS:
            q_rows = [(0, chunk)]
        else:
            q_rows = [(lo, lo + m) if reverse else (lo + m, lo + 2 * m) for lo in range(0, chunk, 2 * m)]
        zq = jnp.concatenate([z_sc[lv - 1, lo:hi, sl] for lo, hi in q_rows], axis=0)
        mq = jnp.concatenate([mask_ref[lv, lo:hi, :] for lo, hi in q_rows], axis=0)
        part = mq * _dot_nt(zq, z_sc[lv - 1, :, sl])
        src = 0
        for lo, hi in q_rows:
            for r in range(lo, hi, sub):
                tiles[r // sub] = tiles[r // sub] + part[src:src + sub]
                src += sub
    return jnp.concatenate(tiles, axis=0)


def _gla_head_update(st, h, scores, qkvr_ref, s_sc, o_ref):
    dq = GLA_HEADS * GLA_DK
    sl = slice(h * GLA_DK, (h + 1) * GLA_DK)
    vh = qkvr_ref[:, 2 * dq + h * GLA_DV:2 * dq + (h + 1) * GLA_DV]
    s_old = s_sc[h]
    o_ref[:, h * GLA_DV:(h + 1) * GLA_DV] = (_dot(scores.astype(BF16), vh)
                                            + _dot(st["qd"][:, sl], s_old.astype(BF16))).astype(o_ref.dtype)
    etot = jnp.exp(st["btot"][:, sl])
    ecol = jnp.transpose(jnp.broadcast_to(etot, (GLA_DK, GLA_DK)))
    s_sc[h] = s_old * jnp.concatenate([ecol] * (GLA_DV // GLA_DK), axis=1) + _dot_tn(st["kd"][:, sl], vh)


def _gla_kernel(qkvr_f, gf_f, qkvr_b, gf_b, wup_f, wup_b, bup_f, bup_b, a_f, a_b, mask_f, mask_b, pair_f, pair_b,
                s0_f, s0_b, o_f, o_b, sfin_f, sfin_b, s_sc_f, s_sc_b, z_sc_f, z_sc_b):
    c = pl.program_id(1)
    levels = mask_f.shape[0] - 1
    dirs = ((qkvr_f, mask_f, pair_f, z_sc_f, s_sc_f, o_f), (qkvr_b, mask_b, pair_b, z_sc_b, s_sc_b, o_b))

    @pl.when(c == 0)
    def _():
        s_sc_f[...] = s0_f[...]
        s_sc_b[...] = s0_b[...]

    sts = (_gla_gates(qkvr_f, gf_f, wup_f, bup_f, a_f, False), _gla_gates(qkvr_b, gf_b, wup_b, bup_b, a_b, True))
    worst = jnp.maximum(jnp.max(jnp.abs(sts[0]["btot"])), jnp.max(jnp.abs(sts[1]["btot"])))
    small_decay = worst <= GLA_FAST_MAX_DECAY

    @pl.when(small_decay)
    def _():
        kinv = []
        for st in sts:
            kexp = st["k_all"] * jnp.exp(-st["bq"])
            kinv.append(kexp.astype(BF16))
            st["kd"] = (kexp * jnp.exp(st["btot"])).astype(BF16)
        for h in range(GLA_HEADS):
            sl = slice(h * GLA_DK, (h + 1) * GLA_DK)
            for st, kv, (qkvr, _, pair, _, s_sc, o_ref) in zip(sts, kinv, dirs):
                scores = pair[...] * _dot_nt(st["qd"][:, sl], kv[:, sl])
                _gla_head_update(st, h, scores, qkvr, s_sc, o_ref)

    @pl.when(jnp.logical_not(small_decay))
    def _():
        for st, (_, _, _, z_sc, _, _) in zip(sts, dirs):
            st["bk"] = st["partial_sums"](1)
            st["kd"] = (st["k_all"] * jnp.exp(st["bk"])).astype(BF16)
            _gla_levels(st, z_sc, range(1, levels + 1))
        for h in range(GLA_HEADS):
            for st, (qkvr, mask, _, z_sc, s_sc, o_ref) in zip(sts, dirs):
                _gla_head_update(st, h, _gla_scores_by_level(st, h, mask, z_sc, levels), qkvr, s_sc, o_ref)

    @pl.when(c == pl.num_programs(1) - 1)
    def _():
        sfin_f[...] = s_sc_f[...]
        sfin_b[...] = s_sc_b[...]


def _gla_scan(qkvr, gfeat, wup_f, wup_b, bup_f, bup_b, s0_f, s0_b, bsz, chunk):
    rows = qkvr.shape[0]
    nc = rows // bsz // chunk
    a_f, mask_f = _gla_tables(chunk, False)
    a_b, mask_b = _gla_tables(chunk, True)
    tri = np.tril(np.ones((chunk, chunk), np.float32))
    pair_f, pair_b = jnp.asarray(tri), jnp.asarray(tri.T)
    dv = GLA_HEADS * GLA_DV
    fwd_block = lambda b, c: (b * nc + c, 0)
    bwd_block = lambda b, c: (b * nc + nc - 1 - c, 0)
    const2 = lambda x: pl.BlockSpec(x.shape, lambda b, c: (0, 0))
    const3 = lambda x: pl.BlockSpec(x.shape, lambda b, c: (0, 0, 0))
    state_spec = pl.BlockSpec((None, GLA_HEADS, GLA_DK, GLA_DV), lambda b, c: (b, 0, 0, 0))
    state_shape = jax.ShapeDtypeStruct((bsz, GLA_HEADS, GLA_DK, GLA_DV), F32)
    state_sc = lambda: pltpu.VMEM((GLA_HEADS, GLA_DK, GLA_DV), F32)
    z_sc = lambda: pltpu.VMEM((mask_f.shape[0] - 1, chunk, GLA_HEADS * GLA_DK), BF16)
    return pl.pallas_call(
        _gla_kernel,
        grid=(bsz, nc),
        in_specs=[pl.BlockSpec((chunk, qkvr.shape[1]), fwd_block), pl.BlockSpec((chunk, gfeat.shape[1]), fwd_block),
                  pl.BlockSpec((chunk, qkvr.shape[1]), bwd_block), pl.BlockSpec((chunk, gfeat.shape[1]), bwd_block),
                  const2(wup_f), const2(wup_b), const2(bup_f), const2(bup_b), const2(a_f), const2(a_b),
                  const3(mask_f), const3(mask_b), const2(pair_f), const2(pair_b), state_spec, state_spec],
        out_specs=[pl.BlockSpec((chunk, dv), fwd_block), pl.BlockSpec((chunk, dv), bwd_block), state_spec, state_spec],
        out_shape=[jax.ShapeDtypeStruct((rows, dv), BF16), jax.ShapeDtypeStruct((rows, dv), BF16),
                   state_shape, state_shape],
        scratch_shapes=[state_sc(), state_sc(), z_sc(), z_sc()],
        compiler_params=_cparams("arbitrary", "arbitrary"),
        name="gla_scan",
    )(qkvr, gfeat, qkvr, gfeat, wup_f, wup_b, bup_f, bup_b, a_f, a_b, mask_f, mask_b, pair_f, pair_b, s0_f, s0_b)


LRU_HALO = 16
CONV_SHIFTS = tuple(j - CONV_LEFT for j in range(CONV_WIDTH) if j != CONV_LEFT)
F32_TINY = float(np.finfo(np.float32).tiny)


def _lru_shift_matrix(tt):
    sel = np.zeros((len(CONV_SHIFTS) * tt, tt + 2 * LRU_HALO), np.float32)
    for i, off in enumerate(CONV_SHIFTS):
        sel[i * tt + np.arange(tt), LRU_HALO + np.arange(tt) + off] = 1.0
    return jnp.asarray(sel, BF16)


def _lru_kernel(x_ref, prev_ref, next_ref, shift_ref, cw_ref, cb_ref, wcat_ref, ba_ref, bx_ref, lam_ref, h0_ref,
                hs_ref, hfin_ref, xs_sc, xc_sc, a_sc, u_sc, hs_sc, h_sc, *, reverse):
    i = pl.program_id(0)
    n = pl.num_programs(0)
    tidx = n - 1 - i if reverse else i
    bsz, tt, width = x_ref.shape
    bd = RNN_BLOCK_DIM

    @pl.when(i == 0)
    def _():
        h_sc[...] = h0_ref[...]

    xs_sc[:, 0:LRU_HALO, :] = jnp.where(tidx > 0, prev_ref[...], jnp.zeros_like(prev_ref))
    xs_sc[:, LRU_HALO:LRU_HALO + tt, :] = x_ref[...]
    xs_sc[:, LRU_HALO + tt:2 * LRU_HALO + tt, :] = jnp.where(tidx < n - 1, next_ref[...], jnp.zeros_like(next_ref))
    for b in range(bsz):
        shifted = _dot(shift_ref[...], xs_sc[b])
        xc = cb_ref[...] + cw_ref[CONV_LEFT:CONV_LEFT + 1, :] * x_ref[b].astype(F32)
        for k, off in enumerate(CONV_SHIFTS):
            j = off + CONV_LEFT
            xc = xc + cw_ref[j:j + 1, :] * shifted[k * tt:(k + 1) * tt]
        xc_sc[b] = xc

    lam = lam_ref[...]
    neg4sp = -0.5 * LRU_C * (jnp.maximum(-lam, 0.0) + jnp.log1p(jnp.exp(-jnp.abs(lam))))
    for nb in range(RNN_BLOCKS):
        sl = slice(nb * bd, (nb + 1) * bd)
        xc2 = xc_sc[:, :, sl].reshape(bsz * tt, bd)
        ri = _dot(xc2.astype(BF16), wcat_ref[nb])
        log_a = neg4sp[:, sl] + neg4sp[:, sl] * jnp.tanh(ri[:, :bd] + ba_ref[:, sl])
        ig = 0.5 + 0.5 * jnp.tanh(ri[:, bd:] + bx_ref[:, sl])
        th = jnp.tanh(log_a)
        one_minus_a2 = (-2.0 * th) / (1.0 - th)
        root = one_minus_a2 * lax.rsqrt(jnp.maximum(one_minus_a2, F32_TINY))
        a = jnp.exp(log_a)
        u = root * (ig * xc2)
        for b in range(bsz):
            a_sc[nb, pl.ds(b, tt, stride=bsz), :] = a[b * tt:(b + 1) * tt]
            u_sc[nb, pl.ds(b, tt, stride=bsz), :] = u[b * tt:(b + 1) * tt]

    def step(k, hs):
        t = tt - 1 - k if reverse else k
        rows = pl.ds(pl.multiple_of(t * bsz, bsz), bsz)
        new = []
        for nb in range(RNN_BLOCKS):
            h = a_sc[nb, rows, :] * hs[nb] + u_sc[nb, rows, :]
            hs_sc[nb, rows, :] = h
            new.append(h)
        return tuple(new)

    h_init = tuple(h_sc[:, nb * bd:(nb + 1) * bd] for nb in range(RNN_BLOCKS))
    h_last = lax.fori_loop(0, tt, step, h_init, unroll=4)
    for nb in range(RNN_BLOCKS):
        sl = slice(nb * bd, (nb + 1) * bd)
        h_sc[:, sl] = h_last[nb]
        for b in range(bsz):
            hs_ref[b, :, sl] = hs_sc[nb, pl.ds(b, tt, stride=bsz), :].astype(hs_ref.dtype)

    @pl.when(i == n - 1)
    def _():
        hfin_ref[...] = h_sc[...]


def _lru_scan(p3, conv_w, conv_b, wcat, ba, bx, lam, h0, tt, reverse):
    bsz, seq, two_w = p3.shape
    width = two_w // 2
    n = seq // tt
    hb = tt // LRU_HALO
    assert tt % LRU_HALO == 0 and width == RNN_BLOCKS * RNN_BLOCK_DIM
    shift = _lru_shift_matrix(tt)

    def tix(i):
        return n - 1 - i if reverse else i

    vec = lambda: pl.BlockSpec((1, width), lambda i: (0, 0))
    slab = lambda: pltpu.VMEM((RNN_BLOCKS, tt * bsz, RNN_BLOCK_DIM), F32)
    return pl.pallas_call(
        functools.partial(_lru_kernel, reverse=reverse),
        grid=(n,),
        in_specs=[pl.BlockSpec((bsz, tt, width), lambda i: (0, tix(i), 1)),
                  pl.BlockSpec((bsz, LRU_HALO, width), lambda i: (0, jnp.maximum(tix(i) * hb - 1, 0), 1)),
                  pl.BlockSpec((bsz, LRU_HALO, width), lambda i: (0, jnp.minimum((tix(i) + 1) * hb, seq // LRU_HALO - 1), 1)),
                  pl.BlockSpec(shift.shape, lambda i: (0, 0)),
                  pl.BlockSpec((CONV_WIDTH, width), lambda i: (0, 0)), vec(),
                  pl.BlockSpec(wcat.shape, lambda i: (0, 0, 0)), vec(), vec(), vec(),
                  pl.BlockSpec((bsz, width), lambda i: (0, 0))],
        out_specs=[pl.BlockSpec((bsz, tt, width), lambda i: (0, tix(i), 0)),
                   pl.BlockSpec((bsz, width), lambda i: (0, 0))],
        out_shape=[jax.ShapeDtypeStruct((bsz, seq, width), BF16),
                   jax.ShapeDtypeStruct((bsz, width), F32)],
        scratch_shapes=[pltpu.VMEM((bsz, tt + 2 * LRU_HALO, width), BF16), pltpu.VMEM((bsz, tt, width), F32),
                        slab(), slab(), slab(), pltpu.VMEM((bsz, width), F32)],
        compiler_params=_cparams("arbitrary"),
        name="lru_scan_bwd" if reverse else "lru_scan_fwd",
    )(p3, p3, p3, shift, conv_w, conv_b.reshape(1, width), wcat, ba.reshape(1, width),
      bx.reshape(1, width), lam.reshape(1, width), h0)


def _rope(x, cos, sin):
    return x * cos + pltpu.roll(x, HEAD_DIM // 2, axis=1) * sin


KV_OUT_WIDTH = 3 * KV_HEADS * HEAD_DIM
PV_SPLITS = 4


def _attn_in_kernel(x_ref, mod_ref, g_ref, wq_ref, wkv_ref, qg_ref, kg_ref, seg_ref, *rest, rope):
    if rope:
        cos_ref, sin_ref, q_ref, kv_ref = rest
    else:
        q_ref, kv_ref = rest
    h = _adaln(x_ref[...], g_ref[...], mod_ref[0:1, :], mod_ref[1:2, :]).astype(BF16)
    pair_w = 2 * HEAD_DIM

    def head_pair(x2, gain, scale):
        sq = x2 * x2
        sq_hi = sq.astype(BF16)
        ss = _dot(sq_hi, seg_ref[...]) + _dot((sq - sq_hi.astype(F32)).astype(BF16), seg_ref[...])
        g2 = jnp.concatenate([gain, gain], axis=1)
        xn = x2 * lax.rsqrt(ss * (1.0 / HEAD_DIM) + EPS) * g2
        if rope:
            xn = jnp.concatenate([_rope(xn[:, i * HEAD_DIM:(i + 1) * HEAD_DIM], cos_ref[...], sin_ref[...])
                                  for i in range(2)], axis=1)
        return (xn * scale).astype(BF16) if scale != 1.0 else xn.astype(BF16)

    group_w = GROUP * HEAD_DIM
    q_groups = [_dot(h, wq_ref[:, n0:n0 + group_w]) for n0 in range(0, Q_HEADS * HEAD_DIM, group_w)]
    kv = _dot(h, wkv_ref[...])
    for gi, res in enumerate(q_groups):
        for n0 in range(0, group_w, pair_w):
            q_ref[:, gi * group_w + n0:gi * group_w + n0 + pair_w] = head_pair(res[:, n0:n0 + pair_w], qg_ref[...],
                                                                               HEAD_DIM ** -0.5)
    kw = KV_HEADS * HEAD_DIM
    assert kw == pair_w
    kv_ref[:, 0:kw] = head_pair(kv[:, 0:kw], kg_ref[...], 1.0)
    ones = jnp.ones((kv.shape[0], HEAD_DIM), BF16)
    for j in range(KV_HEADS):
        v0 = kw + 2 * j * HEAD_DIM
        kv_ref[:, v0:v0 + HEAD_DIM] = kv[:, kw + j * HEAD_DIM:kw + (j + 1) * HEAD_DIM].astype(BF16)
        kv_ref[:, v0 + HEAD_DIM:v0 + 2 * HEAD_DIM] = ones


def _attn_in(x2d, mod, layer, mod_index, gain, w_q, w_kv, q_g, k_g, tm, rope_tables, seq, name):
    rows, d = x2d.shape
    vec = lambda: pl.BlockSpec((1, HEAD_DIM), lambda i: (0, 0))
    in_specs = [pl.BlockSpec((tm, d), lambda i: (i, 0)), _mod_spec(layer, d, mod_index),
                pl.BlockSpec((1, d), lambda i: (0, 0)),
                pl.BlockSpec(w_q.shape, lambda i: (0, 0)), pl.BlockSpec(w_kv.shape, lambda i: (0, 0)), vec(), vec(),
                pl.BlockSpec((2 * HEAD_DIM, 2 * HEAD_DIM), lambda i: (0, 0))]
    seg = jnp.asarray(np.kron(np.eye(2, dtype=np.float32), np.ones((HEAD_DIM, HEAD_DIM), np.float32)), BF16)
    args = [x2d, mod, gain.reshape(1, d), w_q, w_kv, q_g.reshape(1, HEAD_DIM), k_g.reshape(1, HEAD_DIM), seg]
    if rope_tables is not None:
        tpb = seq // tm
        in_specs += [pl.BlockSpec((tm, HEAD_DIM), lambda i: (i % tpb, 0))] * 2
        args += list(rope_tables)
    return pl.pallas_call(
        functools.partial(_attn_in_kernel, rope=rope_tables is not None),
        grid=(rows // tm,),
        in_specs=in_specs,
        out_specs=[pl.BlockSpec((tm, w_q.shape[1]), lambda i: (i, 0)),
                   pl.BlockSpec((tm, KV_OUT_WIDTH), lambda i: (i, 0))],
        out_shape=[jax.ShapeDtypeStruct((rows, w_q.shape[1]), BF16),
                   jax.ShapeDtypeStruct((rows, KV_OUT_WIDTH), BF16)],
        compiler_params=_cparams("arbitrary"),
        name=name,
    )(*args)


def _attn_kernel(q_ref, kvc_ref, *rest, has_latent, tk):
    if has_latent:
        kvl_ref, o0_ref, o1_ref, s0_sc, s1_sc, p0_sc, p1_sc, m1_sc = rest
    else:
        o0_ref, o1_ref, s0_sc, s1_sc, p0_sc, p1_sc, m1_sc = rest
    tq = q_ref.shape[0]
    rows = GROUP * tq
    kw = KV_HEADS * HEAD_DIM
    lanes = HEAD_DIM
    n_ctx = kvc_ref.shape[0]
    n_lat = kvl_ref.shape[0] if has_latent else 0

    @pl.when((pl.program_id(0) == 0) & (pl.program_id(1) == 0))
    def _():
        s1_sc[...] = jnp.zeros_like(s1_sc)
        m1_sc[...] = jnp.zeros_like(m1_sc)

    def lane_groups(s):
        return [s[:, c * lanes:(c + 1) * lanes] for c in range(s.shape[1] // lanes)]

    tkc = min(tk, n_ctx)
    tiles = [(kvc_ref, r0, tkc, r0) for r0 in range(0, n_ctx, tkc)]
    tiles += [(kvl_ref, r0, tk, n_ctx + r0) for r0 in range(0, n_lat, tk)]
    pv_parts = [(kvc_ref, 0, n_ctx, 0)]
    pv_parts += [(kvl_ref, r0, n_lat // PV_SPLITS, n_ctx + r0) for r0 in range(0, n_lat, max(n_lat // PV_SPLITS, 1))]

    def stage(hk_new, s_new_sc, hk_old, s_old_sc, m_old, p_sc, o_ref):
        ksl = slice(hk_new * HEAD_DIM, (hk_new + 1) * HEAD_DIM)
        vsl = slice(kw + 2 * hk_old * HEAD_DIM, kw + 2 * (hk_old + 1) * HEAD_DIM)
        qcat = jnp.concatenate([q_ref[:, (hk_new * GROUP + gi) * HEAD_DIM:(hk_new * GROUP + gi + 1) * HEAD_DIM]
                                for gi in range(GROUP)], axis=0)
        mrun = jnp.full((rows, lanes), -jnp.inf, F32)
        partial, pending = [], list(pv_parts)
        for kv_ref, r0, w, c0 in tiles:
            s = _dot_nt(qcat, kv_ref[r0:r0 + w, ksl])
            s_new_sc[:, c0:c0 + w] = s
            mrun = functools.reduce(jnp.maximum, [mrun] + lane_groups(s))
            p = [jnp.exp((sg - m_old).astype(BF16)) for sg in lane_groups(s_old_sc[:, c0:c0 + w])]
            p_sc[:, c0:c0 + w] = jnp.concatenate(p, axis=1)
            while pending and pending[0][3] + pending[0][2] <= c0 + w:
                pv_ref, pr0, pw, pc0 = pending.pop(0)
                partial.append(_dot(p_sc[:, pc0:pc0 + pw], pv_ref[pr0:pr0 + pw, vsl]))
        acc = functools.reduce(jnp.add, partial)
        o = acc[:, 0:HEAD_DIM] / acc[:, HEAD_DIM:HEAD_DIM + 1]
        for gi in range(GROUP):
            o_ref[:, gi * HEAD_DIM:(gi + 1) * HEAD_DIM] = o[gi * tq:(gi + 1) * tq].astype(o_ref.dtype)
        return jnp.broadcast_to(jnp.max(mrun, axis=-1, keepdims=True), (rows, lanes))

    m0 = stage(0, s0_sc, 1, s1_sc, m1_sc[...], p1_sc, o1_ref)
    m1_sc[...] = stage(1, s1_sc, 0, s0_sc, m0, p0_sc, o0_ref)


def _attention(q, kv_ctx, kv_lat, bsz, tq, tk):
    rows = q.shape[0]
    nq = rows // bsz // tq
    n_ctx = kv_ctx.shape[0] // bsz
    has_latent = kv_lat is not None
    n_keys = n_ctx + (kv_lat.shape[0] // bsz if has_latent else 0)
    assert n_ctx % min(tk, n_ctx) == 0 and (n_keys - n_ctx) % (tk * PV_SPLITS) == 0
    in_specs = [pl.BlockSpec((tq, q.shape[1]), lambda b, j: (b * nq + jnp.minimum(j, nq - 1), 0)),
                pl.BlockSpec((n_ctx, kv_ctx.shape[1]), lambda b, j: (b, 0))]
    args = [q, kv_ctx]
    if has_latent:
        in_specs.append(pl.BlockSpec((n_keys - n_ctx, kv_lat.shape[1]), lambda b, j: (b, 0)))
        args.append(kv_lat)
    group_w = GROUP * HEAD_DIM
    scores = lambda: pltpu.VMEM((GROUP * tq, n_keys), F32)
    probs = lambda: pltpu.VMEM((GROUP * tq, n_keys), BF16)
    return pl.pallas_call(
        functools.partial(_attn_kernel, has_latent=has_latent, tk=tk),
        grid=(bsz, nq + 1),
        in_specs=in_specs,
        out_specs=[pl.BlockSpec((tq, group_w), lambda b, j: (b * nq + jnp.minimum(j, nq - 1), 0)),
                   pl.BlockSpec((tq, group_w), lambda b, j: (b * nq + jnp.maximum(j - 1, 0), 0))],
        out_shape=[jax.ShapeDtypeStruct((rows, group_w), BF16), jax.ShapeDtypeStruct((rows, group_w), BF16)],
        scratch_shapes=[scores(), scores(), probs(), probs(), pltpu.VMEM((GROUP * tq, HEAD_DIM), F32)],
        compiler_params=_cparams("arbitrary", "arbitrary"),
        name="attention" if has_latent else "attention_ctx",
    )(*args)


def _rope_tables(seq):
    n_rows = seq // GRID_W
    row = jnp.repeat(jnp.arange(n_rows), GRID_W)
    col = jnp.tile(jnp.arange(GRID_W), n_rows)
    n_pairs_axis = HEAD_DIM // 4
    inv_freq = ROPE_THETA ** (-jnp.arange(n_pairs_axis, dtype=F32) / n_pairs_axis)
    ang = jnp.concatenate([row[:, None] * inv_freq, col[:, None] * inv_freq], axis=-1)
    cos, sin = jnp.cos(ang), jnp.sin(ang)
    return jnp.concatenate([cos, cos], axis=-1), jnp.concatenate([-sin, sin], axis=-1)


def _tile(n, pref):
    t = min(n, pref)
    assert n % t == 0, (n, pref)
    return t


def kernel(x, c, ctx, c_ctx, norm_mix_g, norm_mlp_g, w_mod, b_mod, w_mlp1, w_mlp2, gla_w_in, gla_w_up_f, gla_b_f, gla_w_up_b, gla_b_b, gla_norm_g, gla_w_o, lru_w_in, lru_conv_w, lru_conv_b, lru_wa_f, lru_ba_f, lru_wx_f, lru_bx_f, lru_lam_f, lru_wa_b, lru_ba_b, lru_wx_b, lru_bx_b, lru_lam_b, lru_w_o, attn_w_in, attn_q_g, attn_k_g, attn_w_o, final_g):
    bsz, seq, d = x.shape
    n_ctx = ctx.shape[1]
    depth = w_mod.shape[0]
    assert bsz + 1 <= MOD_ROWS and d == GLA_HEADS * GLA_DV == Q_HEADS * HEAD_DIM

    mod = _modulation(c, c_ctx, w_mod, b_mod)
    xl = x.reshape(bsz * seq, d)
    xc = ctx.reshape(bsz * n_ctx, d)

    tm_l = _tile(seq, 1024)
    tm_c = _tile(n_ctx, 256)
    lat_idx = lambda i: 1 + (i * tm_l) // seq
    ctx_idx = lambda i: 0

    for layer in range(depth):
        need_ctx = layer < depth - 1
        kind, j = layer % N_MIXERS, layer // N_MIXERS
        gmix = norm_mix_g[layer]

        if kind == 0:
            dq = GLA_HEADS * GLA_DK
            main_w = gla_w_in[j][:, :2 * dq + 2 * d].astype(BF16)
            gate_w = jnp.pad(gla_w_in[j][:, 2 * dq + 2 * d:], ((0, 0), (0, GLA_GATE_PAD - 2 * GLA_GATE_RANK))).astype(BF16)
            zpad = jnp.zeros((GLA_GATE_PAD - 2 * GLA_GATE_RANK, dq), F32)
            wup_f = jnp.concatenate([gla_w_up_f[j], jnp.zeros_like(gla_w_up_b[j]), zpad], axis=0).astype(BF16)
            wup_b = jnp.concatenate([jnp.zeros_like(gla_w_up_f[j]), gla_w_up_b[j], zpad], axis=0).astype(BF16)
            bup_f, bup_b = gla_b_f[j].reshape(1, dq), gla_b_b[j].reshape(1, dq)
            w_o = gla_w_o[j].astype(BF16)
            proj = {}
            for name, (xs, midx, tm, slen) in zip(("lat", "ctx"), [(xl, lat_idx, tm_l, seq), (xc, ctx_idx, tm_c, n_ctx)]):
                proj[name] = _ln_matmul(xs, mod, layer, midx, gmix, [main_w, gate_w], [BF16, BF16], tm,
                                        name="gla_in_" + name)
            s0 = jnp.zeros((bsz, GLA_HEADS, GLA_DK, GLA_DV), F32)
            ch_c, ch_l = _tile(n_ctx, GLA_CHUNK), _tile(seq, GLA_CHUNK)
            ocf, ocb, s_cf, s_cb = _gla_scan(*proj["ctx"], wup_f, wup_b, bup_f, bup_b, s0, s0, bsz, ch_c)
            olf, olb, _, _ = _gla_scan(*proj["lat"], wup_f, wup_b, bup_f, bup_b, s_cf, s_cb, bsz, ch_l)
            ng = gla_norm_g[j].reshape(1, GLA_DV)

            def gla_out(of, ob, qkvr, xs, midx, tm):
                blk = lambda i: (i, 0)
                return _proj_res("gla", [of, ob, qkvr], [pl.BlockSpec((tm, d), blk), pl.BlockSpec((tm, d), blk),
                                                        pl.BlockSpec((tm, d), lambda i: (i, (2 * dq + d) // d))],
                                 w_o, xs, mod, layer, midx, tm, extra=[ng],
                                 extra_specs=[pl.BlockSpec((1, GLA_DV), lambda i: (0, 0))])

            xl = gla_out(olf, olb, proj["lat"][0], xl, lat_idx, tm_l)
            if need_ctx:
                xc = gla_out(ocf, ocb, proj["ctx"][0], xc, ctx_idx, tm_c)

        elif kind == 1:
            w_in = lru_w_in[j].astype(BF16)
            w_o = lru_w_o[j].astype(BF16)
            wcat_f = (0.5 * jnp.concatenate([lru_wa_f[j], lru_wx_f[j]], axis=-1)).astype(BF16)
            wcat_b = (0.5 * jnp.concatenate([lru_wa_b[j], lru_wx_b[j]], axis=-1)).astype(BF16)
            p_l, = _ln_matmul(xl, mod, layer, lat_idx, gmix, [w_in], [BF16], tm_l, name="lru_in_lat")
            p_c, = _ln_matmul(xc, mod, layer, ctx_idx, gmix, [w_in], [BF16], tm_c, name="lru_in_ctx")
            h0 = jnp.zeros((bsz, d), F32)
            fwd = (lru_conv_w[j], lru_conv_b[j], wcat_f, 0.5 * lru_ba_f[j], 0.5 * lru_bx_f[j], lru_lam_f[j])
            bwd = (lru_conv_w[j], lru_conv_b[j], wcat_b, 0.5 * lru_ba_b[j], 0.5 * lru_bx_b[j], lru_lam_b[j])
            tt_c, tt_l = _tile(n_ctx, 128), _tile(seq, 128)
            p_c3, p_l3 = p_c.reshape(bsz, n_ctx, 2 * d), p_l.reshape(bsz, seq, 2 * d)
            hcf, s_cf = _lru_scan(p_c3, *fwd, h0, tt_c, False)
            hcb, s_cb = _lru_scan(p_c3, *bwd, h0, tt_c, True)
            hlf, _ = _lru_scan(p_l3, *fwd, s_cf, tt_l, False)
            hlb, _ = _lru_scan(p_l3, *bwd, s_cb, tt_l, True)

            def lru_out(hf, hb, p2d, xs, midx, tm):
                blk = lambda i: (i, 0)
                spec = lambda: pl.BlockSpec((tm, d), blk)
                return _proj_res("lru", [hf.reshape(-1, d), hb.reshape(-1, d), p2d], [spec(), spec(), spec()],
                                 w_o, xs, mod, layer, midx, tm)

            xl = lru_out(hlf, hlb, p_l, xl, lat_idx, tm_l)
            if need_ctx:
                xc = lru_out(hcf, hcb, p_c, xc, ctx_idx, tm_c)

        else:
            qw = Q_HEADS * HEAD_DIM
            w_q = attn_w_in[j][:, :qw].astype(BF16)
            w_kv = attn_w_in[j][:, qw:].astype(BF16)
            w_o = attn_w_o[j].astype(BF16)
            qg, kg = attn_q_g[j], attn_k_g[j]
            q_l, kv_l = _attn_in(xl, mod, layer, lat_idx, gmix, w_q, w_kv, qg, kg, tm_l, _rope_tables(seq), seq,
                                 "attn_in_lat")
            q_c, kv_c = _attn_in(xc, mod, layer, ctx_idx, gmix, w_q, w_kv, qg, kg, tm_c, None, n_ctx, "attn_in_ctx")
            tk = _tile(seq // PV_SPLITS, 512)
            o_l = _attention(q_l, kv_c, kv_l, bsz, _tile(seq, 128), tk)
            half = lambda tm: [pl.BlockSpec((tm, d // KV_HEADS), lambda i: (i, 0))] * KV_HEADS
            xl = _proj_res("plain", o_l, half(tm_l), w_o, xl, mod, layer, lat_idx, tm_l)
            if need_ctx:
                o_c = _attention(q_c, kv_c, None, bsz, _tile(n_ctx, 128), tk)
                xc = _proj_res("plain", o_c, half(tm_c), w_o, xc, mod, layer, ctx_idx, tm_c)

        w1, w2 = w_mlp1[layer].astype(BF16), w_mlp2[layer].astype(BF16)
        tf = _tile(w1.shape[1], 1024)
        tmm_l = _tile(seq, 1024)
        xl = _mlp(xl, mod, layer, lambda i: 1 + (i * tmm_l) // seq, norm_mlp_g[layer], w1, w2, tmm_l, tf,
                  final_gain=None if need_ctx else final_g)
        if need_ctx:
            xc = _mlp(xc, mod, layer, ctx_idx, norm_mlp_g[layer], w1, w2, _tile(bsz * n_ctx, 1024), tf)

    return xl.reshape(bsz, seq, d)
```

```python
import functools

import numpy as np
import jax
import jax.numpy as jnp
from jax import lax
from jax.experimental import pallas as pl
from jax.experimental.pallas import tpu as pltpu

F32 = jnp.float32
BF16 = jnp.bfloat16

EPS = 1e-6
N_MOD = 6
N_MIXERS = 3
MOD_ROWS = 16

GLA_HEADS = 4
GLA_DK = 128
GLA_DV = 256
GLA_GATE_RANK = 16
GLA_GATE_TAU = 16.0
GLA_GATE_PAD = 128

RNN_BLOCKS = 8
RNN_BLOCK_DIM = 128
CONV_WIDTH = 4
CONV_LEFT = 2
LRU_C = 8.0

HEAD_DIM = 128
Q_HEADS = 8
KV_HEADS = 2
GROUP = Q_HEADS // KV_HEADS
GRID_W = 64
ROPE_THETA = 10000.0

VMEM_LIMIT_BYTES = 56 * 1024 * 1024


def _cparams(*sem):
    return pltpu.CompilerParams(dimension_semantics=sem, vmem_limit_bytes=VMEM_LIMIT_BYTES)


def _dot(a, b):
    return jnp.dot(a, b, preferred_element_type=F32)


def _dot_nt(a, b):
    return lax.dot_general(a, b, (((1,), (1,)), ((), ())), preferred_element_type=F32)


def _dot_tn(a, b):
    return lax.dot_general(a, b, (((0,), (0,)), ((), ())), preferred_element_type=F32)


def _rms(x, g):
    return x * lax.rsqrt(jnp.mean(x * x, axis=-1, keepdims=True) + EPS) * g


def _adaln(x, g, shift, scale):
    return _rms(x, g) * (1.0 + scale) + shift


def _mod_kernel(s_ref, w_ref, b_ref, o_ref):
    s = s_ref[...]
    s = s * jax.nn.sigmoid(s)
    o_ref[...] = _dot(s.astype(BF16), w_ref[...].astype(BF16)) + b_ref[...]


def _modulation(c, c_ctx, w_mod, b_mod):
    depth, d, nd = w_mod.shape
    bsz = c.shape[0]
    rows = jnp.concatenate([c_ctx[None, :], c, jnp.zeros((MOD_ROWS - 1 - bsz, d), F32)], axis=0)
    tn = min(nd, 1536)
    out = pl.pallas_call(
        _mod_kernel,
        grid=(depth, nd // tn),
        in_specs=[pl.BlockSpec((MOD_ROWS, d), lambda l, j: (0, 0)),
                  pl.BlockSpec((None, d, tn), lambda l, j: (l, 0, j)),
                  pl.BlockSpec((None, 1, tn), lambda l, j: (l, 0, j))],
        out_specs=pl.BlockSpec((None, MOD_ROWS, tn), lambda l, j: (l, 0, j)),
        out_shape=jax.ShapeDtypeStruct((depth, MOD_ROWS, nd), F32),
        compiler_params=_cparams("arbitrary", "arbitrary"),
        name="modulation",
    )(rows, w_mod, b_mod.reshape(depth, 1, nd))
    return out.reshape(depth, MOD_ROWS, N_MOD, d)


def _mod_spec(layer, d, mod_index):
    return pl.BlockSpec((None, None, N_MOD, d), lambda i, *_: (layer, mod_index(i), 0, 0))


def _ln_matmul_kernel(x_ref, mod_ref, g_ref, *refs, n_out, mod_off, col_chunk):
    w_refs, o_refs = refs[:n_out], refs[n_out:]
    h = _adaln(x_ref[...], g_ref[...], mod_ref[mod_off:mod_off + 1, :],
               mod_ref[mod_off + 1:mod_off + 2, :]).astype(BF16)
    for w_ref, o_ref in zip(w_refs, o_refs):
        n = w_ref.shape[1]
        for n0 in range(0, n, col_chunk):
            n1 = min(n, n0 + col_chunk)
            o_ref[:, n0:n1] = _dot(h, w_ref[:, n0:n1]).astype(o_ref.dtype)


def _ln_matmul(x2d, mod, layer, mod_index, gain, ws, out_dtypes, tm, name="ln_matmul"):
    rows, d = x2d.shape
    in_specs = [pl.BlockSpec((tm, d), lambda i: (i, 0)), _mod_spec(layer, d, mod_index),
                pl.BlockSpec((1, d), lambda i: (0, 0))]
    in_specs += [pl.BlockSpec(w.shape, lambda i: (0, 0)) for w in ws]
    return pl.pallas_call(
        functools.partial(_ln_matmul_kernel, n_out=len(ws), mod_off=0, col_chunk=512),
        grid=(rows // tm,),
        in_specs=in_specs,
        out_specs=[pl.BlockSpec((tm, w.shape[1]), lambda i: (i, 0)) for w in ws],
        out_shape=[jax.ShapeDtypeStruct((rows, w.shape[1]), dt) for w, dt in zip(ws, out_dtypes)],
        compiler_params=_cparams("arbitrary"),
        name=name,
    )(x2d, mod, gain.reshape(1, d), *ws)


def _proj_res_kernel(y0_ref, y1_ref, w_ref, x_ref, mod_ref, o_ref):
    y = jnp.concatenate([y0_ref[...], y1_ref[...]], axis=1)
    o_ref[...] = x_ref[...] + mod_ref[2:3, :] * _dot(y, w_ref[...])


def _gla_proj_res_kernel(of_ref, ob_ref, r_ref, ng_ref, w_ref, x_ref, mod_ref, o_ref):
    o = of_ref[...].astype(F32) + ob_ref[...].astype(F32)
    r = r_ref[...].astype(F32)
    ys = []
    for h in range(GLA_HEADS):
        sl = slice(h * GLA_DV, (h + 1) * GLA_DV)
        rh = r[:, sl]
        ys.append(_rms(o[:, sl], ng_ref[...]) * (rh * jax.nn.sigmoid(rh)))
    y = jnp.concatenate(ys, axis=1).astype(BF16)
    o_ref[...] = x_ref[...] + mod_ref[2:3, :] * _dot(y, w_ref[...])


def _lru_proj_res_kernel(hf_ref, hb_ref, gate_ref, w_ref, x_ref, mod_ref, o_ref):
    h = hf_ref[...].astype(F32) + hb_ref[...].astype(F32)
    y = (h * jax.nn.gelu(gate_ref[...].astype(F32))).astype(BF16)
    o_ref[...] = x_ref[...] + mod_ref[2:3, :] * _dot(y, w_ref[...])


def _proj_res(kind, operands, operand_specs, w, x2d, mod, layer, mod_index, tm, extra=(), extra_specs=()):
    rows, d = x2d.shape
    body = {"plain": _proj_res_kernel, "gla": _gla_proj_res_kernel, "lru": _lru_proj_res_kernel}[kind]
    n_in = len(operands) + len(extra) + 1
    return pl.pallas_call(
        body,
        grid=(rows // tm,),
        in_specs=[*operand_specs, *extra_specs, pl.BlockSpec(w.shape, lambda i: (0, 0)),
                  pl.BlockSpec((tm, d), lambda i: (i, 0)), _mod_spec(layer, d, mod_index)],
        out_specs=pl.BlockSpec((tm, d), lambda i: (i, 0)),
        out_shape=jax.ShapeDtypeStruct((rows, d), F32),
        input_output_aliases={n_in: 0} if layer > 0 else {},
        compiler_params=_cparams("arbitrary"),
        name=kind + "_proj_res",
    )(*operands, *extra, w, x2d, mod)


def _mlp_kernel(x_ref, mod_ref, g_ref, w1_ref, w2_ref, *rest, final_norm):
    if final_norm:
        fg_ref, o_ref, h_sc, acc_sc = rest
    else:
        o_ref, h_sc, acc_sc = rest
    j = pl.program_id(1)

    @pl.when(j == 0)
    def _():
        h_sc[...] = _adaln(x_ref[...], g_ref[...], mod_ref[3:4, :], mod_ref[4:5, :]).astype(BF16)
        acc_sc[...] = jnp.zeros_like(acc_sc)

    a = jnp.maximum(_dot(h_sc[...], w1_ref[...]), 0.0)
    acc_sc[...] += _dot((a * a).astype(BF16), w2_ref[...])

    @pl.when(j == pl.num_programs(1) - 1)
    def _():
        y = x_ref[...] + mod_ref[5:6, :] * acc_sc[...]
        o_ref[...] = _rms(y, fg_ref[...]) if final_norm else y


def _mlp(x2d, mod, layer, mod_index, gain, w1, w2, tm, tf, final_gain=None):
    rows, d = x2d.shape
    dff = w1.shape[1]
    vec = lambda: pl.BlockSpec((1, d), lambda i, j: (0, 0))
    in_specs = [pl.BlockSpec((tm, d), lambda i, j: (i, 0)), _mod_spec(layer, d, mod_index), vec(),
                pl.BlockSpec((d, tf), lambda i, j: (0, j)),
                pl.BlockSpec((tf, d), lambda i, j: (j, 0))]
    args = [x2d, mod, gain.reshape(1, d), w1, w2]
    if final_gain is not None:
        in_specs.append(vec())
        args.append(final_gain.reshape(1, d))
    return pl.pallas_call(
        functools.partial(_mlp_kernel, final_norm=final_gain is not None),
        grid=(rows // tm, dff // tf),
        in_specs=in_specs,
        out_specs=pl.BlockSpec((tm, d), lambda i, j: (i, 0)),
        out_shape=jax.ShapeDtypeStruct((rows, d), F32),
        scratch_shapes=[pltpu.VMEM((tm, d), BF16), pltpu.VMEM((tm, d), F32)],
        input_output_aliases={0: 0},
        compiler_params=_cparams("arbitrary", "arbitrary"),
        name="mlp",
    )(*args)


GLA_CHUNK = 256
GLA_MATMUL_LEVEL_ROWS = 8
GLA_FAST_MAX_DECAY = 60.0


def _gla_tables(chunk, reverse):
    levels = int(np.log2(chunk))
    idx = np.arange(chunk)
    t, r = idx[:, None], idx[None, :]
    if not reverse:
        q_incl = (r <= t)
        k_rest = (r > t)
    else:
        q_incl = (r >= t)
        k_rest = (r < t)
    mats = [q_incl, k_rest]
    masks = [np.eye(chunk, dtype=bool)]
    for lv in range(1, levels + 1):
        m = 1 << (lv - 1)
        mid = (idx // (2 * m)) * (2 * m) + m
        upper = idx >= mid
        midc = mid[:, None]
        if not reverse:
            pat = np.where(upper[:, None], (r >= midc) & (r <= t), (r > t) & (r < midc))
            mask = (upper[:, None] & ~upper[None, :])
        else:
            pat = np.where(upper[:, None], (r >= midc) & (r < t), (r >= t) & (r < midc))
            mask = (~upper[:, None] & upper[None, :])
        mask = mask & ((idx[:, None] // (2 * m)) == (idx[None, :] // (2 * m)))
        if 2 * m <= GLA_MATMUL_LEVEL_ROWS:
            mats.append(pat)
        masks.append(mask)
    a = np.concatenate(mats, axis=0).astype(np.float32)
    a2 = np.concatenate([a, a], axis=1)
    return jnp.asarray(a2, BF16), jnp.asarray(np.stack(masks).astype(np.float32), F32)


def _gla_gates(qkvr_ref, gf_ref, wup_ref, bup_ref, a_ref, reverse):
    chunk = qkvr_ref.shape[0]
    dq = GLA_HEADS * GLA_DK
    graw = _dot(gf_ref[...], wup_ref[...]) + bup_ref[...]
    g = (jnp.minimum(graw, 0.0) - jnp.log(1.0 + jnp.exp(-jnp.abs(graw)))) * (1.0 / GLA_GATE_TAU)
    g_hi = g.astype(BF16)
    g_cat = jnp.concatenate([g_hi, (g - g_hi.astype(F32)).astype(BF16)], axis=0)

    def partial_sums(block):
        return _dot(a_ref[block * chunk:(block + 1) * chunk, :], g_cat)

    st = dict(partial_sums=partial_sums, bq=partial_sums(0), reverse=reverse, chunk=chunk,
              q_all=qkvr_ref[:, 0:dq].astype(F32) * (GLA_DK ** -0.5), k_all=qkvr_ref[:, dq:2 * dq].astype(F32))
    st["qd"] = (st["q_all"] * jnp.exp(st["bq"])).astype(BF16)
    st["btot"] = st["bq"][0:1, :] if reverse else st["bq"][chunk - 1:chunk, :]
    return st


def _gla_levels(st, z_sc, lv_range):
    chunk, reverse, bq, bk = st["chunk"], st["reverse"], st["bq"], st["bk"]
    q_all, k_all = st["q_all"], st["k_all"]
    row = lax.broadcasted_iota(jnp.int32, (chunk, 1), 0)
    for lv in lv_range:
        m = 1 << (lv - 1)
        if 2 * m <= GLA_MATMUL_LEVEL_ROWS:
            use_q = ((row // m) % 2) == (0 if reverse else 1)
            z = jnp.where(use_q, q_all, k_all) * jnp.exp(st["partial_sums"](1 + lv))
        else:
            pieces = []
            for lo in range(0, chunk, 2 * m):
                mid, hi = lo + m, lo + 2 * m
                if not reverse:
                    pieces.append(k_all[lo:mid] * jnp.exp(bk[lo:mid] - bk[mid - 1:mid]))
                    pieces.append(q_all[mid:hi] * jnp.exp(bq[mid:hi] - bq[mid - 1:mid]))
                else:
                    pieces.append(q_all[lo:mid] * jnp.exp(bq[lo:mid] - bq[mid:mid + 1]))
                    pieces.append(k_all[mid:hi] * jnp.exp(bk[mid:hi] - bk[mid:mid + 1]))
            z = jnp.concatenate(pieces, axis=0)
        z_sc[lv - 1] = z.astype(BF16)


def _gla_scores_by_level(st, h, mask_ref, z_sc, levels):
    sl = slice(h * GLA_DK, (h + 1) * GLA_DK)
    chunk, reverse = st["chunk"], st["reverse"]
    sub = 8
    diag = mask_ref[0] * jnp.sum(st["q_all"][:, sl] * st["k_all"][:, sl], axis=-1, keepdims=True)
    tiles = [diag[r:r + sub] for r in range(0, chunk, sub)]
    for lv in range(1, levels + 1):
        m = 1 << (lv - 1)
        if 2 * m <= GLA_MATMUL_LEVEL_ROWS:
            q_rows = [(0, chunk)]
        else:
            q_rows = [(lo, lo + m) if reverse else (lo + m, lo + 2 * m) for lo in range(0, chunk, 2 * m)]
        zq = jnp.concatenate([z_sc[lv - 1, lo:hi, sl] for lo, hi in q_rows], axis=0)
        mq = jnp.concatenate([mask_ref[lv, lo:hi, :] for lo, hi in q_rows], axis=0)
        part = mq * _dot_nt(zq, z_sc[lv - 1, :, sl])
        src = 0
        for lo, hi in q_rows:
            for r in range(lo, hi, sub):
                tiles[r // sub] = tiles[r // sub] + part[src:src + sub]
                src += sub
    return jnp.concatenate(tiles, axis=0)


def _gla_head_update(st, h, scores, qkvr_ref, s_sc, o_ref):
    dq = GLA_HEADS * GLA_DK
    sl = slice(h * GLA_DK, (h + 1) * GLA_DK)
    vh = qkvr_ref[:, 2 * dq + h * GLA_DV:2 * dq + (h + 1) * GLA_DV]
    s_old = s_sc[h]
    o_ref[:, h * GLA_DV:(h + 1) * GLA_DV] = (_dot(scores.astype(BF16), vh)
                                            + _dot(st["qd"][:, sl], s_old.astype(BF16))).astype(o_ref.dtype)
    etot = jnp.exp(st["btot"][:, sl])
    ecol = jnp.transpose(jnp.broadcast_to(etot, (GLA_DK, GLA_DK)))
    s_sc[h] = s_old * jnp.concatenate([ecol] * (GLA_DV // GLA_DK), axis=1) + _dot_tn(st["kd"][:, sl], vh)


def _gla_kernel(qkvr_f, gf_f, qkvr_b, gf_b, wup_f, wup_b, bup_f, bup_b, a_f, a_b, mask_f, mask_b, pair_f, pair_b,
                s0_f, s0_b, o_f, o_b, sfin_f, sfin_b, s_sc_f, s_sc_b, z_sc_f, z_sc_b):
    c = pl.program_id(1)
    levels = mask_f.shape[0] - 1
    dirs = ((qkvr_f, mask_f, pair_f, z_sc_f, s_sc_f, o_f), (qkvr_b, mask_b, pair_b, z_sc_b, s_sc_b, o_b))

    @pl.when(c == 0)
    def _():
        s_sc_f[...] = s0_f[...]
        s_sc_b[...] = s0_b[...]

    sts = (_gla_gates(qkvr_f, gf_f, wup_f, bup_f, a_f, False), _gla_gates(qkvr_b, gf_b, wup_b, bup_b, a_b, True))
    worst = jnp.maximum(jnp.max(jnp.abs(sts[0]["btot"])), jnp.max(jnp.abs(sts[1]["btot"])))
    small_decay = worst <= GLA_FAST_MAX_DECAY

    @pl.when(small_decay)
    def _():
        kinv = []
        for st in sts:
            kexp = st["k_all"] * jnp.exp(-st["bq"])
            kinv.append(kexp.astype(BF16))
            st["kd"] = (kexp * jnp.exp(st["btot"])).astype(BF16)
        for h in range(GLA_HEADS):
            sl = slice(h * GLA_DK, (h + 1) * GLA_DK)
            for st, kv, (qkvr, _, pair, _, s_sc, o_ref) in zip(sts, kinv, dirs):
                scores = pair[...] * _dot_nt(st["qd"][:, sl], kv[:, sl])
                _gla_head_update(st, h, scores, qkvr, s_sc, o_ref)

    @pl.when(jnp.logical_not(small_decay))
    def _():
        for st, (_, _, _, z_sc, _, _) in zip(sts, dirs):
            st["bk"] = st["partial_sums"](1)
            st["kd"] = (st["k_all"] * jnp.exp(st["bk"])).astype(BF16)
            _gla_levels(st, z_sc, range(1, levels + 1))
        for h in range(GLA_HEADS):
            for st, (qkvr, mask, _, z_sc, s_sc, o_ref) in zip(sts, dirs):
                _gla_head_update(st, h, _gla_scores_by_level(st, h, mask, z_sc, levels), qkvr, s_sc, o_ref)

    @pl.when(c == pl.num_programs(1) - 1)
    def _():
        sfin_f[...] = s_sc_f[...]
        sfin_b[...] = s_sc_b[...]


def _gla_scan(qkvr, gfeat, wup_f, wup_b, bup_f, bup_b, s0_f, s0_b, bsz, chunk):
    rows = qkvr.shape[0]
    nc = rows // bsz // chunk
    a_f, mask_f = _gla_tables(chunk, False)
    a_b, mask_b = _gla_tables(chunk, True)
    tri = np.tril(np.ones((chunk, chunk), np.float32))
    pair_f, pair_b = jnp.asarray(tri), jnp.asarray(tri.T)
    dv = GLA_HEADS * GLA_DV
    fwd_block = lambda b, c: (b * nc + c, 0)
    bwd_block = lambda b, c: (b * nc + nc - 1 - c, 0)
    const2 = lambda x: pl.BlockSpec(x.shape, lambda b, c: (0, 0))
    const3 = lambda x: pl.BlockSpec(x.shape, lambda b, c: (0, 0, 0))
    state_spec = pl.BlockSpec((None, GLA_HEADS, GLA_DK, GLA_DV), lambda b, c: (b, 0, 0, 0))
    state_shape = jax.ShapeDtypeStruct((bsz, GLA_HEADS, GLA_DK, GLA_DV), F32)
    state_sc = lambda: pltpu.VMEM((GLA_HEADS, GLA_DK, GLA_DV), F32)
    z_sc = lambda: pltpu.VMEM((mask_f.shape[0] - 1, chunk, GLA_HEADS * GLA_DK), BF16)
    return pl.pallas_call(
        _gla_kernel,
        grid=(bsz, nc),
        in_specs=[pl.BlockSpec((chunk, qkvr.shape[1]), fwd_block), pl.BlockSpec((chunk, gfeat.shape[1]), fwd_block),
                  pl.BlockSpec((chunk, qkvr.shape[1]), bwd_block), pl.BlockSpec((chunk, gfeat.shape[1]), bwd_block),
                  const2(wup_f), const2(wup_b), const2(bup_f), const2(bup_b), const2(a_f), const2(a_b),
                  const3(mask_f), const3(mask_b), const2(pair_f), const2(pair_b), state_spec, state_spec],
        out_specs=[pl.BlockSpec((chunk, dv), fwd_block), pl.BlockSpec((chunk, dv), bwd_block), state_spec, state_spec],
        out_shape=[jax.ShapeDtypeStruct((rows, dv), BF16), jax.ShapeDtypeStruct((rows, dv), BF16),
                   state_shape, state_shape],
        scratch_shapes=[state_sc(), state_sc(), z_sc(), z_sc()],
        compiler_params=_cparams("arbitrary", "arbitrary"),
        name="gla_scan",
    )(qkvr, gfeat, qkvr, gfeat, wup_f, wup_b, bup_f, bup_b, a_f, a_b, mask_f, mask_b, pair_f, pair_b, s0_f, s0_b)


LRU_HALO = 16
CONV_SHIFTS = tuple(j - CONV_LEFT for j in range(CONV_WIDTH) if j != CONV_LEFT)
F32_TINY = float(np.finfo(np.float32).tiny)


def _lru_shift_matrix(tt):
    sel = np.zeros((len(CONV_SHIFTS) * tt, tt + 2 * LRU_HALO), np.float32)
    for i, off in enumerate(CONV_SHIFTS):
        sel[i * tt + np.arange(tt), LRU_HALO + np.arange(tt) + off] = 1.0
    return jnp.asarray(sel, BF16)


def _lru_kernel(x_ref, prev_ref, next_ref, shift_ref, cw_ref, cb_ref, wcat_ref, ba_ref, bx_ref, lam_ref, h0_ref,
                hs_ref, hfin_ref, xs_sc, xc_sc, a_sc, u_sc, hs_sc, h_sc, *, reverse):
    i = pl.program_id(0)
    n = pl.num_programs(0)
    tidx = n - 1 - i if reverse else i
    bsz, tt, width = x_ref.shape
    bd = RNN_BLOCK_DIM

    @pl.when(i == 0)
    def _():
        h_sc[...] = h0_ref[...]

    xs_sc[:, 0:LRU_HALO, :] = jnp.where(tidx > 0, prev_ref[...], jnp.zeros_like(prev_ref))
    xs_sc[:, LRU_HALO:LRU_HALO + tt, :] = x_ref[...]
    xs_sc[:, LRU_HALO + tt:2 * LRU_HALO + tt, :] = jnp.where(tidx < n - 1, next_ref[...], jnp.zeros_like(next_ref))
    for b in range(bsz):
        shifted = _dot(shift_ref[...], xs_sc[b])
        xc = cb_ref[...] + cw_ref[CONV_LEFT:CONV_LEFT + 1, :] * x_ref[b].astype(F32)
        for k, off in enumerate(CONV_SHIFTS):
            j = off + CONV_LEFT
            xc = xc + cw_ref[j:j + 1, :] * shifted[k * tt:(k + 1) * tt]
        xc_sc[b] = xc

    lam = lam_ref[...]
    neg4sp = -0.5 * LRU_C * (jnp.maximum(-lam, 0.0) + jnp.log1p(jnp.exp(-jnp.abs(lam))))
    for nb in range(RNN_BLOCKS):
        sl = slice(nb * bd, (nb + 1) * bd)
        xc2 = xc_sc[:, :, sl].reshape(bsz * tt, bd)
        ri = _dot(xc2.astype(BF16), wcat_ref[nb])
        log_a = neg4sp[:, sl] + neg4sp[:, sl] * jnp.tanh(ri[:, :bd] + ba_ref[:, sl])
        ig = 0.5 + 0.5 * jnp.tanh(ri[:, bd:] + bx_ref[:, sl])
        th = jnp.tanh(log_a)
        one_minus_a2 = (-2.0 * th) / (1.0 - th)
        root = one_minus_a2 * lax.rsqrt(jnp.maximum(one_minus_a2, F32_TINY))
        a = jnp.exp(log_a)
        u = root * (ig * xc2)
        for b in range(bsz):
            a_sc[nb, pl.ds(b, tt, stride=bsz), :] = a[b * tt:(b + 1) * tt]
            u_sc[nb, pl.ds(b, tt, stride=bsz), :] = u[b * tt:(b + 1) * tt]

    def step(k, hs):
        t = tt - 1 - k if reverse else k
        rows = pl.ds(pl.multiple_of(t * bsz, bsz), bsz)
        new = []
        for nb in range(RNN_BLOCKS):
            h = a_sc[nb, rows, :] * hs[nb] + u_sc[nb, rows, :]
            hs_sc[nb, rows, :] = h
            new.append(h)
        return tuple(new)

    h_init = tuple(h_sc[:, nb * bd:(nb + 1) * bd] for nb in range(RNN_BLOCKS))
    h_last = lax.fori_loop(0, tt, step, h_init, unroll=4)
    for nb in range(RNN_BLOCKS):
        sl = slice(nb * bd, (nb + 1) * bd)
        h_sc[:, sl] = h_last[nb]
        for b in range(bsz):
            hs_ref[b, :, sl] = hs_sc[nb, pl.ds(b, tt, stride=bsz), :].astype(hs_ref.dtype)

    @pl.when(i == n - 1)
    def _():
        hfin_ref[...] = h_sc[...]


def _lru_scan(p3, conv_w, conv_b, wcat, ba, bx, lam, h0, tt, reverse):
    bsz, seq, two_w = p3.shape
    width = two_w // 2
    n = seq // tt
    hb = tt // LRU_HALO
    assert tt % LRU_HALO == 0 and width == RNN_BLOCKS * RNN_BLOCK_DIM
    shift = _lru_shift_matrix(tt)

    def tix(i):
        return n - 1 - i if reverse else i

    vec = lambda: pl.BlockSpec((1, width), lambda i: (0, 0))
    slab = lambda: pltpu.VMEM((RNN_BLOCKS, tt * bsz, RNN_BLOCK_DIM), F32)
    return pl.pallas_call(
        functools.partial(_lru_kernel, reverse=reverse),
        grid=(n,),
        in_specs=[pl.BlockSpec((bsz, tt, width), lambda i: (0, tix(i), 1)),
                  pl.BlockSpec((bsz, LRU_HALO, width), lambda i: (0, jnp.maximum(tix(i) * hb - 1, 0), 1)),
                  pl.BlockSpec((bsz, LRU_HALO, width), lambda i: (0, jnp.minimum((tix(i) + 1) * hb, seq // LRU_HALO - 1), 1)),
                  pl.BlockSpec(shift.shape, lambda i: (0, 0)),
                  pl.BlockSpec((CONV_WIDTH, width), lambda i: (0, 0)), vec(),
                  pl.BlockSpec(wcat.shape, lambda i: (0, 0, 0)), vec(), vec(), vec(),
                  pl.BlockSpec((bsz, width), lambda i: (0, 0))],
        out_specs=[pl.BlockSpec((bsz, tt, width), lambda i: (0, tix(i), 0)),
                   pl.BlockSpec((bsz, width), lambda i: (0, 0))],
        out_shape=[jax.ShapeDtypeStruct((bsz, seq, width), BF16),
                   jax.ShapeDtypeStruct((bsz, width), F32)],
        scratch_shapes=[pltpu.VMEM((bsz, tt + 2 * LRU_HALO, width), BF16), pltpu.VMEM((bsz, tt, width), F32),
                        slab(), slab(), slab(), pltpu.VMEM((bsz, width), F32)],
        compiler_params=_cparams("arbitrary"),
        name="lru_scan_bwd" if reverse else "lru_scan_fwd",
    )(p3, p3, p3, shift, conv_w, conv_b.reshape(1, width), wcat, ba.reshape(1, width),
      bx.reshape(1, width), lam.reshape(1, width), h0)


def _rope(x, cos, sin):
    return x * cos + pltpu.roll(x, HEAD_DIM // 2, axis=1) * sin


KV_OUT_WIDTH = 3 * KV_HEADS * HEAD_DIM
PV_SPLITS = 4


def _attn_in_kernel(x_ref, mod_ref, g_ref, wq_ref, wkv_ref, qg_ref, kg_ref, seg_ref, *rest, rope):
    if rope:
        cos_ref, sin_ref, q_ref, kv_ref = rest
    else:
        q_ref, kv_ref = rest
    h = _adaln(x_ref[...], g_ref[...], mod_ref[0:1, :], mod_ref[1:2, :]).astype(BF16)
    pair_w = 2 * HEAD_DIM

    def head_pair(x2, gain, scale):
        sq = x2 * x2
        sq_hi = sq.astype(BF16)
        ss = _dot(sq_hi, seg_ref[...]) + _dot((sq - sq_hi.astype(F32)).astype(BF16), seg_ref[...])
        g2 = jnp.concatenate([gain, gain], axis=1)
        xn = x2 * lax.rsqrt(ss * (1.0 / HEAD_DIM) + EPS) * g2
        if rope:
            xn = jnp.concatenate([_rope(xn[:, i * HEAD_DIM:(i + 1) * HEAD_DIM], cos_ref[...], sin_ref[...])
                                  for i in range(2)], axis=1)
        return (xn * scale).astype(BF16) if scale != 1.0 else xn.astype(BF16)

    group_w = GROUP * HEAD_DIM
    q_groups = [_dot(h, wq_ref[:, n0:n0 + group_w]) for n0 in range(0, Q_HEADS * HEAD_DIM, group_w)]
    kv = _dot(h, wkv_ref[...])
    for gi, res in enumerate(q_groups):
        for n0 in range(0, group_w, pair_w):
            q_ref[:, gi * group_w + n0:gi * group_w + n0 + pair_w] = head_pair(res[:, n0:n0 + pair_w], qg_ref[...],
                                                                               HEAD_DIM ** -0.5)
    kw = KV_HEADS * HEAD_DIM
    assert kw == pair_w
    kv_ref[:, 0:kw] = head_pair(kv[:, 0:kw], kg_ref[...], 1.0)
    ones = jnp.ones((kv.shape[0], HEAD_DIM), BF16)
    for j in range(KV_HEADS):
        v0 = kw + 2 * j * HEAD_DIM
        kv_ref[:, v0:v0 + HEAD_DIM] = kv[:, kw + j * HEAD_DIM:kw + (j + 1) * HEAD_DIM].astype(BF16)
        kv_ref[:, v0 + HEAD_DIM:v0 + 2 * HEAD_DIM] = ones


def _attn_in(x2d, mod, layer, mod_index, gain, w_q, w_kv, q_g, k_g, tm, rope_tables, seq, name):
    rows, d = x2d.shape
    vec = lambda: pl.BlockSpec((1, HEAD_DIM), lambda i: (0, 0))
    in_specs = [pl.BlockSpec((tm, d), lambda i: (i, 0)), _mod_spec(layer, d, mod_index),
                pl.BlockSpec((1, d), lambda i: (0, 0)),
                pl.BlockSpec(w_q.shape, lambda i: (0, 0)), pl.BlockSpec(w_kv.shape, lambda i: (0, 0)), vec(), vec(),
                pl.BlockSpec((2 * HEAD_DIM, 2 * HEAD_DIM), lambda i: (0, 0))]
    seg = jnp.asarray(np.kron(np.eye(2, dtype=np.float32), np.ones((HEAD_DIM, HEAD_DIM), np.float32)), BF16)
    args = [x2d, mod, gain.reshape(1, d), w_q, w_kv, q_g.reshape(1, HEAD_DIM), k_g.reshape(1, HEAD_DIM), seg]
    if rope_tables is not None:
        tpb = seq // tm
        in_specs += [pl.BlockSpec((tm, HEAD_DIM), lambda i: (i % tpb, 0))] * 2
        args += list(rope_tables)
    return pl.pallas_call(
        functools.partial(_attn_in_kernel, rope=rope_tables is not None),
        grid=(rows // tm,),
        in_specs=in_specs,
        out_specs=[pl.BlockSpec((tm, w_q.shape[1]), lambda i: (i, 0)),
                   pl.BlockSpec((tm, KV_OUT_WIDTH), lambda i: (i, 0))],
        out_shape=[jax.ShapeDtypeStruct((rows, w_q.shape[1]), BF16),
                   jax.ShapeDtypeStruct((rows, KV_OUT_WIDTH), BF16)],
        compiler_params=_cparams("arbitrary"),
        name=name,
    )(*args)


def _attn_kernel(q_ref, kvc_ref, *rest, has_latent, tk):
    if has_latent:
        kvl_ref, o0_ref, o1_ref, s0_sc, s1_sc, p0_sc, p1_sc, m1_sc = rest
    else:
        o0_ref, o1_ref, s0_sc, s1_sc, p0_sc, p1_sc, m1_sc = rest
    tq = q_ref.shape[0]
    rows = GROUP * tq
    kw = KV_HEADS * HEAD_DIM
    lanes = HEAD_DIM
    n_ctx = kvc_ref.shape[0]
    n_lat = kvl_ref.shape[0] if has_latent else 0

    @pl.when((pl.program_id(0) == 0) & (pl.program_id(1) == 0))
    def _():
        s1_sc[...] = jnp.zeros_like(s1_sc)
        m1_sc[...] = jnp.zeros_like(m1_sc)

    def lane_groups(s):
        return [s[:, c * lanes:(c + 1) * lanes] for c in range(s.shape[1] // lanes)]

    tkc = min(tk, n_ctx)
    tiles = [(kvc_ref, r0, tkc, r0) for r0 in range(0, n_ctx, tkc)]
    tiles += [(kvl_ref, r0, tk, n_ctx + r0) for r0 in range(0, n_lat, tk)]
    pv_parts = [(kvc_ref, 0, n_ctx, 0)]
    pv_parts += [(kvl_ref, r0, n_lat // PV_SPLITS, n_ctx + r0) for r0 in range(0, n_lat, max(n_lat // PV_SPLITS, 1))]

    def stage(hk_new, s_new_sc, hk_old, s_old_sc, m_old, p_sc, o_ref):
        ksl = slice(hk_new * HEAD_DIM, (hk_new + 1) * HEAD_DIM)
        vsl = slice(kw + 2 * hk_old * HEAD_DIM, kw + 2 * (hk_old + 1) * HEAD_DIM)
        qcat = jnp.concatenate([q_ref[:, (hk_new * GROUP + gi) * HEAD_DIM:(hk_new * GROUP + gi + 1) * HEAD_DIM]
                                for gi in range(GROUP)], axis=0)
        mrun = jnp.full((rows, lanes), -jnp.inf, F32)
        partial, pending = [], list(pv_parts)
        for kv_ref, r0, w, c0 in tiles:
            s = _dot_nt(qcat, kv_ref[r0:r0 + w, ksl])
            s_new_sc[:, c0:c0 + w] = s
            mrun = functools.reduce(jnp.maximum, [mrun] + lane_groups(s))
            p = [jnp.exp((sg - m_old).astype(BF16)) for sg in lane_groups(s_old_sc[:, c0:c0 + w])]
            p_sc[:, c0:c0 + w] = jnp.concatenate(p, axis=1)
            while pending and pending[0][3] + pending[0][2] <= c0 + w:
                pv_ref, pr0, pw, pc0 = pending.pop(0)
                partial.append(_dot(p_sc[:, pc0:pc0 + pw], pv_ref[pr0:pr0 + pw, vsl]))
        acc = functools.reduce(jnp.add, partial)
        o = acc[:, 0:HEAD_DIM] / acc[:, HEAD_DIM:HEAD_DIM + 1]
        for gi in range(GROUP):
            o_ref[:, gi * HEAD_DIM:(gi + 1) * HEAD_DIM] = o[gi * tq:(gi + 1) * tq].astype(o_ref.dtype)
        return jnp.broadcast_to(jnp.max(mrun, axis=-1, keepdims=True), (rows, lanes))

    m0 = stage(0, s0_sc, 1, s1_sc, m1_sc[...], p1_sc, o1_ref)
    m1_sc[...] = stage(1, s1_sc, 0, s0_sc, m0, p0_sc, o0_ref)


def _attention(q, kv_ctx, kv_lat, bsz, tq, tk):
    rows = q.shape[0]
    nq = rows // bsz // tq
    n_ctx = kv_ctx.shape[0] // bsz
    has_latent = kv_lat is not None
    n_keys = n_ctx + (kv_lat.shape[0] // bsz if has_latent else 0)
    assert n_ctx % min(tk, n_ctx) == 0 and (n_keys - n_ctx) % (tk * PV_SPLITS) == 0
    in_specs = [pl.BlockSpec((tq, q.shape[1]), lambda b, j: (b * nq + jnp.minimum(j, nq - 1), 0)),
                pl.BlockSpec((n_ctx, kv_ctx.shape[1]), lambda b, j: (b, 0))]
    args = [q, kv_ctx]
    if has_latent:
        in_specs.append(pl.BlockSpec((n_keys - n_ctx, kv_lat.shape[1]), lambda b, j: (b, 0)))
        args.append(kv_lat)
    group_w = GROUP * HEAD_DIM
    scores = lambda: pltpu.VMEM((GROUP * tq, n_keys), F32)
    probs = lambda: pltpu.VMEM((GROUP * tq, n_keys), BF16)
    return pl.pallas_call(
        functools.partial(_attn_kernel, has_latent=has_latent, tk=tk),
        grid=(bsz, nq + 1),
        in_specs=in_specs,
        out_specs=[pl.BlockSpec((tq, group_w), lambda b, j: (b * nq + jnp.minimum(j, nq - 1), 0)),
                   pl.BlockSpec((tq, group_w), lambda b, j: (b * nq + jnp.maximum(j - 1, 0), 0))],
        out_shape=[jax.ShapeDtypeStruct((rows, group_w), BF16), jax.ShapeDtypeStruct((rows, group_w), BF16)],
        scratch_shapes=[scores(), scores(), probs(), probs(), pltpu.VMEM((GROUP * tq, HEAD_DIM), F32)],
        compiler_params=_cparams("arbitrary", "arbitrary"),
        name="attention" if has_latent else "attention_ctx",
    )(*args)


def _rope_tables(seq):
    n_rows = seq // GRID_W
    row = jnp.repeat(jnp.arange(n_rows), GRID_W)
    col = jnp.tile(jnp.arange(GRID_W), n_rows)
    n_pairs_axis = HEAD_DIM // 4
    inv_freq = ROPE_THETA ** (-jnp.arange(n_pairs_axis, dtype=F32) / n_pairs_axis)
    ang = jnp.concatenate([row[:, None] * inv_freq, col[:, None] * inv_freq], axis=-1)
    cos, sin = jnp.cos(ang), jnp.sin(ang)
    return jnp.concatenate([cos, cos], axis=-1), jnp.concatenate([-sin, sin], axis=-1)


def _tile(n, pref):
    t = min(n, pref)
    assert n % t == 0, (n, pref)
    return t


def kernel(x, c, ctx, c_ctx, norm_mix_g, norm_mlp_g, w_mod, b_mod, w_mlp1, w_mlp2, gla_w_in, gla_w_up_f, gla_b_f, gla_w_up_b, gla_b_b, gla_norm_g, gla_w_o, lru_w_in, lru_conv_w, lru_conv_b, lru_wa_f, lru_ba_f, lru_wx_f, lru_bx_f, lru_lam_f, lru_wa_b, lru_ba_b, lru_wx_b, lru_bx_b, lru_lam_b, lru_w_o, attn_w_in, attn_q_g, attn_k_g, attn_w_o, final_g):
    bsz, seq, d = x.shape
    n_ctx = ctx.shape[1]
    depth = w_mod.shape[0]
    assert bsz + 1 <= MOD_ROWS and d == GLA_HEADS * GLA_DV == Q_HEADS * HEAD_DIM

    mod = _modulation(c, c_ctx, w_mod, b_mod)
    xl = x.reshape(bsz * seq, d)
    xc = ctx.reshape(bsz * n_ctx, d)

    tm_l = _tile(seq, 1024)
    tm_c = _tile(n_ctx, 256)
    lat_idx = lambda i: 1 + (i * tm_l) // seq
    ctx_idx = lambda i: 0

    for layer in range(depth):
        need_ctx = layer < depth - 1
        kind, j = layer % N_MIXERS, layer // N_MIXERS
        gmix = norm_mix_g[layer]

        if kind == 0:
            dq = GLA_HEADS * GLA_DK
            main_w = gla_w_in[j][:, :2 * dq + 2 * d].astype(BF16)
            gate_w = jnp.pad(gla_w_in[j][:, 2 * dq + 2 * d:], ((0, 0), (0, GLA_GATE_PAD - 2 * GLA_GATE_RANK))).astype(BF16)
            zpad = jnp.zeros((GLA_GATE_PAD - 2 * GLA_GATE_RANK, dq), F32)
            wup_f = jnp.concatenate([gla_w_up_f[j], jnp.zeros_like(gla_w_up_b[j]), zpad], axis=0).astype(BF16)
            wup_b = jnp.concatenate([jnp.zeros_like(gla_w_up_f[j]), gla_w_up_b[j], zpad], axis=0).astype(BF16)
            bup_f, bup_b = gla_b_f[j].reshape(1, dq), gla_b_b[j].reshape(1, dq)
            w_o = gla_w_o[j].astype(BF16)
            proj = {}
            for name, (xs, midx, tm, slen) in zip(("lat", "ctx"), [(xl, lat_idx, tm_l, seq), (xc, ctx_idx, tm_c, n_ctx)]):
                proj[name] = _ln_matmul(xs, mod, layer, midx, gmix, [main_w, gate_w], [BF16, BF16], tm,
                                        name="gla_in_" + name)
            s0 = jnp.zeros((bsz, GLA_HEADS, GLA_DK, GLA_DV), F32)
            ch_c, ch_l = _tile(n_ctx, GLA_CHUNK), _tile(seq, GLA_CHUNK)
            ocf, ocb, s_cf, s_cb = _gla_scan(*proj["ctx"], wup_f, wup_b, bup_f, bup_b, s0, s0, bsz, ch_c)
            olf, olb, _, _ = _gla_scan(*proj["lat"], wup_f, wup_b, bup_f, bup_b, s_cf, s_cb, bsz, ch_l)
            ng = gla_norm_g[j].reshape(1, GLA_DV)

            def gla_out(of, ob, qkvr, xs, midx, tm):
                blk = lambda i: (i, 0)
                return _proj_res("gla", [of, ob, qkvr], [pl.BlockSpec((tm, d), blk), pl.BlockSpec((tm, d), blk),
                                                        pl.BlockSpec((tm, d), lambda i: (i, (2 * dq + d) // d))],
                                 w_o, xs, mod, layer, midx, tm, extra=[ng],
                                 extra_specs=[pl.BlockSpec((1, GLA_DV), lambda i: (0, 0))])

            xl = gla_out(olf, olb, proj["lat"][0], xl, lat_idx, tm_l)
            if need_ctx:
                xc = gla_out(ocf, ocb, proj["ctx"][0], xc, ctx_idx, tm_c)

        elif kind == 1:
            w_in = lru_w_in[j].astype(BF16)
            w_o = lru_w_o[j].astype(BF16)
            wcat_f = (0.5 * jnp.concatenate([lru_wa_f[j], lru_wx_f[j]], axis=-1)).astype(BF16)
            wcat_b = (0.5 * jnp.concatenate([lru_wa_b[j], lru_wx_b[j]], axis=-1)).astype(BF16)
            p_l, = _ln_matmul(xl, mod, layer, lat_idx, gmix, [w_in], [BF16], tm_l, name="lru_in_lat")
            p_c, = _ln_matmul(xc, mod, layer, ctx_idx, gmix, [w_in], [BF16], tm_c, name="lru_in_ctx")
            h0 = jnp.zeros((bsz, d), F32)
            fwd = (lru_conv_w[j], lru_conv_b[j], wcat_f, 0.5 * lru_ba_f[j], 0.5 * lru_bx_f[j], lru_lam_f[j])
            bwd = (lru_conv_w[j], lru_conv_b[j], wcat_b, 0.5 * lru_ba_b[j], 0.5 * lru_bx_b[j], lru_lam_b[j])
            tt_c, tt_l = _tile(n_ctx, 128), _tile(seq, 128)
            p_c3, p_l3 = p_c.reshape(bsz, n_ctx, 2 * d), p_l.reshape(bsz, seq, 2 * d)
            hcf, s_cf = _lru_scan(p_c3, *fwd, h0, tt_c, False)
            hcb, s_cb = _lru_scan(p_c3, *bwd, h0, tt_c, True)
            hlf, _ = _lru_scan(p_l3, *fwd, s_cf, tt_l, False)
            hlb, _ = _lru_scan(p_l3, *bwd, s_cb, tt_l, True)

            def lru_out(hf, hb, p2d, xs, midx, tm):
                blk = lambda i: (i, 0)
                spec = lambda: pl.BlockSpec((tm, d), blk)
                return _proj_res("lru", [hf.reshape(-1, d), hb.reshape(-1, d), p2d], [spec(), spec(), spec()],
                                 w_o, xs, mod, layer, midx, tm)

            xl = lru_out(hlf, hlb, p_l, xl, lat_idx, tm_l)
            if need_ctx:
                xc = lru_out(hcf, hcb, p_c, xc, ctx_idx, tm_c)

        else:
            qw = Q_HEADS * HEAD_DIM
            w_q = attn_w_in[j][:, :qw].astype(BF16)
            w_kv = attn_w_in[j][:, qw:].astype(BF16)
            w_o = attn_w_o[j].astype(BF16)
            qg, kg = attn_q_g[j], attn_k_g[j]
            q_l, kv_l = _attn_in(xl, mod, layer, lat_idx, gmix, w_q, w_kv, qg, kg, tm_l, _rope_tables(seq), seq,
                                 "attn_in_lat")
            q_c, kv_c = _attn_in(xc, mod, layer, ctx_idx, gmix, w_q, w_kv, qg, kg, tm_c, None, n_ctx, "attn_in_ctx")
            tk = _tile(seq // PV_SPLITS, 512)
            o_l = _attention(q_l, kv_c, kv_l, bsz, _tile(seq, 128), tk)
            half = lambda tm: [pl.BlockSpec((tm, d // KV_HEADS), lambda i: (i, 0))] * KV_HEADS
            xl = _proj_res("plain", o_l, half(tm_l), w_o, xl, mod, layer, lat_idx, tm_l)
            if need_ctx:
                o_c = _attention(q_c, kv_c, None, bsz, _tile(n_ctx, 128), tk)
                xc = _proj_res("plain", o_c, half(tm_c), w_o, xc, mod, layer, ctx_idx, tm_c)

        w1, w2 = w_mlp1[layer].astype(BF16), w_mlp2[layer].astype(BF16)
        tf = _tile(w1.shape[1], 2048)
        tmm_l = _tile(seq, 1024)
        xl = _mlp(xl, mod, layer, lambda i: 1 + (i * tmm_l) // seq, norm_mlp_g[layer], w1, w2, tmm_l, tf,
                  final_gain=None if need_ctx else final_g)
        if need_ctx:
            xc = _mlp(xc, mod, layer, ctx_idx, norm_mlp_g[layer], w1, w2, _tile(bsz * n_ctx, 1024), tf)

    return xl.reshape(bsz, seq, d)
```

```python
import functools

import numpy as np
import jax
import jax.numpy as jnp
from jax import lax
from jax.experimental import pallas as pl
from jax.experimental.pallas import tpu as pltpu

F32 = jnp.float32
BF16 = jnp.bfloat16

EPS = 1e-6
N_MOD = 6
N_MIXERS = 3
MOD_ROWS = 16

GLA_HEADS = 4
GLA_DK = 128
GLA_DV = 256
GLA_GATE_RANK = 16
GLA_GATE_TAU = 16.0
GLA_GATE_PAD = 128

RNN_BLOCKS = 8
RNN_BLOCK_DIM = 128
CONV_WIDTH = 4
CONV_LEFT = 2
LRU_C = 8.0

HEAD_DIM = 128
Q_HEADS = 8
KV_HEADS = 2
GROUP = Q_HEADS // KV_HEADS
GRID_W = 64
ROPE_THETA = 10000.0

VMEM_LIMIT_BYTES = 56 * 1024 * 1024


def _cparams(*sem):
    return pltpu.CompilerParams(dimension_semantics=sem, vmem_limit_bytes=VMEM_LIMIT_BYTES)


def _dot(a, b):
    return jnp.dot(a, b, preferred_element_type=F32)


def _dot_nt(a, b):
    return lax.dot_general(a, b, (((1,), (1,)), ((), ())), preferred_element_type=F32)


def _dot_tn(a, b):
    return lax.dot_general(a, b, (((0,), (0,)), ((), ())), preferred_element_type=F32)


def _rms(x, g):
    return x * lax.rsqrt(jnp.mean(x * x, axis=-1, keepdims=True) + EPS) * g


def _adaln(x, g, shift, scale):
    return _rms(x, g) * (1.0 + scale) + shift


def _mod_kernel(s_ref, w_ref, b_ref, o_ref):
    s = s_ref[...]
    s = s * jax.nn.sigmoid(s)
    o_ref[...] = _dot(s.astype(BF16), w_ref[...].astype(BF16)) + b_ref[...]


def _modulation(c, c_ctx, w_mod, b_mod):
    depth, d, nd = w_mod.shape
    bsz = c.shape[0]
    rows = jnp.concatenate([c_ctx[None, :], c, jnp.zeros((MOD_ROWS - 1 - bsz, d), F32)], axis=0)
    tn = min(nd, 1536)
    out = pl.pallas_call(
        _mod_kernel,
        grid=(depth, nd // tn),
        in_specs=[pl.BlockSpec((MOD_ROWS, d), lambda l, j: (0, 0)),
                  pl.BlockSpec((None, d, tn), lambda l, j: (l, 0, j)),
                  pl.BlockSpec((None, 1, tn), lambda l, j: (l, 0, j))],
        out_specs=pl.BlockSpec((None, MOD_ROWS, tn), lambda l, j: (l, 0, j)),
        out_shape=jax.ShapeDtypeStruct((depth, MOD_ROWS, nd), F32),
        compiler_params=_cparams("arbitrary", "arbitrary"),
        name="modulation",
    )(rows, w_mod, b_mod.reshape(depth, 1, nd))
    return out.reshape(depth, MOD_ROWS, N_MOD, d)


def _mod_spec(layer, d, mod_index):
    return pl.BlockSpec((None, None, N_MOD, d), lambda i, *_: (layer, mod_index(i), 0, 0))


def _ln_matmul_kernel(x_ref, mod_ref, g_ref, *refs, n_out, mod_off, col_chunk):
    w_refs, o_refs = refs[:n_out], refs[n_out:]
    h = _adaln(x_ref[...], g_ref[...], mod_ref[mod_off:mod_off + 1, :],
               mod_ref[mod_off + 1:mod_off + 2, :]).astype(BF16)
    for w_ref, o_ref in zip(w_refs, o_refs):
        n = w_ref.shape[1]
        for n0 in range(0, n, col_chunk):
            n1 = min(n, n0 + col_chunk)
            o_ref[:, n0:n1] = _dot(h, w_ref[:, n0:n1]).astype(o_ref.dtype)


def _ln_matmul(x2d, mod, layer, mod_index, gain, ws, out_dtypes, tm, name="ln_matmul"):
    rows, d = x2d.shape
    in_specs = [pl.BlockSpec((tm, d), lambda i: (i, 0)), _mod_spec(layer, d, mod_index),
                pl.BlockSpec((1, d), lambda i: (0, 0))]
    in_specs += [pl.BlockSpec(w.shape, lambda i: (0, 0)) for w in ws]
    return pl.pallas_call(
        functools.partial(_ln_matmul_kernel, n_out=len(ws), mod_off=0, col_chunk=512),
        grid=(rows // tm,),
        in_specs=in_specs,
        out_specs=[pl.BlockSpec((tm, w.shape[1]), lambda i: (i, 0)) for w in ws],
        out_shape=[jax.ShapeDtypeStruct((rows, w.shape[1]), dt) for w, dt in zip(ws, out_dtypes)],
        compiler_params=_cparams("arbitrary"),
        name=name,
    )(x2d, mod, gain.reshape(1, d), *ws)


def _proj_res_kernel(y0_ref, y1_ref, w_ref, x_ref, mod_ref, o_ref):
    y = jnp.concatenate([y0_ref[...], y1_ref[...]], axis=1)
    o_ref[...] = x_ref[...] + mod_ref[2:3, :] * _dot(y, w_ref[...])


def _gla_proj_res_kernel(of_ref, ob_ref, r_ref, ng_ref, w_ref, x_ref, mod_ref, o_ref):
    o = of_ref[...].astype(F32) + ob_ref[...].astype(F32)
    r = r_ref[...].astype(F32)
    ys = []
    for h in range(GLA_HEADS):
        sl = slice(h * GLA_DV, (h + 1) * GLA_DV)
        rh = r[:, sl]
        ys.append(_rms(o[:, sl], ng_ref[...]) * (rh * jax.nn.sigmoid(rh)))
    y = jnp.concatenate(ys, axis=1).astype(BF16)
    o_ref[...] = x_ref[...] + mod_ref[2:3, :] * _dot(y, w_ref[...])


def _lru_proj_res_kernel(hf_ref, hb_ref, gate_ref, w_ref, x_ref, mod_ref, o_ref):
    h = hf_ref[...].astype(F32) + hb_ref[...].astype(F32)
    y = (h * jax.nn.gelu(gate_ref[...].astype(F32))).astype(BF16)
    o_ref[...] = x_ref[...] + mod_ref[2:3, :] * _dot(y, w_ref[...])


def _proj_res(kind, operands, operand_specs, w, x2d, mod, layer, mod_index, tm, extra=(), extra_specs=()):
    rows, d = x2d.shape
    body = {"plain": _proj_res_kernel, "gla": _gla_proj_res_kernel, "lru": _lru_proj_res_kernel}[kind]
    n_in = len(operands) + len(extra) + 1
    return pl.pallas_call(
        body,
        grid=(rows // tm,),
        in_specs=[*operand_specs, *extra_specs, pl.BlockSpec(w.shape, lambda i: (0, 0)),
                  pl.BlockSpec((tm, d), lambda i: (i, 0)), _mod_spec(layer, d, mod_index)],
        out_specs=pl.BlockSpec((tm, d), lambda i: (i, 0)),
        out_shape=jax.ShapeDtypeStruct((rows, d), F32),
        input_output_aliases={n_in: 0} if layer > 0 else {},
        compiler_params=_cparams("arbitrary"),
        name=kind + "_proj_res",
    )(*operands, *extra, w, x2d, mod)


def _mlp_kernel(x_ref, mod_ref, g_ref, w1_ref, w2_ref, *rest, final_norm):
    if final_norm:
        fg_ref, o_ref, h_sc, acc_sc = rest
    else:
        o_ref, h_sc, acc_sc = rest
    j = pl.program_id(1)

    @pl.when(j == 0)
    def _():
        h_sc[...] = _adaln(x_ref[...], g_ref[...], mod_ref[3:4, :], mod_ref[4:5, :]).astype(BF16)
        acc_sc[...] = jnp.zeros_like(acc_sc)

    a = jnp.maximum(_dot(h_sc[...], w1_ref[...]), 0.0)
    acc_sc[...] += _dot((a * a).astype(BF16), w2_ref[...])

    @pl.when(j == pl.num_programs(1) - 1)
    def _():
        y = x_ref[...] + mod_ref[5:6, :] * acc_sc[...]
        o_ref[...] = _rms(y, fg_ref[...]) if final_norm else y


def _mlp(x2d, mod, layer, mod_index, gain, w1, w2, tm, tf, final_gain=None):
    rows, d = x2d.shape
    dff = w1.shape[1]
    vec = lambda: pl.BlockSpec((1, d), lambda i, j: (0, 0))
    in_specs = [pl.BlockSpec((tm, d), lambda i, j: (i, 0)), _mod_spec(layer, d, mod_index), vec(),
                pl.BlockSpec((d, tf), lambda i, j: (0, j)),
                pl.BlockSpec((tf, d), lambda i, j: (j, 0))]
    args = [x2d, mod, gain.reshape(1, d), w1, w2]
    if final_gain is not None:
        in_specs.append(vec())
        args.append(final_gain.reshape(1, d))
    return pl.pallas_call(
        functools.partial(_mlp_kernel, final_norm=final_gain is not None),
        grid=(rows // tm, dff // tf),
        in_specs=in_specs,
        out_specs=pl.BlockSpec((tm, d), lambda i, j: (i, 0)),
        out_shape=jax.ShapeDtypeStruct((rows, d), F32),
        scratch_shapes=[pltpu.VMEM((tm, d), BF16), pltpu.VMEM((tm, d), F32)],
        input_output_aliases={0: 0},
        compiler_params=_cparams("arbitrary", "arbitrary"),
        name="mlp",
    )(*args)


GLA_CHUNK = 256
GLA_MATMUL_LEVEL_ROWS = 8
GLA_FAST_MAX_DECAY = 60.0


def _gla_tables(chunk, reverse):
    levels = int(np.log2(chunk))
    idx = np.arange(chunk)
    t, r = idx[:, None], idx[None, :]
    if not reverse:
        q_incl = (r <= t)
        k_rest = (r > t)
    else:
        q_incl = (r >= t)
        k_rest = (r < t)
    mats = [q_incl, k_rest]
    masks = [np.eye(chunk, dtype=bool)]
    for lv in range(1, levels + 1):
        m = 1 << (lv - 1)
        mid = (idx // (2 * m)) * (2 * m) + m
        upper = idx >= mid
        midc = mid[:, None]
        if not reverse:
            pat = np.where(upper[:, None], (r >= midc) & (r <= t), (r > t) & (r < midc))
            mask = (upper[:, None] & ~upper[None, :])
        else:
            pat = np.where(upper[:, None], (r >= midc) & (r < t), (r >= t) & (r < midc))
            mask = (~upper[:, None] & upper[None, :])
        mask = mask & ((idx[:, None] // (2 * m)) == (idx[None, :] // (2 * m)))
        if 2 * m <= GLA_MATMUL_LEVEL_ROWS:
            mats.append(pat)
        masks.append(mask)
    a = np.concatenate(mats, axis=0).astype(np.float32)
    a2 = np.concatenate([a, a], axis=1)
    return jnp.asarray(a2, BF16), jnp.asarray(np.stack(masks).astype(np.float32), F32)


def _gla_gates(qkvr_ref, gf_ref, wup_ref, bup_ref, a_ref, reverse):
    chunk = qkvr_ref.shape[0]
    dq = GLA_HEADS * GLA_DK
    graw = _dot(gf_ref[...], wup_ref[...]) + bup_ref[...]
    g = (jnp.minimum(graw, 0.0) - jnp.log(1.0 + jnp.exp(-jnp.abs(graw)))) * (1.0 / GLA_GATE_TAU)
    g_hi = g.astype(BF16)
    g_cat = jnp.concatenate([g_hi, (g - g_hi.astype(F32)).astype(BF16)], axis=0)

    def partial_sums(block):
        return _dot(a_ref[block * chunk:(block + 1) * chunk, :], g_cat)

    st = dict(partial_sums=partial_sums, bq=partial_sums(0), reverse=reverse, chunk=chunk,
              q_all=qkvr_ref[:, 0:dq].astype(F32) * (GLA_DK ** -0.5), k_all=qkvr_ref[:, dq:2 * dq].astype(F32))
    st["qd"] = (st["q_all"] * jnp.exp(st["bq"])).astype(BF16)
    st["btot"] = st["bq"][0:1, :] if reverse else st["bq"][chunk - 1:chunk, :]
    return st


def _gla_levels(st, z_sc, lv_range):
    chunk, reverse, bq, bk = st["chunk"], st["reverse"], st["bq"], st["bk"]
    q_all, k_all = st["q_all"], st["k_all"]
    row = lax.broadcasted_iota(jnp.int32, (chunk, 1), 0)
    for lv in lv_range:
        m = 1 << (lv - 1)
        if 2 * m <= GLA_MATMUL_LEVEL_ROWS:
            use_q = ((row // m) % 2) == (0 if reverse else 1)
            z = jnp.where(use_q, q_all, k_all) * jnp.exp(st["partial_sums"](1 + lv))
        else:
            pieces = []
            for lo in range(0, chunk, 2 * m):
                mid, hi = lo + m, lo + 2 * m
                if not reverse:
                    pieces.append(k_all[lo:mid] * jnp.exp(bk[lo:mid] - bk[mid - 1:mid]))
                    pieces.append(q_all[mid:hi] * jnp.exp(bq[mid:hi] - bq[mid - 1:mid]))
                else:
                    pieces.append(q_all[lo:mid] * jnp.exp(bq[lo:mid] - bq[mid:mid + 1]))
                    pieces.append(k_all[mid:hi] * jnp.exp(bk[mid:hi] - bk[mid:mid + 1]))
            z = jnp.concatenate(pieces, axis=0)
        z_sc[lv - 1] = z.astype(BF16)


def _gla_scores_by_level(st, h, mask_ref, z_sc, levels):
    sl = slice(h * GLA_DK, (h + 1) * GLA_DK)
    chunk, reverse = st["chunk"], st["reverse"]
    sub = 8
    diag = mask_ref[0] * jnp.sum(st["q_all"][:, sl] * st["k_all"][:, sl], axis=-1, keepdims=True)
    tiles = [diag[r:r + sub] for r in range(0, chunk, sub)]
    for lv in range(1, levels + 1):
        m = 1 << (lv - 1)
        if 2 * m <= GLA_MATMUL_LEVEL_ROWS:
            q_rows = [(0, chunk)]
        else:
            q_rows = [(lo, lo + m) if reverse else (lo + m, lo + 2 * m) for lo in range(0, chunk, 2 * m)]
        zq = jnp.concatenate([z_sc[lv - 1, lo:hi, sl] for lo, hi in q_rows], axis=0)
        mq = jnp.concatenate([mask_ref[lv, lo:hi, :] for lo, hi in q_rows], axis=0)
        part = mq * _dot_nt(zq, z_sc[lv - 1, :, sl])
        src = 0
        for lo, hi in q_rows:
            for r in range(lo, hi, sub):
                tiles[r // sub] = tiles[r // sub] + part[src:src + sub]
                src += sub
    return jnp.concatenate(tiles, axis=0)


def _gla_head_update(st, h, scores, qkvr_ref, s_sc, o_ref):
    dq = GLA_HEADS * GLA_DK
    sl = slice(h * GLA_DK, (h + 1) * GLA_DK)
    vh = qkvr_ref[:, 2 * dq + h * GLA_DV:2 * dq + (h + 1) * GLA_DV]
    s_old = s_sc[h]
    o_ref[:, h * GLA_DV:(h + 1) * GLA_DV] = (_dot(scores.astype(BF16), vh)
                                            + _dot(st["qd"][:, sl], s_old.astype(BF16))).astype(o_ref.dtype)
    etot = jnp.exp(st["btot"][:, sl])
    ecol = jnp.transpose(jnp.broadcast_to(etot, (GLA_DK, GLA_DK)))
    s_sc[h] = s_old * jnp.concatenate([ecol] * (GLA_DV // GLA_DK), axis=1) + _dot_tn(st["kd"][:, sl], vh)


def _gla_kernel(qkvr_f, gf_f, qkvr_b, gf_b, wup_f, wup_b, bup_f, bup_b, a_f, a_b, mask_f, mask_b, pair_f, pair_b,
                s0_f, s0_b, o_f, o_b, sfin_f, sfin_b, s_sc_f, s_sc_b, z_sc_f, z_sc_b):
    c = pl.program_id(1)
    levels = mask_f.shape[0] - 1
    dirs = ((qkvr_f, mask_f, pair_f, z_sc_f, s_sc_f, o_f), (qkvr_b, mask_b, pair_b, z_sc_b, s_sc_b, o_b))

    @pl.when(c == 0)
    def _():
        s_sc_f[...] = s0_f[...]
        s_sc_b[...] = s0_b[...]

    sts = (_gla_gates(qkvr_f, gf_f, wup_f, bup_f, a_f, False), _gla_gates(qkvr_b, gf_b, wup_b, bup_b, a_b, True))
    worst = jnp.maximum(jnp.max(jnp.abs(sts[0]["btot"])), jnp.max(jnp.abs(sts[1]["btot"])))
    small_decay = worst <= GLA_FAST_MAX_DECAY

    @pl.when(small_decay)
    def _():
        kinv = []
        for st in sts:
            kexp = st["k_all"] * jnp.exp(-st["bq"])
            kinv.append(kexp.astype(BF16))
            st["kd"] = (kexp * jnp.exp(st["btot"])).astype(BF16)
        for h in range(GLA_HEADS):
            sl = slice(h * GLA_DK, (h + 1) * GLA_DK)
            for st, kv, (qkvr, _, pair, _, s_sc, o_ref) in zip(sts, kinv, dirs):
                scores = pair[...] * _dot_nt(st["qd"][:, sl], kv[:, sl])
                _gla_head_update(st, h, scores, qkvr, s_sc, o_ref)

    @pl.when(jnp.logical_not(small_decay))
    def _():
        for st, (_, _, _, z_sc, _, _) in zip(sts, dirs):
            st["bk"] = st["partial_sums"](1)
            st["kd"] = (st["k_all"] * jnp.exp(st["bk"])).astype(BF16)
            _gla_levels(st, z_sc, range(1, levels + 1))
        for h in range(GLA_HEADS):
            for st, (qkvr, mask, _, z_sc, s_sc, o_ref) in zip(sts, dirs):
                _gla_head_update(st, h, _gla_scores_by_level(st, h, mask, z_sc, levels), qkvr, s_sc, o_ref)

    @pl.when(c == pl.num_programs(1) - 1)
    def _():
        sfin_f[...] = s_sc_f[...]
        sfin_b[...] = s_sc_b[...]


def _gla_scan(qkvr, gfeat, wup_f, wup_b, bup_f, bup_b, s0_f, s0_b, bsz, chunk):
    rows = qkvr.shape[0]
    nc = rows // bsz // chunk
    a_f, mask_f = _gla_tables(chunk, False)
    a_b, mask_b = _gla_tables(chunk, True)
    tri = np.tril(np.ones((chunk, chunk), np.float32))
    pair_f, pair_b = jnp.asarray(tri), jnp.asarray(tri.T)
    dv = GLA_HEADS * GLA_DV
    fwd_block = lambda b, c: (b * nc + c, 0)
    bwd_block = lambda b, c: (b * nc + nc - 1 - c, 0)
    const2 = lambda x: pl.BlockSpec(x.shape, lambda b, c: (0, 0))
    const3 = lambda x: pl.BlockSpec(x.shape, lambda b, c: (0, 0, 0))
    state_spec = pl.BlockSpec((None, GLA_HEADS, GLA_DK, GLA_DV), lambda b, c: (b, 0, 0, 0))
    state_shape = jax.ShapeDtypeStruct((bsz, GLA_HEADS, GLA_DK, GLA_DV), F32)
    state_sc = lambda: pltpu.VMEM((GLA_HEADS, GLA_DK, GLA_DV), F32)
    z_sc = lambda: pltpu.VMEM((mask_f.shape[0] - 1, chunk, GLA_HEADS * GLA_DK), BF16)
    return pl.pallas_call(
        _gla_kernel,
        grid=(bsz, nc),
        in_specs=[pl.BlockSpec((chunk, qkvr.shape[1]), fwd_block), pl.BlockSpec((chunk, gfeat.shape[1]), fwd_block),
                  pl.BlockSpec((chunk, qkvr.shape[1]), bwd_block), pl.BlockSpec((chunk, gfeat.shape[1]), bwd_block),
                  const2(wup_f), const2(wup_b), const2(bup_f), const2(bup_b), const2(a_f), const2(a_b),
                  const3(mask_f), const3(mask_b), const2(pair_f), const2(pair_b), state_spec, state_spec],
        out_specs=[pl.BlockSpec((chunk, dv), fwd_block), pl.BlockSpec((chunk, dv), bwd_block), state_spec, state_spec],
        out_shape=[jax.ShapeDtypeStruct((rows, dv), BF16), jax.ShapeDtypeStruct((rows, dv), BF16),
                   state_shape, state_shape],
        scratch_shapes=[state_sc(), state_sc(), z_sc(), z_sc()],
        compiler_params=_cparams("arbitrary", "arbitrary"),
        name="gla_scan",
    )(qkvr, gfeat, qkvr, gfeat, wup_f, wup_b, bup_f, bup_b, a_f, a_b, mask_f, mask_b, pair_f, pair_b, s0_f, s0_b)


LRU_HALO = 16
CONV_SHIFTS = tuple(j - CONV_LEFT for j in range(CONV_WIDTH) if j != CONV_LEFT)
F32_TINY = float(np.finfo(np.float32).tiny)


def _lru_shift_matrix(tt):
    sel = np.zeros((len(CONV_SHIFTS) * tt, tt + 2 * LRU_HALO), np.float32)
    for i, off in enumerate(CONV_SHIFTS):
        sel[i * tt + np.arange(tt), LRU_HALO + np.arange(tt) + off] = 1.0
    return jnp.asarray(sel, BF16)


def _lru_kernel(x_ref, prev_ref, next_ref, shift_ref, cw_ref, cb_ref, wcat_ref, ba_ref, bx_ref, lam_ref, h0_ref,
                hs_ref, hfin_ref, xs_sc, xc_sc, a_sc, u_sc, hs_sc, h_sc, *, reverse):
    i = pl.program_id(0)
    n = pl.num_programs(0)
    tidx = n - 1 - i if reverse else i
    bsz, tt, width = x_ref.shape
    bd = RNN_BLOCK_DIM

    @pl.when(i == 0)
    def _():
        h_sc[...] = h0_ref[...]

    xs_sc[:, 0:LRU_HALO, :] = jnp.where(tidx > 0, prev_ref[...], jnp.zeros_like(prev_ref))
    xs_sc[:, LRU_HALO:LRU_HALO + tt, :] = x_ref[...]
    xs_sc[:, LRU_HALO + tt:2 * LRU_HALO + tt, :] = jnp.where(tidx < n - 1, next_ref[...], jnp.zeros_like(next_ref))
    for b in range(bsz):
        shifted = _dot(shift_ref[...], xs_sc[b])
        xc = cb_ref[...] + cw_ref[CONV_LEFT:CONV_LEFT + 1, :] * x_ref[b].astype(F32)
        for k, off in enumerate(CONV_SHIFTS):
            j = off + CONV_LEFT
            xc = xc + cw_ref[j:j + 1, :] * shifted[k * tt:(k + 1) * tt]
        xc_sc[b] = xc

    lam = lam_ref[...]
    neg4sp = -0.5 * LRU_C * (jnp.maximum(-lam, 0.0) + jnp.log1p(jnp.exp(-jnp.abs(lam))))
    for nb in range(RNN_BLOCKS):
        sl = slice(nb * bd, (nb + 1) * bd)
        xc2 = xc_sc[:, :, sl].reshape(bsz * tt, bd)
        ri = _dot(xc2.astype(BF16), wcat_ref[nb])
        log_a = neg4sp[:, sl] + neg4sp[:, sl] * jnp.tanh(ri[:, :bd] + ba_ref[:, sl])
        ig = 0.5 + 0.5 * jnp.tanh(ri[:, bd:] + bx_ref[:, sl])
        th = jnp.tanh(log_a)
        one_minus_a2 = (-2.0 * th) / (1.0 - th)
        root = one_minus_a2 * lax.rsqrt(jnp.maximum(one_minus_a2, F32_TINY))
        a = jnp.exp(log_a)
        u = root * (ig * xc2)
        for b in range(bsz):
            a_sc[nb, pl.ds(b, tt, stride=bsz), :] = a[b * tt:(b + 1) * tt]
            u_sc[nb, pl.ds(b, tt, stride=bsz), :] = u[b * tt:(b + 1) * tt]

    def step(k, hs):
        t = tt - 1 - k if reverse else k
        rows = pl.ds(pl.multiple_of(t * bsz, bsz), bsz)
        new = []
        for nb in range(RNN_BLOCKS):
            h = a_sc[nb, rows, :] * hs[nb] + u_sc[nb, rows, :]
            hs_sc[nb, rows, :] = h
            new.append(h)
        return tuple(new)

    h_init = tuple(h_sc[:, nb * bd:(nb + 1) * bd] for nb in range(RNN_BLOCKS))
    h_last = lax.fori_loop(0, tt, step, h_init, unroll=4)
    for nb in range(RNN_BLOCKS):
        sl = slice(nb * bd, (nb + 1) * bd)
        h_sc[:, sl] = h_last[nb]
        for b in range(bsz):
            hs_ref[b, :, sl] = hs_sc[nb, pl.ds(b, tt, stride=bsz), :].astype(hs_ref.dtype)

    @pl.when(i == n - 1)
    def _():
        hfin_ref[...] = h_sc[...]


def _lru_scan(p3, conv_w, conv_b, wcat, ba, bx, lam, h0, tt, reverse):
    bsz, seq, two_w = p3.shape
    width = two_w // 2
    n = seq // tt
    hb = tt // LRU_HALO
    assert tt % LRU_HALO == 0 and width == RNN_BLOCKS * RNN_BLOCK_DIM
    shift = _lru_shift_matrix(tt)

    def tix(i):
        return n - 1 - i if reverse else i

    vec = lambda: pl.BlockSpec((1, width), lambda i: (0, 0))
    slab = lambda: pltpu.VMEM((RNN_BLOCKS, tt * bsz, RNN_BLOCK_DIM), F32)
    return pl.pallas_call(
        functools.partial(_lru_kernel, reverse=reverse),
        grid=(n,),
        in_specs=[pl.BlockSpec((bsz, tt, width), lambda i: (0, tix(i), 1)),
                  pl.BlockSpec((bsz, LRU_HALO, width), lambda i: (0, jnp.maximum(tix(i) * hb - 1, 0), 1)),
                  pl.BlockSpec((bsz, LRU_HALO, width), lambda i: (0, jnp.minimum((tix(i) + 1) * hb, seq // LRU_HALO - 1), 1)),
                  pl.BlockSpec(shift.shape, lambda i: (0, 0)),
                  pl.BlockSpec((CONV_WIDTH, width), lambda i: (0, 0)), vec(),
                  pl.BlockSpec(wcat.shape, lambda i: (0, 0, 0)), vec(), vec(), vec(),
                  pl.BlockSpec((bsz, width), lambda i: (0, 0))],
        out_specs=[pl.BlockSpec((bsz, tt, width), lambda i: (0, tix(i), 0)),
                   pl.BlockSpec((bsz, width), lambda i: (0, 0))],
        out_shape=[jax.ShapeDtypeStruct((bsz, seq, width), BF16),
                   jax.ShapeDtypeStruct((bsz, width), F32)],
        scratch_shapes=[pltpu.VMEM((bsz, tt + 2 * LRU_HALO, width), BF16), pltpu.VMEM((bsz, tt, width), F32),
                        slab(), slab(), slab(), pltpu.VMEM((bsz, width), F32)],
        compiler_params=_cparams("arbitrary"),
        name="lru_scan_bwd" if reverse else "lru_scan_fwd",
    )(p3, p3, p3, shift, conv_w, conv_b.reshape(1, width), wcat, ba.reshape(1, width),
      bx.reshape(1, width), lam.reshape(1, width), h0)


def _rope(x, cos, sin):
    return x * cos + pltpu.roll(x, HEAD_DIM // 2, axis=1) * sin


KV_OUT_WIDTH = 3 * KV_HEADS * HEAD_DIM
PV_SPLITS = 4


def _attn_in_kernel(x_ref, mod_ref, g_ref, wq_ref, wkv_ref, qg_ref, kg_ref, seg_ref, *rest, rope):
    if rope:
        cos_ref, sin_ref, q_ref, kv_ref = rest
    else:
        q_ref, kv_ref = rest
    h = _adaln(x_ref[...], g_ref[...], mod_ref[0:1, :], mod_ref[1:2, :]).astype(BF16)
    pair_w = 2 * HEAD_DIM

    def head_pair(x2, gain, scale):
        sq = x2 * x2
        sq_hi = sq.astype(BF16)
        ss = _dot(sq_hi, seg_ref[...]) + _dot((sq - sq_hi.astype(F32)).astype(BF16), seg_ref[...])
        g2 = jnp.concatenate([gain, gain], axis=1)
        xn = x2 * lax.rsqrt(ss * (1.0 / HEAD_DIM) + EPS) * g2
        if rope:
            xn = jnp.concatenate([_rope(xn[:, i * HEAD_DIM:(i + 1) * HEAD_DIM], cos_ref[...], sin_ref[...])
                                  for i in range(2)], axis=1)
        return (xn * scale).astype(BF16) if scale != 1.0 else xn.astype(BF16)

    group_w = GROUP * HEAD_DIM
    q_groups = [_dot(h, wq_ref[:, n0:n0 + group_w]) for n0 in range(0, Q_HEADS * HEAD_DIM, group_w)]
    kv = _dot(h, wkv_ref[...])
    for gi, res in enumerate(q_groups):
        for n0 in range(0, group_w, pair_w):
            q_ref[:, gi * group_w + n0:gi * group_w + n0 + pair_w] = head_pair(res[:, n0:n0 + pair_w], qg_ref[...],
                                                                               HEAD_DIM ** -0.5)
    kw = KV_HEADS * HEAD_DIM
    assert kw == pair_w
    kv_ref[:, 0:kw] = head_pair(kv[:, 0:kw], kg_ref[...], 1.0)
    ones = jnp.ones((kv.shape[0], HEAD_DIM), BF16)
    for j in range(KV_HEADS):
        v0 = kw + 2 * j * HEAD_DIM
        kv_ref[:, v0:v0 + HEAD_DIM] = kv[:, kw + j * HEAD_DIM:kw + (j + 1) * HEAD_DIM].astype(BF16)
        kv_ref[:, v0 + HEAD_DIM:v0 + 2 * HEAD_DIM] = ones


def _attn_in(x2d, mod, layer, mod_index, gain, w_q, w_kv, q_g, k_g, tm, rope_tables, seq, name):
    rows, d = x2d.shape
    vec = lambda: pl.BlockSpec((1, HEAD_DIM), lambda i: (0, 0))
    in_specs = [pl.BlockSpec((tm, d), lambda i: (i, 0)), _mod_spec(layer, d, mod_index),
                pl.BlockSpec((1, d), lambda i: (0, 0)),
                pl.BlockSpec(w_q.shape, lambda i: (0, 0)), pl.BlockSpec(w_kv.shape, lambda i: (0, 0)), vec(), vec(),
                pl.BlockSpec((2 * HEAD_DIM, 2 * HEAD_DIM), lambda i: (0, 0))]
    seg = jnp.asarray(np.kron(np.eye(2, dtype=np.float32), np.ones((HEAD_DIM, HEAD_DIM), np.float32)), BF16)
    args = [x2d, mod, gain.reshape(1, d), w_q, w_kv, q_g.reshape(1, HEAD_DIM), k_g.reshape(1, HEAD_DIM), seg]
    if rope_tables is not None:
        tpb = seq // tm
        in_specs += [pl.BlockSpec((tm, HEAD_DIM), lambda i: (i % tpb, 0))] * 2
        args += list(rope_tables)
    return pl.pallas_call(
        functools.partial(_attn_in_kernel, rope=rope_tables is not None),
        grid=(rows // tm,),
        in_specs=in_specs,
        out_specs=[pl.BlockSpec((tm, w_q.shape[1]), lambda i: (i, 0)),
                   pl.BlockSpec((tm, KV_OUT_WIDTH), lambda i: (i, 0))],
        out_shape=[jax.ShapeDtypeStruct((rows, w_q.shape[1]), BF16),
                   jax.ShapeDtypeStruct((rows, KV_OUT_WIDTH), BF16)],
        compiler_params=_cparams("arbitrary"),
        name=name,
    )(*args)


def _attn_kernel(q_ref, kvc_ref, *rest, has_latent, tk):
    if has_latent:
        kvl_ref, o0_ref, o1_ref, s0_sc, s1_sc, p0_sc, p1_sc, m1_sc = rest
    else:
        o0_ref, o1_ref, s0_sc, s1_sc, p0_sc, p1_sc, m1_sc = rest
    tq = q_ref.shape[0]
    rows = GROUP * tq
    kw = KV_HEADS * HEAD_DIM
    lanes = HEAD_DIM
    n_ctx = kvc_ref.shape[0]
    n_lat = kvl_ref.shape[0] if has_latent else 0

    @pl.when((pl.program_id(0) == 0) & (pl.program_id(1) == 0))
    def _():
        s1_sc[...] = jnp.zeros_like(s1_sc)
        m1_sc[...] = jnp.zeros_like(m1_sc)

    def lane_groups(s):
        return [s[:, c * lanes:(c + 1) * lanes] for c in range(s.shape[1] // lanes)]

    tkc = min(tk, n_ctx)
    tiles = [(kvc_ref, r0, tkc, r0) for r0 in range(0, n_ctx, tkc)]
    tiles += [(kvl_ref, r0, tk, n_ctx + r0) for r0 in range(0, n_lat, tk)]
    pv_parts = [(kvc_ref, 0, n_ctx, 0)]
    pv_parts += [(kvl_ref, r0, n_lat // PV_SPLITS, n_ctx + r0) for r0 in range(0, n_lat, max(n_lat // PV_SPLITS, 1))]

    def stage(hk_new, s_new_sc, hk_old, s_old_sc, m_old, p_sc, o_ref):
        ksl = slice(hk_new * HEAD_DIM, (hk_new + 1) * HEAD_DIM)
        vsl = slice(kw + 2 * hk_old * HEAD_DIM, kw + 2 * (hk_old + 1) * HEAD_DIM)
        qcat = jnp.concatenate([q_ref[:, (hk_new * GROUP + gi) * HEAD_DIM:(hk_new * GROUP + gi + 1) * HEAD_DIM]
                                for gi in range(GROUP)], axis=0)
        mrun = jnp.full((rows, lanes), -jnp.inf, F32)
        partial, pending = [], list(pv_parts)
        for kv_ref, r0, w, c0 in tiles:
            s = _dot_nt(qcat, kv_ref[r0:r0 + w, ksl])
            s_new_sc[:, c0:c0 + w] = s
            mrun = functools.reduce(jnp.maximum, [mrun] + lane_groups(s))
            p = [jnp.exp((sg - m_old).astype(BF16)) for sg in lane_groups(s_old_sc[:, c0:c0 + w])]
            p_sc[:, c0:c0 + w] = jnp.concatenate(p, axis=1)
            while pending and pending[0][3] + pending[0][2] <= c0 + w:
                pv_ref, pr0, pw, pc0 = pending.pop(0)
                partial.append(_dot(p_sc[:, pc0:pc0 + pw], pv_ref[pr0:pr0 + pw, vsl]))
        acc = functools.reduce(jnp.add, partial)
        o = acc[:, 0:HEAD_DIM] / acc[:, HEAD_DIM:HEAD_DIM + 1]
        for gi in range(GROUP):
            o_ref[:, gi * HEAD_DIM:(gi + 1) * HEAD_DIM] = o[gi * tq:(gi + 1) * tq].astype(o_ref.dtype)
        return jnp.broadcast_to(jnp.max(mrun, axis=-1, keepdims=True), (rows, lanes))

    m0 = stage(0, s0_sc, 1, s1_sc, m1_sc[...], p1_sc, o1_ref)
    m1_sc[...] = stage(1, s1_sc, 0, s0_sc, m0, p0_sc, o0_ref)


def _attention(q, kv_ctx, kv_lat, bsz, tq, tk):
    rows = q.shape[0]
    nq = rows // bsz // tq
    n_ctx = kv_ctx.shape[0] // bsz
    has_latent = kv_lat is not None
    n_keys = n_ctx + (kv_lat.shape[0] // bsz if has_latent else 0)
    assert n_ctx % min(tk, n_ctx) == 0 and (n_keys - n_ctx) % (tk * PV_SPLITS) == 0
    in_specs = [pl.BlockSpec((tq, q.shape[1]), lambda b, j: (b * nq + jnp.minimum(j, nq - 1), 0)),
                pl.BlockSpec((n_ctx, kv_ctx.shape[1]), lambda b, j: (b, 0))]
    args = [q, kv_ctx]
    if has_latent:
        in_specs.append(pl.BlockSpec((n_keys - n_ctx, kv_lat.shape[1]), lambda b, j: (b, 0)))
        args.append(kv_lat)
    group_w = GROUP * HEAD_DIM
    scores = lambda: pltpu.VMEM((GROUP * tq, n_keys), F32)
    probs = lambda: pltpu.VMEM((GROUP * tq, n_keys), BF16)
    return pl.pallas_call(
        functools.partial(_attn_kernel, has_latent=has_latent, tk=tk),
        grid=(bsz, nq + 1),
        in_specs=in_specs,
        out_specs=[pl.BlockSpec((tq, group_w), lambda b, j: (b * nq + jnp.minimum(j, nq - 1), 0)),
                   pl.BlockSpec((tq, group_w), lambda b, j: (b * nq + jnp.maximum(j - 1, 0), 0))],
        out_shape=[jax.ShapeDtypeStruct((rows, group_w), BF16), jax.ShapeDtypeStruct((rows, group_w), BF16)],
        scratch_shapes=[scores(), scores(), probs(), probs(), pltpu.VMEM((GROUP * tq, HEAD_DIM), F32)],
        compiler_params=_cparams("arbitrary", "arbitrary"),
        name="attention" if has_latent else "attention_ctx",
    )(*args)


def _rope_tables(seq):
    n_rows = seq // GRID_W
    row = jnp.repeat(jnp.arange(n_rows), GRID_W)
    col = jnp.tile(jnp.arange(GRID_W), n_rows)
    n_pairs_axis = HEAD_DIM // 4
    inv_freq = ROPE_THETA ** (-jnp.arange(n_pairs_axis, dtype=F32) / n_pairs_axis)
    ang = jnp.concatenate([row[:, None] * inv_freq, col[:, None] * inv_freq], axis=-1)
    cos, sin = jnp.cos(ang), jnp.sin(ang)
    return jnp.concatenate([cos, cos], axis=-1), jnp.concatenate([-sin, sin], axis=-1)


def _tile(n, pref):
    t = min(n, pref)
    assert n % t == 0, (n, pref)
    return t


def kernel(x, c, ctx, c_ctx, norm_mix_g, norm_mlp_g, w_mod, b_mod, w_mlp1, w_mlp2, gla_w_in, gla_w_up_f, gla_b_f, gla_w_up_b, gla_b_b, gla_norm_g, gla_w_o, lru_w_in, lru_conv_w, lru_conv_b, lru_wa_f, lru_ba_f, lru_wx_f, lru_bx_f, lru_lam_f, lru_wa_b, lru_ba_b, lru_wx_b, lru_bx_b, lru_lam_b, lru_w_o, attn_w_in, attn_q_g, attn_k_g, attn_w_o, final_g):
    bsz, seq, d = x.shape
    n_ctx = ctx.shape[1]
    depth = w_mod.shape[0]
    assert bsz + 1 <= MOD_ROWS and d == GLA_HEADS * GLA_DV == Q_HEADS * HEAD_DIM

    mod = _modulation(c, c_ctx, w_mod, b_mod)
    xl = x.reshape(bsz * seq, d)
    xc = ctx.reshape(bsz * n_ctx, d)

    tm_l = _tile(seq, 1024)
    tm_c = _tile(bsz * n_ctx, 1024)
    lat_idx = lambda i: 1 + (i * tm_l) // seq
    ctx_idx = lambda i: 0

    for layer in range(depth):
        need_ctx = layer < depth - 1
        kind, j = layer % N_MIXERS, layer // N_MIXERS
        gmix = norm_mix_g[layer]

        if kind == 0:
            dq = GLA_HEADS * GLA_DK
            main_w = gla_w_in[j][:, :2 * dq + 2 * d].astype(BF16)
            gate_w = jnp.pad(gla_w_in[j][:, 2 * dq + 2 * d:], ((0, 0), (0, GLA_GATE_PAD - 2 * GLA_GATE_RANK))).astype(BF16)
            zpad = jnp.zeros((GLA_GATE_PAD - 2 * GLA_GATE_RANK, dq), F32)
            wup_f = jnp.concatenate([gla_w_up_f[j], jnp.zeros_like(gla_w_up_b[j]), zpad], axis=0).astype(BF16)
            wup_b = jnp.concatenate([jnp.zeros_like(gla_w_up_f[j]), gla_w_up_b[j], zpad], axis=0).astype(BF16)
            bup_f, bup_b = gla_b_f[j].reshape(1, dq), gla_b_b[j].reshape(1, dq)
            w_o = gla_w_o[j].astype(BF16)
            proj = {}
            for name, (xs, midx, tm, slen) in zip(("lat", "ctx"), [(xl, lat_idx, tm_l, seq), (xc, ctx_idx, tm_c, n_ctx)]):
                proj[name] = _ln_matmul(xs, mod, layer, midx, gmix, [main_w, gate_w], [BF16, BF16], tm,
                                        name="gla_in_" + name)
            s0 = jnp.zeros((bsz, GLA_HEADS, GLA_DK, GLA_DV), F32)
            ch_c, ch_l = _tile(n_ctx, GLA_CHUNK), _tile(seq, GLA_CHUNK)
            ocf, ocb, s_cf, s_cb = _gla_scan(*proj["ctx"], wup_f, wup_b, bup_f, bup_b, s0, s0, bsz, ch_c)
            olf, olb, _, _ = _gla_scan(*proj["lat"], wup_f, wup_b, bup_f, bup_b, s_cf, s_cb, bsz, ch_l)
            ng = gla_norm_g[j].reshape(1, GLA_DV)

            def gla_out(of, ob, qkvr, xs, midx, tm):
                blk = lambda i: (i, 0)
                return _proj_res("gla", [of, ob, qkvr], [pl.BlockSpec((tm, d), blk), pl.BlockSpec((tm, d), blk),
                                                        pl.BlockSpec((tm, d), lambda i: (i, (2 * dq + d) // d))],
                                 w_o, xs, mod, layer, midx, tm, extra=[ng],
                                 extra_specs=[pl.BlockSpec((1, GLA_DV), lambda i: (0, 0))])

            xl = gla_out(olf, olb, proj["lat"][0], xl, lat_idx, tm_l)
            if need_ctx:
                xc = gla_out(ocf, ocb, proj["ctx"][0], xc, ctx_idx, tm_c)

        elif kind == 1:
            w_in = lru_w_in[j].astype(BF16)
            w_o = lru_w_o[j].astype(BF16)
            wcat_f = (0.5 * jnp.concatenate([lru_wa_f[j], lru_wx_f[j]], axis=-1)).astype(BF16)
            wcat_b = (0.5 * jnp.concatenate([lru_wa_b[j], lru_wx_b[j]], axis=-1)).astype(BF16)
            p_l, = _ln_matmul(xl, mod, layer, lat_idx, gmix, [w_in], [BF16], tm_l, name="lru_in_lat")
            p_c, = _ln_matmul(xc, mod, layer, ctx_idx, gmix, [w_in], [BF16], tm_c, name="lru_in_ctx")
            h0 = jnp.zeros((bsz, d), F32)
            fwd = (lru_conv_w[j], lru_conv_b[j], wcat_f, 0.5 * lru_ba_f[j], 0.5 * lru_bx_f[j], lru_lam_f[j])
            bwd = (lru_conv_w[j], lru_conv_b[j], wcat_b, 0.5 * lru_ba_b[j], 0.5 * lru_bx_b[j], lru_lam_b[j])
            tt_c, tt_l = _tile(n_ctx, 128), _tile(seq, 128)
            p_c3, p_l3 = p_c.reshape(bsz, n_ctx, 2 * d), p_l.reshape(bsz, seq, 2 * d)
            hcf, s_cf = _lru_scan(p_c3, *fwd, h0, tt_c, False)
            hcb, s_cb = _lru_scan(p_c3, *bwd, h0, tt_c, True)
            hlf, _ = _lru_scan(p_l3, *fwd, s_cf, tt_l, False)
            hlb, _ = _lru_scan(p_l3, *bwd, s_cb, tt_l, True)

            def lru_out(hf, hb, p2d, xs, midx, tm):
                blk = lambda i: (i, 0)
                spec = lambda: pl.BlockSpec((tm, d), blk)
                return _proj_res("lru", [hf.reshape(-1, d), hb.reshape(-1, d), p2d], [spec(), spec(), spec()],
                                 w_o, xs, mod, layer, midx, tm)

            xl = lru_out(hlf, hlb, p_l, xl, lat_idx, tm_l)
            if need_ctx:
                xc = lru_out(hcf, hcb, p_c, xc, ctx_idx, tm_c)

        else:
            qw = Q_HEADS * HEAD_DIM
            w_q = attn_w_in[j][:, :qw].astype(BF16)
            w_kv = attn_w_in[j][:, qw:].astype(BF16)
            w_o = attn_w_o[j].astype(BF16)
            qg, kg = attn_q_g[j], attn_k_g[j]
            q_l, kv_l = _attn_in(xl, mod, layer, lat_idx, gmix, w_q, w_kv, qg, kg, tm_l, _rope_tables(seq), seq,
                                 "attn_in_lat")
            q_c, kv_c = _attn_in(xc, mod, layer, ctx_idx, gmix, w_q, w_kv, qg, kg, tm_c, None, n_ctx, "attn_in_ctx")
            tk = _tile(seq // PV_SPLITS, 512)
            o_l = _attention(q_l, kv_c, kv_l, bsz, _tile(seq, 128), tk)
            half = lambda tm: [pl.BlockSpec((tm, d // KV_HEADS), lambda i: (i, 0))] * KV_HEADS
            xl = _proj_res("plain", o_l, half(tm_l), w_o, xl, mod, layer, lat_idx, tm_l)
            if need_ctx:
                o_c = _attention(q_c, kv_c, None, bsz, _tile(n_ctx, 128), tk)
                xc = _proj_res("plain", o_c, half(tm_c), w_o, xc, mod, layer, ctx_idx, tm_c)

        w1, w2 = w_mlp1[layer].astype(BF16), w_mlp2[layer].astype(BF16)
        tf = _tile(w1.shape[1], 2048)
        tmm_l = _tile(seq, 1024)
        xl = _mlp(xl, mod, layer, lambda i: 1 + (i * tmm_l) // seq, norm_mlp_g[layer], w1, w2, tmm_l, tf,
                  final_gain=None if need_ctx else final_g)
        if need_ctx:
            xc = _mlp(xc, mod, layer, ctx_idx, norm_mlp_g[layer], w1, w2, _tile(bsz * n_ctx, 1024), tf)

    return xl.reshape(bsz, seq, d)
```
